```python
import math
import jax
import jax.numpy as jnp
from jax import lax
import numpy as np

D_MODEL = 2048
BATCH = 4
SEQ = 4096
DEPTH = 4

CHUNK = 128
EPS = 1e-6
D_FF = 4 * D_MODEL
N_RET_LAYERS = (DEPTH + 1) // 2
N_MLSTM_LAYERS = DEPTH // 2

RET_HEADS = D_MODEL // 256
RET_DK = D_MODEL // RET_HEADS
RET_QK = RET_HEADS * RET_DK
RET_V = 2 * D_MODEL
RET_DV = RET_V // RET_HEADS
RET_IN = 2 * RET_QK + 2 * RET_V
ROPE_BASE = 10000.0

ML_HEADS = 8
ML_QK = D_MODEL // 2
ML_DQK = ML_QK // ML_HEADS
ML_V = D_MODEL
ML_DV = ML_V // ML_HEADS
ML_IN = 2 * ML_QK + 2 * ML_V + 2 * ML_HEADS
GATE_SOFTCAP = 15.0

kernel_name = "hybrid_retention_mlstm_sqrelu_sandwich"


def rms_norm(x, g):
    xf = x.astype(jnp.float32)
    y = xf * lax.rsqrt(jnp.mean(xf * xf, axis=-1, keepdims=True) + EPS)
    return (y * g.astype(jnp.float32)).astype(x.dtype)


def rotary(t, cos, sin):
    t1, t2 = jnp.split(t, 2, axis=-1)
    return jnp.concatenate([t1 * cos - t2 * sin, t2 * cos + t1 * sin], axis=-1)


def to_chunks(t):
    b, s, h, d = t.shape
    return t.reshape(b, s // CHUNK, CHUNK, h, d).transpose(1, 0, 3, 2, 4)


def gate_chunks(t):
    b, s, h = t.shape
    return t.reshape(b, s // CHUNK, CHUNK, h).transpose(1, 0, 3, 2)


def from_chunks(t):
    nc, b, h, c, d = t.shape
    return t.transpose(1, 0, 3, 2, 4).reshape(b, nc * c, h, d)


def retention(h, w_in, w_out, cos, sin):
    b, s, _ = h.shape
    proj = h @ w_in
    q, k, v, g = jnp.split(proj, [RET_QK, 2 * RET_QK, 2 * RET_QK + RET_V], axis=-1)
    q = rotary(q.reshape(b, s, RET_HEADS, RET_DK).astype(jnp.float32), cos, sin)
    k = rotary(k.reshape(b, s, RET_HEADS, RET_DK).astype(jnp.float32), cos, sin) * (RET_DK ** -0.5)
    v = v.reshape(b, s, RET_HEADS, RET_DV).astype(jnp.float32)

    log_gamma = jnp.log1p(-jnp.power(2.0, -5.0 - jnp.arange(RET_HEADS, dtype=jnp.float32)))
    idx = jnp.arange(CHUNK, dtype=jnp.float32)
    rel = idx[:, None] - idx[None, :]
    decay = jnp.where(rel >= 0, jnp.exp(jnp.maximum(rel, 0.0) * log_gamma[:, None, None]), 0.0)
    xi = jnp.exp((idx + 1.0) * log_gamma[:, None])[..., None]
    zeta = jnp.exp((CHUNK - 1.0 - idx) * log_gamma[:, None])[..., None]
    g_chunk = jnp.exp(CHUNK * log_gamma)[:, None, None]

    def step(state, xs):
        qc, kc, vc = xs
        scores = jnp.einsum('bhnd,bhmd->bhnm', qc, kc) * decay
        inner = jnp.einsum('bhnm,bhme->bhne', scores, vc)
        cross = jnp.einsum('bhnd,bhde->bhne', qc, state) * xi
        new_state = state * g_chunk + jnp.einsum('bhmd,bhme->bhde', kc * zeta, vc)
        return new_state, inner + cross

    state0 = jnp.zeros((b, RET_HEADS, RET_DK, RET_DV), jnp.float32)
    _, y = lax.scan(step, state0, (to_chunks(q), to_chunks(k), to_chunks(v)))
    y = from_chunks(y)
    mu = jnp.mean(y, axis=-1, keepdims=True)
    var = jnp.mean(jnp.square(y - mu), axis=-1, keepdims=True)
    y = ((y - mu) * lax.rsqrt(var + EPS)).reshape(b, s, RET_V).astype(h.dtype)
    return (jax.nn.silu(g) * y) @ w_out


def mlstm(h, w_in, b_gate, norm_g, w_out):
    b, s, _ = h.shape
    proj = h @ w_in
    q, k, v, o, gates = jnp.split(
        proj, [ML_QK, 2 * ML_QK, 2 * ML_QK + ML_V, 2 * ML_QK + 2 * ML_V], axis=-1)
    gates = gates.astype(jnp.float32) + b_gate.astype(jnp.float32)
    gates = GATE_SOFTCAP * jnp.tanh(gates / GATE_SOFTCAP)
    i_log = gates[..., :ML_HEADS]
    f_log = jax.nn.log_sigmoid(gates[..., ML_HEADS:])
    q = q.reshape(b, s, ML_HEADS, ML_DQK).astype(jnp.float32)
    k = k.reshape(b, s, ML_HEADS, ML_DQK).astype(jnp.float32) * (ML_DQK ** -0.5)
    v = v.reshape(b, s, ML_HEADS, ML_DV).astype(jnp.float32)
    causal = jnp.tril(jnp.ones((CHUNK, CHUNK), dtype=bool))

    def step(carry, xs):
        c_st, n_st, m_st = carry
        qc, kc, vc, ic, fc = xs
        bcum = jnp.cumsum(fc, axis=-1)
        d_log = bcum[..., :, None] - bcum[..., None, :] + ic[..., None, :]
        d_log = jnp.where(causal, d_log, -jnp.inf)
        inter_log = bcum + m_st[..., None]
        m_out = jnp.maximum(inter_log, jnp.max(d_log, axis=-1))
        d_w = jnp.exp(d_log - m_out[..., None])
        inter_w = jnp.exp(inter_log - m_out)
        scores = jnp.einsum('bhnd,bhsd->bhns', qc, kc) * d_w
        num = jnp.einsum('bhns,bhse->bhne', scores, vc) \
            + inter_w[..., None] * jnp.einsum('bhnd,bhde->bhne', qc, c_st)
        den = jnp.sum(scores, axis=-1) + inter_w * jnp.einsum('bhnd,bhd->bhn', qc, n_st)
        h_out = num / jnp.maximum(jnp.abs(den), jnp.exp(-m_out))[..., None]
        b_tot = bcum[..., -1]
        w_log = b_tot[..., None] - bcum + ic
        m_new = jnp.maximum(b_tot + m_st, jnp.max(w_log, axis=-1))
        w_s = jnp.exp(w_log - m_new[..., None])
        carry_decay = jnp.exp(b_tot + m_st - m_new)
        kw = kc * w_s[..., None]
        c_new = carry_decay[..., None, None] * c_st + jnp.einsum('bhsd,bhse->bhde', kw, vc)
        n_new = carry_decay[..., None] * n_st + jnp.sum(kw, axis=2)
        return (c_new, n_new, m_new), h_out

    carry0 = (jnp.zeros((b, ML_HEADS, ML_DQK, ML_DV), jnp.float32),
              jnp.zeros((b, ML_HEADS, ML_DQK), jnp.float32),
              jnp.zeros((b, ML_HEADS), jnp.float32))
    _, y = lax.scan(step, carry0, (to_chunks(q), to_chunks(k), to_chunks(v),
                                   gate_chunks(i_log), gate_chunks(f_log)))
    y = from_chunks(y)
    y = y * lax.rsqrt(jnp.mean(y * y, axis=-1, keepdims=True) + EPS)
    y = (y.reshape(b, s, ML_V) * norm_g.astype(jnp.float32)).astype(h.dtype)
    return (y * jax.nn.sigmoid(o)) @ w_out


def sqrelu_mlp(h, w1, w2):
    return jnp.square(jax.nn.relu(h @ w1)) @ w2


def setup_inputs(seed: int = 0) -> dict:
    key = jax.random.key(seed)
    ks = jax.random.split(key, 12)
    f32 = jnp.float32
    x = jax.random.normal(ks[0], (BATCH, SEQ, D_MODEL), f32)
    positions = jnp.broadcast_to(jnp.arange(SEQ, dtype=jnp.int32), (BATCH, SEQ))
    norm_g = 1.0 + 0.05 * jax.random.normal(ks[1], (DEPTH, 4, D_MODEL), f32)
    ret_w_in = jax.random.normal(ks[2], (N_RET_LAYERS, D_MODEL, RET_IN), f32) * D_MODEL ** -0.5
    ret_w_out = jax.random.normal(ks[3], (N_RET_LAYERS, RET_V, D_MODEL), f32) * RET_V ** -0.5
    mlstm_w_in = jax.random.normal(ks[4], (N_MLSTM_LAYERS, D_MODEL, ML_IN), f32) * D_MODEL ** -0.5
    i_bias = 0.1 * jax.random.normal(ks[5], (N_MLSTM_LAYERS, ML_HEADS), f32)
    f_bias = jnp.linspace(3.0, 6.0, ML_HEADS, dtype=f32)[None, :] \
        + 0.1 * jax.random.normal(ks[6], (N_MLSTM_LAYERS, ML_HEADS), f32)
    mlstm_b_gate = jnp.concatenate([i_bias, f_bias], axis=-1)
    mlstm_norm_g = 1.0 + 0.05 * jax.random.normal(ks[7], (N_MLSTM_LAYERS, ML_V), f32)
    mlstm_w_out = jax.random.normal(ks[8], (N_MLSTM_LAYERS, ML_V, D_MODEL), f32) * ML_V ** -0.5
    mlp_w1 = jax.random.normal(ks[9], (DEPTH, D_MODEL, D_FF), f32) * D_MODEL ** -0.5
    mlp_w2 = jax.random.normal(ks[10], (DEPTH, D_FF, D_MODEL), f32) * D_FF ** -0.5
    return {"x": x, "positions": positions, "norm_g": norm_g,
            "ret_w_in": ret_w_in, "ret_w_out": ret_w_out,
            "mlstm_w_in": mlstm_w_in, "mlstm_b_gate": mlstm_b_gate,
            "mlstm_norm_g": mlstm_norm_g, "mlstm_w_out": mlstm_w_out,
            "mlp_w1": mlp_w1, "mlp_w2": mlp_w2}


def reference(x, positions, norm_g, ret_w_in, ret_w_out, mlstm_w_in, mlstm_b_gate,
              mlstm_norm_g, mlstm_w_out, mlp_w1, mlp_w2):
    inv_freq = jnp.power(ROPE_BASE, -jnp.linspace(0.0, 1.0, RET_DK // 2, dtype=jnp.float32))
    ang = positions.astype(jnp.float32)[..., None, None] * inv_freq
    cos, sin = jnp.cos(ang), jnp.sin(ang)
    for i in range(DEPTH):
        g = norm_g[i]
        h = rms_norm(x, g[0])
        if i % 2 == 0:
            j = i // 2
            y = retention(h, ret_w_in[j], ret_w_out[j], cos, sin)
        else:
            j = i // 2
            y = mlstm(h, mlstm_w_in[j], mlstm_b_gate[j], mlstm_norm_g[j], mlstm_w_out[j])
        x = x + rms_norm(y, g[1])
        y = sqrelu_mlp(rms_norm(x, g[2]), mlp_w1[i], mlp_w2[i])
        x = x + rms_norm(y, g[3])
    return x
```

```python
import functools
import math

import jax
import jax.numpy as jnp
from jax import lax
from jax.experimental import pallas as pl
from jax.experimental.pallas import tpu as pltpu

F32 = jnp.float32
BF16 = jnp.bfloat16

D_MODEL = 2048
DEPTH = 4
CHUNK = 128
EPS = 1e-6
D_FF = 4 * D_MODEL

RET_HEADS = 8
RET_DK = 256
RET_QK = RET_HEADS * RET_DK
RET_V = 2 * D_MODEL
RET_DV = RET_V // RET_HEADS
RET_IN = 2 * RET_QK + 2 * RET_V
ROPE_BASE = 10000.0

ML_HEADS = 8
ML_QK = D_MODEL // 2
ML_DQK = ML_QK // ML_HEADS
ML_V = D_MODEL
ML_DV = ML_V // ML_HEADS
ML_MAIN = 2 * ML_QK + 2 * ML_V
GATE_SOFTCAP = 15.0
GATE_LANES = 128

VMEM_LIMIT = 56 * 1024 * 1024

NT_DIMS = (((1,), (1,)), ((), ()))
TN_DIMS = (((0,), (0,)), ((), ()))


def _rms(x, g):
    ms = jnp.mean(x * x, axis=-1, keepdims=True)
    return x * lax.rsqrt(ms + EPS) * g


def _params(sem):
    return pltpu.CompilerParams(dimension_semantics=sem, vmem_limit_bytes=VMEM_LIMIT)


def _rope_kernel(pos_ref, freq_ref, cos_ref, sin_ref):
    ang = pos_ref[...].astype(F32) * freq_ref[...]
    cos_ref[...] = jnp.cos(ang)
    sin_ref[...] = jnp.sin(ang)


def rope_tables(positions, inv_freq, tm=2048):
    m = positions.size
    pos = positions.reshape(m, 1)
    half = inv_freq.shape[-1]
    return pl.pallas_call(
        _rope_kernel,
        grid=(m // tm,),
        in_specs=[pl.BlockSpec((tm, 1), lambda i: (i, 0)),
                  pl.BlockSpec((1, half), lambda i: (0, 0))],
        out_specs=[pl.BlockSpec((tm, half), lambda i: (i, 0)),
                   pl.BlockSpec((tm, half), lambda i: (i, 0))],
        out_shape=[jax.ShapeDtypeStruct((m, half), F32)] * 2,
        compiler_params=_params(("parallel",)),
        name="rope_tables",
    )(pos, inv_freq.reshape(1, half))


def _ret_proj_kernel(x_ref, g_ref, w_ref, cos_ref, sin_ref, o_ref, h_ref, *, tn):
    j = pl.program_id(1)
    n_rot = 2 * RET_QK // tn

    @pl.when(j == 0)
    def _():
        h_ref[...] = _rms(x_ref[...], g_ref[...]).astype(BF16)

    res = jnp.dot(h_ref[...], w_ref[...], preferred_element_type=F32)

    @pl.when(j < n_rot)
    def _():
        scale = jnp.where(j >= n_rot // 2, RET_DK ** -0.5, 1.0).astype(F32)
        c = cos_ref[...] * scale
        s = sin_ref[...] * scale
        half = RET_DK // 2
        for hh in range(tn // RET_DK):
            lo = hh * RET_DK
            t1 = res[:, lo:lo + half]
            t2 = res[:, lo + half:lo + RET_DK]
            o_ref[:, lo:lo + half] = (t1 * c - t2 * s).astype(BF16)
            o_ref[:, lo + half:lo + RET_DK] = (t2 * c + t1 * s).astype(BF16)

    @pl.when(j >= n_rot)
    def _():
        o_ref[...] = res.astype(BF16)


def ret_proj(x, g, w, cos, sin, tm=1024, tn=1024):
    m = x.shape[0]
    return pl.pallas_call(
        functools.partial(_ret_proj_kernel, tn=tn),
        grid=(m // tm, RET_IN // tn),
        in_specs=[pl.BlockSpec((tm, D_MODEL), lambda i, j: (i, 0)),
                  pl.BlockSpec((1, D_MODEL), lambda i, j: (0, 0)),
                  pl.BlockSpec((D_MODEL, tn), lambda i, j: (0, j)),
                  pl.BlockSpec((tm, RET_DK // 2), lambda i, j: (i, 0)),
                  pl.BlockSpec((tm, RET_DK // 2), lambda i, j: (i, 0))],
        out_specs=pl.BlockSpec((tm, tn), lambda i, j: (i, j)),
        out_shape=jax.ShapeDtypeStruct((m, RET_IN), BF16),
        scratch_shapes=[pltpu.VMEM((tm, D_MODEL), BF16)],
        compiler_params=_params(("parallel", "arbitrary")),
        name="ret_proj",
    )(x, g, w, cos, sin)


def _ret_log_gamma(h):
    return math.log1p(-(2.0 ** (-5.0 - h)))


def _ret_kernel(q_ref, k_ref, v_ref, g_ref, y_ref, st_ref, dec_ref, xi_ref, zeta_ref):
    c = pl.program_id(1)

    @pl.when(c == 0)
    def _():
        st_ref[...] = jnp.zeros_like(st_ref)
        n = lax.broadcasted_iota(jnp.int32, (CHUNK, CHUNK), 0)
        mcol = lax.broadcasted_iota(jnp.int32, (CHUNK, CHUNK), 1)
        rel = (n - mcol).astype(F32)
        row = lax.broadcasted_iota(jnp.int32, (CHUNK, RET_DK), 0).astype(F32)
        for h in range(RET_HEADS):
            lg = _ret_log_gamma(h)
            dec_ref[h] = jnp.where(rel >= 0, jnp.exp(jnp.maximum(rel, 0.0) * lg), 0.0)
            xi_ref[h] = jnp.exp((row + 1.0) * lg)
            zeta_ref[h] = jnp.exp((CHUNK - 1.0 - row) * lg)

    for h in range(RET_HEADS):
        g_chunk = math.exp(CHUNK * _ret_log_gamma(h))
        qh = q_ref[:, h * RET_DK:(h + 1) * RET_DK]
        kh = k_ref[:, h * RET_DK:(h + 1) * RET_DK]
        vh = v_ref[:, h * RET_DV:(h + 1) * RET_DV]
        gh = g_ref[:, h * RET_DV:(h + 1) * RET_DV].astype(F32)
        st = st_ref[h]

        scores = lax.dot_general(qh, kh, NT_DIMS, preferred_element_type=F32) * dec_ref[h]
        qx = (qh.astype(F32) * xi_ref[h]).astype(BF16)
        lhs = jnp.concatenate([scores.astype(BF16), qx], axis=1)
        rhs = jnp.concatenate([vh, st.astype(BF16)], axis=0)
        y = jnp.dot(lhs, rhs, preferred_element_type=F32)

        kz = (kh.astype(F32) * zeta_ref[h]).astype(BF16)
        st_ref[h] = st * g_chunk + lax.dot_general(kz, vh, TN_DIMS, preferred_element_type=F32)

        mu = jnp.mean(y, axis=-1, keepdims=True)
        yc = y - mu
        var = jnp.mean(yc * yc, axis=-1, keepdims=True)
        yn = yc * lax.rsqrt(var + EPS)
        y_ref[:, h * RET_DV:(h + 1) * RET_DV] = (gh * jax.nn.sigmoid(gh) * yn).astype(BF16)


def ret_mix(proj, batch, seq):
    m = batch * seq
    nc = seq // CHUNK
    row = lambda b, c: b * nc + c
    return pl.pallas_call(
        _ret_kernel,
        grid=(batch, nc),
        in_specs=[pl.BlockSpec((CHUNK, RET_QK), lambda b, c: (row(b, c), 0)),
                  pl.BlockSpec((CHUNK, RET_QK), lambda b, c: (row(b, c), 1)),
                  pl.BlockSpec((CHUNK, RET_V), lambda b, c: (row(b, c), 1)),
                  pl.BlockSpec((CHUNK, RET_V), lambda b, c: (row(b, c), 2))],
        out_specs=pl.BlockSpec((CHUNK, RET_V), lambda b, c: (row(b, c), 0)),
        out_shape=jax.ShapeDtypeStruct((m, RET_V), BF16),
        scratch_shapes=[pltpu.VMEM((RET_HEADS, RET_DK, RET_DV), F32),
                        pltpu.VMEM((RET_HEADS, CHUNK, CHUNK), F32),
                        pltpu.VMEM((RET_HEADS, CHUNK, RET_DK), F32),
                        pltpu.VMEM((RET_HEADS, CHUNK, RET_DK), F32)],
        compiler_params=_params(("parallel", "arbitrary")),
        name="ret_mix",
    )(proj, proj, proj, proj)


def _log_sigmoid(x):
    return jnp.minimum(x, 0.0) - jnp.log1p(jnp.exp(-jnp.abs(x)))


def _gate_logs(pre, is_forget):
    capped = GATE_SOFTCAP * jnp.tanh(pre / GATE_SOFTCAP)
    return jnp.where(is_forget, _log_sigmoid(capped), capped)


def _ml_proj_kernel(x_ref, g_ref, w_ref, wg_ref, wgt_ref, b_ref, bt_ref,
                    o_ref, gcol_ref, grow_ref, h_ref):
    j = pl.program_id(1)

    @pl.when(j == 0)
    def _():
        h = _rms(x_ref[...], g_ref[...]).astype(BF16)
        h_ref[...] = h
        pc = jnp.dot(h, wg_ref[...], preferred_element_type=F32) + b_ref[...]
        lane = lax.broadcasted_iota(jnp.int32, pc.shape, 1)
        gcol_ref[...] = _gate_logs(pc, lane >= ML_HEADS)
        pr = lax.dot_general(wgt_ref[...], h, NT_DIMS, preferred_element_type=F32) + bt_ref[...]
        sub = lax.broadcasted_iota(jnp.int32, pr.shape, 0)
        grow_ref[...] = _gate_logs(pr, sub >= ML_HEADS)

    o_ref[...] = jnp.dot(h_ref[...], w_ref[...], preferred_element_type=F32).astype(BF16)


def ml_proj(x, g, w, wg, wgt, b, bt, tm=1024, tn=1024):
    m = x.shape[0]
    ng = 2 * ML_HEADS
    return pl.pallas_call(
        _ml_proj_kernel,
        grid=(m // tm, ML_MAIN // tn),
        in_specs=[pl.BlockSpec((tm, D_MODEL), lambda i, j: (i, 0)),
                  pl.BlockSpec((1, D_MODEL), lambda i, j: (0, 0)),
                  pl.BlockSpec((D_MODEL, tn), lambda i, j: (0, j)),
                  pl.BlockSpec((D_MODEL, GATE_LANES), lambda i, j: (0, 0)),
                  pl.BlockSpec((ng, D_MODEL), lambda i, j: (0, 0)),
                  pl.BlockSpec((1, GATE_LANES), lambda i, j: (0, 0)),
                  pl.BlockSpec((ng, 1), lambda i, j: (0, 0))],
        out_specs=[pl.BlockSpec((tm, tn), lambda i, j: (i, j)),
                   pl.BlockSpec((tm, GATE_LANES), lambda i, j: (i, 0)),
                   pl.BlockSpec((ng, tm), lambda i, j: (0, i))],
        out_shape=[jax.ShapeDtypeStruct((m, ML_MAIN), BF16),
                   jax.ShapeDtypeStruct((m, GATE_LANES), F32),
                   jax.ShapeDtypeStruct((ng, m), F32)],
        scratch_shapes=[pltpu.VMEM((tm, D_MODEL), BF16)],
        compiler_params=_params(("parallel", "arbitrary")),
        name="ml_proj",
    )(x, g, w, wg, wgt, b, bt)


def _ml_kernel(q_ref, k_ref, v_ref, o_ref, gcol_ref, grow_ref, ng_ref, y_ref,
               c_ref, n_ref, m_ref):
    c = pl.program_id(1)

    @pl.when(c == 0)
    def _():
        c_ref[...] = jnp.zeros_like(c_ref)
        n_ref[...] = jnp.zeros_like(n_ref)
        m_ref[...] = jnp.zeros_like(m_ref)

    n_idx = lax.broadcasted_iota(jnp.int32, (CHUNK, CHUNK), 0)
    s_idx = lax.broadcasted_iota(jnp.int32, (CHUNK, CHUNK), 1)
    causal = n_idx >= s_idx
    lower = causal.astype(F32)
    upper = (n_idx <= s_idx).astype(F32)

    gcol = gcol_ref[...]
    grow = grow_ref[...]
    hi = lax.Precision.HIGHEST
    cum_col = jnp.dot(lower, gcol, precision=hi, preferred_element_type=F32)
    cum_row = jnp.dot(grow, upper, precision=hi, preferred_element_type=F32)
    scale = ML_DQK ** -0.5

    for h in range(ML_HEADS):
        qh = q_ref[:, h * ML_DQK:(h + 1) * ML_DQK]
        kh = k_ref[:, h * ML_DQK:(h + 1) * ML_DQK]
        vh = v_ref[:, h * ML_DV:(h + 1) * ML_DV]
        oh = o_ref[:, h * ML_DV:(h + 1) * ML_DV].astype(F32)
        bcol = cum_col[:, ML_HEADS + h:ML_HEADS + h + 1]
        icol = gcol[:, h:h + 1]
        brow = cum_row[ML_HEADS + h:ML_HEADS + h + 1, :]
        irow = grow[h:h + 1, :]
        c_st = c_ref[h]
        n_st = n_ref[h]
        m_st = m_ref[h][0:1, 0:1]

        d_log = jnp.where(causal, bcol - brow + irow, -jnp.inf)
        inter_log = bcol + m_st
        m_out = jnp.maximum(inter_log, jnp.max(d_log, axis=-1, keepdims=True))
        d_w = jnp.exp(d_log - m_out)
        inter_w = jnp.exp(inter_log - m_out)
        qf = qh.astype(F32)
        scores = lax.dot_general(qh, kh, NT_DIMS, preferred_element_type=F32) * (d_w * scale)
        lhs = jnp.concatenate([scores.astype(BF16), (qf * inter_w).astype(BF16)], axis=1)
        rhs = jnp.concatenate([vh, c_st.astype(BF16)], axis=0)
        num = jnp.dot(lhs, rhs, preferred_element_type=F32)
        den = jnp.sum(scores, axis=-1, keepdims=True) \
            + inter_w * jnp.sum(qf * n_st, axis=-1, keepdims=True)
        h_out = num / jnp.maximum(jnp.abs(den), jnp.exp(-m_out))

        b_tot = bcol[CHUNK - 1:CHUNK, :]
        w_log = b_tot - bcol + icol
        m_new = jnp.maximum(b_tot + m_st, jnp.max(w_log, axis=0, keepdims=True))
        w_s = jnp.exp(w_log - m_new) * scale
        carry = jnp.exp(b_tot + m_st - m_new)
        kw = kh.astype(F32) * w_s
        c_ref[h] = carry * c_st + lax.dot_general(kw.astype(BF16), vh, TN_DIMS,
                                                  preferred_element_type=F32)
        n_ref[h] = carry * n_st + jnp.sum(kw, axis=0, keepdims=True)
        m_ref[h] = jnp.broadcast_to(m_new, m_ref.shape[1:])

        yn = h_out * lax.rsqrt(jnp.mean(h_out * h_out, axis=-1, keepdims=True) + EPS)
        yn = yn * ng_ref[:, h * ML_DV:(h + 1) * ML_DV]
        y_ref[:, h * ML_DV:(h + 1) * ML_DV] = (yn * jax.nn.sigmoid(oh)).astype(BF16)


def ml_mix(proj, gcol, grow, norm_g, batch, seq):
    m = batch * seq
    nc = seq // CHUNK
    row = lambda b, c: b * nc + c
    return pl.pallas_call(
        _ml_kernel,
        grid=(batch, nc),
        in_specs=[pl.BlockSpec((CHUNK, ML_QK), lambda b, c: (row(b, c), 0)),
                  pl.BlockSpec((CHUNK, ML_QK), lambda b, c: (row(b, c), 1)),
                  pl.BlockSpec((CHUNK, ML_V), lambda b, c: (row(b, c), 1)),
                  pl.BlockSpec((CHUNK, ML_V), lambda b, c: (row(b, c), 2)),
                  pl.BlockSpec((CHUNK, GATE_LANES), lambda b, c: (row(b, c), 0)),
                  pl.BlockSpec((2 * ML_HEADS, CHUNK), lambda b, c: (0, row(b, c))),
                  pl.BlockSpec((1, ML_V), lambda b, c: (0, 0))],
        out_specs=pl.BlockSpec((CHUNK, ML_V), lambda b, c: (row(b, c), 0)),
        out_shape=jax.ShapeDtypeStruct((m, ML_V), BF16),
        scratch_shapes=[pltpu.VMEM((ML_HEADS, ML_DQK, ML_DV), F32),
                        pltpu.VMEM((ML_HEADS, 1, ML_DQK), F32),
                        pltpu.VMEM((ML_HEADS, 8, 128), F32)],
        compiler_params=_params(("parallel", "arbitrary")),
        name="ml_mix",
    )(proj, proj, proj, proj, gcol, grow, norm_g)


def _out_kernel(y_ref, w_ref, g_ref, x_ref, o_ref, acc_ref, *, tn):
    j = pl.program_id(1)
    col = pl.multiple_of(j * tn, tn)
    acc_ref[:, pl.ds(col, tn)] = jnp.dot(y_ref[...], w_ref[...], preferred_element_type=F32)

    @pl.when(j == pl.num_programs(1) - 1)
    def _():
        o_ref[...] = x_ref[...] + _rms(acc_ref[...], g_ref[...])


def out_proj(y, w, g, x, tm=512, tn=512):
    m, kdim = y.shape
    return pl.pallas_call(
        functools.partial(_out_kernel, tn=tn),
        grid=(m // tm, D_MODEL // tn),
        in_specs=[pl.BlockSpec((tm, kdim), lambda i, j: (i, 0)),
                  pl.BlockSpec((kdim, tn), lambda i, j: (0, j)),
                  pl.BlockSpec((1, D_MODEL), lambda i, j: (0, 0)),
                  pl.BlockSpec((tm, D_MODEL), lambda i, j: (i, 0))],
        out_specs=pl.BlockSpec((tm, D_MODEL), lambda i, j: (i, 0)),
        out_shape=jax.ShapeDtypeStruct((m, D_MODEL), F32),
        scratch_shapes=[pltpu.VMEM((tm, D_MODEL), F32)],
        compiler_params=_params(("parallel", "arbitrary")),
        name="out_proj",
    )(y, w, g, x)


def _mlp_kernel(x_ref, g_in_ref, w1_ref, w2_ref, g_out_ref, o_ref, h_ref, acc_ref):
    f = pl.program_id(1)

    @pl.when(f == 0)
    def _():
        h_ref[...] = _rms(x_ref[...], g_in_ref[...]).astype(BF16)
        acc_ref[...] = jnp.zeros_like(acc_ref)

    hid = jnp.maximum(jnp.dot(h_ref[...], w1_ref[...], preferred_element_type=F32), 0.0)
    hid = (hid * hid).astype(BF16)
    acc_ref[...] += jnp.dot(hid, w2_ref[...], preferred_element_type=F32)

    @pl.when(f == pl.num_programs(1) - 1)
    def _():
        o_ref[...] = x_ref[...] + _rms(acc_ref[...], g_out_ref[...])


def mlp(x, g_in, w1, w2, g_out, tm=512, tf=512):
    m = x.shape[0]
    return pl.pallas_call(
        _mlp_kernel,
        grid=(m // tm, D_FF // tf),
        in_specs=[pl.BlockSpec((tm, D_MODEL), lambda i, f: (i, 0)),
                  pl.BlockSpec((1, D_MODEL), lambda i, f: (0, 0)),
                  pl.BlockSpec((D_MODEL, tf), lambda i, f: (0, f)),
                  pl.BlockSpec((tf, D_MODEL), lambda i, f: (f, 0)),
                  pl.BlockSpec((1, D_MODEL), lambda i, f: (0, 0))],
        out_specs=pl.BlockSpec((tm, D_MODEL), lambda i, f: (i, 0)),
        out_shape=jax.ShapeDtypeStruct((m, D_MODEL), F32),
        scratch_shapes=[pltpu.VMEM((tm, D_MODEL), BF16),
                        pltpu.VMEM((tm, D_MODEL), F32)],
        compiler_params=_params(("parallel", "arbitrary")),
        name="mlp",
    )(x, g_in, w1, w2, g_out)


def kernel(x, positions, norm_g, ret_w_in, ret_w_out, mlstm_w_in, mlstm_b_gate,
           mlstm_norm_g, mlstm_w_out, mlp_w1, mlp_w2):
    batch, seq, d = x.shape
    m = batch * seq
    xf = x.reshape(m, d)

    inv_freq = jnp.power(ROPE_BASE, -jnp.linspace(0.0, 1.0, RET_DK // 2, dtype=F32))
    cos, sin = rope_tables(positions, inv_freq)

    for i in range(DEPTH):
        g = norm_g[i].reshape(4, 1, d)
        j = i // 2
        if i % 2 == 0:
            proj = ret_proj(xf, g[0], ret_w_in[j].astype(BF16), cos, sin)
            y = ret_mix(proj, batch, seq)
            w_out = ret_w_out[j].astype(BF16)
        else:
            w_in = mlstm_w_in[j]
            w_gate = w_in[:, ML_MAIN:].astype(BF16)
            wg = jnp.pad(w_gate, ((0, 0), (0, GATE_LANES - 2 * ML_HEADS)))
            bias = mlstm_b_gate[j].astype(F32)
            b = jnp.pad(bias, (0, GATE_LANES - 2 * ML_HEADS)).reshape(1, GATE_LANES)
            proj, gcol, grow = ml_proj(xf, g[0], w_in[:, :ML_MAIN].astype(BF16), wg,
                                       w_gate.T, b, bias.reshape(2 * ML_HEADS, 1))
            y = ml_mix(proj, gcol, grow, mlstm_norm_g[j].reshape(1, ML_V).astype(F32),
                       batch, seq)
            w_out = mlstm_w_out[j].astype(BF16)
        xf = out_proj(y, w_out, g[1], xf)
        xf = mlp(xf, g[2], mlp_w1[i].astype(BF16), mlp_w2[i].astype(BF16), g[3])
    return xf.reshape(batch, seq, d)
```

```python
import functools
import math

import jax
import jax.numpy as jnp
from jax import lax
from jax.experimental import pallas as pl
from jax.experimental.pallas import tpu as pltpu

F32 = jnp.float32
BF16 = jnp.bfloat16

D_MODEL = 2048
DEPTH = 4
CHUNK = 128
EPS = 1e-6
D_FF = 4 * D_MODEL

RET_HEADS = 8
RET_DK = 256
RET_QK = RET_HEADS * RET_DK
RET_V = 2 * D_MODEL
RET_DV = RET_V // RET_HEADS
RET_IN = 2 * RET_QK + 2 * RET_V
ROPE_BASE = 10000.0

ML_HEADS = 8
ML_QK = D_MODEL // 2
ML_DQK = ML_QK // ML_HEADS
ML_V = D_MODEL
ML_DV = ML_V // ML_HEADS
ML_MAIN = 2 * ML_QK + 2 * ML_V
GATE_SOFTCAP = 15.0
GATE_LANES = 128

VMEM_LIMIT = 56 * 1024 * 1024

NT_DIMS = (((1,), (1,)), ((), ()))
TN_DIMS = (((0,), (0,)), ((), ()))


def _rms(x, g):
    ms = jnp.mean(x * x, axis=-1, keepdims=True)
    return x * lax.rsqrt(ms + EPS) * g


def _params(sem):
    return pltpu.CompilerParams(dimension_semantics=sem, vmem_limit_bytes=VMEM_LIMIT)


def _rope_kernel(pos_ref, freq_ref, cos_ref, sin_ref):
    ang = pos_ref[...].astype(F32) * freq_ref[...]
    cos_ref[...] = jnp.cos(ang)
    sin_ref[...] = jnp.sin(ang)


def rope_tables(positions, inv_freq, tm=2048):
    m = positions.size
    pos = positions.reshape(m, 1)
    half = inv_freq.shape[-1]
    return pl.pallas_call(
        _rope_kernel,
        grid=(m // tm,),
        in_specs=[pl.BlockSpec((tm, 1), lambda i: (i, 0)),
                  pl.BlockSpec((1, half), lambda i: (0, 0))],
        out_specs=[pl.BlockSpec((tm, half), lambda i: (i, 0)),
                   pl.BlockSpec((tm, half), lambda i: (i, 0))],
        out_shape=[jax.ShapeDtypeStruct((m, half), F32)] * 2,
        compiler_params=_params(("parallel",)),
        name="rope_tables",
    )(pos, inv_freq.reshape(1, half))


def _rot_proj_kernel(x_ref, g_ref, w_ref, cos_ref, sin_ref, o_ref, h_ref, *, tn):
    j = pl.program_id(1)

    @pl.when(j == 0)
    def _():
        h_ref[...] = _rms(x_ref[...], g_ref[...]).astype(BF16)

    res = jnp.dot(h_ref[...], w_ref[...], preferred_element_type=F32)
    scale = jnp.where(j >= RET_QK // tn, RET_DK ** -0.5, 1.0).astype(F32)
    c = cos_ref[...] * scale
    s = sin_ref[...] * scale
    half = RET_DK // 2
    for hh in range(tn // RET_DK):
        lo = hh * RET_DK
        t1 = res[:, lo:lo + half]
        t2 = res[:, lo + half:lo + RET_DK]
        o_ref[:, lo:lo + half] = (t1 * c - t2 * s).astype(BF16)
        o_ref[:, lo + half:lo + RET_DK] = (t2 * c + t1 * s).astype(BF16)


def rot_proj(x, g, w, cos, sin, tm=1024, tn=1024):
    m = x.shape[0]
    return pl.pallas_call(
        functools.partial(_rot_proj_kernel, tn=tn),
        grid=(m // tm, 2 * RET_QK // tn),
        in_specs=[pl.BlockSpec((tm, D_MODEL), lambda i, j: (i, 0)),
                  pl.BlockSpec((1, D_MODEL), lambda i, j: (0, 0)),
                  pl.BlockSpec((D_MODEL, tn), lambda i, j: (0, j)),
                  pl.BlockSpec((tm, RET_DK // 2), lambda i, j: (i, 0)),
                  pl.BlockSpec((tm, RET_DK // 2), lambda i, j: (i, 0))],
        out_specs=pl.BlockSpec((tm, tn), lambda i, j: (i, j)),
        out_shape=jax.ShapeDtypeStruct((m, 2 * RET_QK), BF16),
        scratch_shapes=[pltpu.VMEM((tm, D_MODEL), BF16)],
        compiler_params=_params(("parallel", "arbitrary")),
        name="rot_proj",
    )(x, g, w, cos, sin)


def _plain_proj_kernel(x_ref, g_ref, w_ref, o_ref, h_ref):
    @pl.when(pl.program_id(1) == 0)
    def _():
        h_ref[...] = _rms(x_ref[...], g_ref[...]).astype(BF16)

    o_ref[...] = jnp.dot(h_ref[...], w_ref[...], preferred_element_type=F32).astype(BF16)


def plain_proj(x, g, w, col0, n, tm=1024, tn=1024):
    m = x.shape[0]
    j0 = col0 // tn
    return pl.pallas_call(
        _plain_proj_kernel,
        grid=(m // tm, n // tn),
        in_specs=[pl.BlockSpec((tm, D_MODEL), lambda i, j: (i, 0)),
                  pl.BlockSpec((1, D_MODEL), lambda i, j: (0, 0)),
                  pl.BlockSpec((D_MODEL, tn), lambda i, j: (0, j + j0))],
        out_specs=pl.BlockSpec((tm, tn), lambda i, j: (i, j)),
        out_shape=jax.ShapeDtypeStruct((m, n), BF16),
        scratch_shapes=[pltpu.VMEM((tm, D_MODEL), BF16)],
        compiler_params=_params(("parallel", "arbitrary")),
        name="plain_proj",
    )(x, g, w)


def _ret_log_gamma(h):
    return math.log1p(-(2.0 ** (-5.0 - h)))


def _ret_kernel(q_ref, k_ref, v_ref, g_ref, y_ref, st_ref, dec_ref, xi_ref, zeta_ref):
    c = pl.program_id(1)

    @pl.when(c == 0)
    def _():
        st_ref[...] = jnp.zeros_like(st_ref)
        n = lax.broadcasted_iota(jnp.int32, (CHUNK, CHUNK), 0)
        mcol = lax.broadcasted_iota(jnp.int32, (CHUNK, CHUNK), 1)
        rel = (n - mcol).astype(F32)
        row = lax.broadcasted_iota(jnp.int32, (CHUNK, RET_DK), 0).astype(F32)
        for h in range(RET_HEADS):
            lg = _ret_log_gamma(h)
            dec_ref[h] = jnp.where(rel >= 0, jnp.exp(jnp.maximum(rel, 0.0) * lg), 0.0)
            xi_ref[h] = jnp.exp((row + 1.0) * lg)
            zeta_ref[h] = jnp.exp((CHUNK - 1.0 - row) * lg)

    for h in range(RET_HEADS):
        g_chunk = math.exp(CHUNK * _ret_log_gamma(h))
        qh = q_ref[:, h * RET_DK:(h + 1) * RET_DK]
        kh = k_ref[:, h * RET_DK:(h + 1) * RET_DK]
        vh = v_ref[:, h * RET_DV:(h + 1) * RET_DV]
        gh = g_ref[:, h * RET_DV:(h + 1) * RET_DV].astype(F32)
        st = st_ref[h]

        scores = lax.dot_general(qh, kh, NT_DIMS, preferred_element_type=F32) * dec_ref[h]
        qx = (qh.astype(F32) * xi_ref[h]).astype(BF16)
        lhs = jnp.concatenate([scores.astype(BF16), qx], axis=1)
        rhs = jnp.concatenate([vh, st.astype(BF16)], axis=0)
        y = jnp.dot(lhs, rhs, preferred_element_type=F32)

        kz = (kh.astype(F32) * zeta_ref[h]).astype(BF16)
        st_ref[h] = st * g_chunk + lax.dot_general(kz, vh, TN_DIMS, preferred_element_type=F32)

        mu = jnp.mean(y, axis=-1, keepdims=True)
        yc = y - mu
        var = jnp.mean(yc * yc, axis=-1, keepdims=True)
        yn = yc * lax.rsqrt(var + EPS)
        y_ref[:, h * RET_DV:(h + 1) * RET_DV] = (gh * jax.nn.sigmoid(gh) * yn).astype(BF16)


def ret_mix(qk, vg, batch, seq):
    m = batch * seq
    nc = seq // CHUNK
    row = lambda b, c: b * nc + c
    return pl.pallas_call(
        _ret_kernel,
        grid=(batch, nc),
        in_specs=[pl.BlockSpec((CHUNK, RET_QK), lambda b, c: (row(b, c), 0)),
                  pl.BlockSpec((CHUNK, RET_QK), lambda b, c: (row(b, c), 1)),
                  pl.BlockSpec((CHUNK, RET_V), lambda b, c: (row(b, c), 0)),
                  pl.BlockSpec((CHUNK, RET_V), lambda b, c: (row(b, c), 1))],
        out_specs=pl.BlockSpec((CHUNK, RET_V), lambda b, c: (row(b, c), 0)),
        out_shape=jax.ShapeDtypeStruct((m, RET_V), BF16),
        scratch_shapes=[pltpu.VMEM((RET_HEADS, RET_DK, RET_DV), F32),
                        pltpu.VMEM((RET_HEADS, CHUNK, CHUNK), F32),
                        pltpu.VMEM((RET_HEADS, CHUNK, RET_DK), F32),
                        pltpu.VMEM((RET_HEADS, CHUNK, RET_DK), F32)],
        compiler_params=_params(("parallel", "arbitrary")),
        name="ret_mix",
    )(qk, qk, vg, vg)


def _log_sigmoid(x):
    return jnp.minimum(x, 0.0) - jnp.log1p(jnp.exp(-jnp.abs(x)))


def _gate_logs(pre, is_forget):
    capped = GATE_SOFTCAP * jnp.tanh(pre / GATE_SOFTCAP)
    return jnp.where(is_forget, _log_sigmoid(capped), capped)


def _ml_proj_kernel(x_ref, g_ref, w_ref, wg_ref, wgt_ref, b_ref, bt_ref,
                    o_ref, gcol_ref, grow_ref, h_ref):
    j = pl.program_id(1)

    @pl.when(j == 0)
    def _():
        h = _rms(x_ref[...], g_ref[...]).astype(BF16)
        h_ref[...] = h
        pc = jnp.dot(h, wg_ref[...], preferred_element_type=F32) + b_ref[...]
        lane = lax.broadcasted_iota(jnp.int32, pc.shape, 1)
        gcol_ref[...] = _gate_logs(pc, lane >= ML_HEADS)
        pr = lax.dot_general(wgt_ref[...], h, NT_DIMS, preferred_element_type=F32) + bt_ref[...]
        sub = lax.broadcasted_iota(jnp.int32, pr.shape, 0)
        grow_ref[...] = _gate_logs(pr, sub >= ML_HEADS)

    o_ref[...] = jnp.dot(h_ref[...], w_ref[...], preferred_element_type=F32).astype(BF16)


def ml_proj(x, g, w, wg, wgt, b, bt, tm=1024, tn=1024):
    m = x.shape[0]
    ng = 2 * ML_HEADS
    return pl.pallas_call(
        _ml_proj_kernel,
        grid=(m // tm, ML_MAIN // tn),
        in_specs=[pl.BlockSpec((tm, D_MODEL), lambda i, j: (i, 0)),
                  pl.BlockSpec((1, D_MODEL), lambda i, j: (0, 0)),
                  pl.BlockSpec((D_MODEL, tn), lambda i, j: (0, j)),
                  pl.BlockSpec((D_MODEL, GATE_LANES), lambda i, j: (0, 0)),
                  pl.BlockSpec((ng, D_MODEL), lambda i, j: (0, 0)),
                  pl.BlockSpec((1, GATE_LANES), lambda i, j: (0, 0)),
                  pl.BlockSpec((ng, 1), lambda i, j: (0, 0))],
        out_specs=[pl.BlockSpec((tm, tn), lambda i, j: (i, j)),
                   pl.BlockSpec((tm, GATE_LANES), lambda i, j: (i, 0)),
                   pl.BlockSpec((ng, tm), lambda i, j: (0, i))],
        out_shape=[jax.ShapeDtypeStruct((m, ML_MAIN), BF16),
                   jax.ShapeDtypeStruct((m, GATE_LANES), F32),
                   jax.ShapeDtypeStruct((ng, m), F32)],
        scratch_shapes=[pltpu.VMEM((tm, D_MODEL), BF16)],
        compiler_params=_params(("parallel", "arbitrary")),
        name="ml_proj",
    )(x, g, w, wg, wgt, b, bt)


def _ml_kernel(q_ref, k_ref, v_ref, o_ref, gcol_ref, grow_ref, ng_ref, y_ref,
               c_ref, n_ref, m_ref):
    c = pl.program_id(1)

    @pl.when(c == 0)
    def _():
        c_ref[...] = jnp.zeros_like(c_ref)
        n_ref[...] = jnp.zeros_like(n_ref)
        m_ref[...] = jnp.zeros_like(m_ref)

    n_idx = lax.broadcasted_iota(jnp.int32, (CHUNK, CHUNK), 0)
    s_idx = lax.broadcasted_iota(jnp.int32, (CHUNK, CHUNK), 1)
    causal = n_idx >= s_idx
    lower = causal.astype(F32)
    upper = (n_idx <= s_idx).astype(F32)

    gcol = gcol_ref[...]
    grow = grow_ref[...]
    hi = lax.Precision.HIGHEST
    cum_col = jnp.dot(lower, gcol, precision=hi, preferred_element_type=F32)
    cum_row = jnp.dot(grow, upper, precision=hi, preferred_element_type=F32)
    scale = ML_DQK ** -0.5

    for h in range(ML_HEADS):
        qh = q_ref[:, h * ML_DQK:(h + 1) * ML_DQK]
        kh = k_ref[:, h * ML_DQK:(h + 1) * ML_DQK]
        vh = v_ref[:, h * ML_DV:(h + 1) * ML_DV]
        oh = o_ref[:, h * ML_DV:(h + 1) * ML_DV].astype(F32)
        bcol = cum_col[:, ML_HEADS + h:ML_HEADS + h + 1]
        icol = gcol[:, h:h + 1]
        brow = cum_row[ML_HEADS + h:ML_HEADS + h + 1, :]
        irow = grow[h:h + 1, :]
        c_st = c_ref[h]
        n_st = n_ref[h]
        m_st = m_ref[h][0:1, 0:1]

        d_log = jnp.where(causal, bcol - brow + irow, -jnp.inf)
        inter_log = bcol + m_st
        m_out = jnp.maximum(inter_log, jnp.max(d_log, axis=-1, keepdims=True))
        d_w = jnp.exp(d_log - m_out)
        inter_w = jnp.exp(inter_log - m_out)
        qf = qh.astype(F32)
        scores = lax.dot_general(qh, kh, NT_DIMS, preferred_element_type=F32) * (d_w * scale)
        lhs = jnp.concatenate([scores.astype(BF16), (qf * inter_w).astype(BF16)], axis=1)
        rhs = jnp.concatenate([vh, c_st.astype(BF16)], axis=0)
        num = jnp.dot(lhs, rhs, preferred_element_type=F32)
        den = jnp.sum(scores, axis=-1, keepdims=True) \
            + inter_w * jnp.sum(qf * n_st, axis=-1, keepdims=True)
        h_out = num / jnp.maximum(jnp.abs(den), jnp.exp(-m_out))

        b_tot = bcol[CHUNK - 1:CHUNK, :]
        w_log = b_tot - bcol + icol
        m_new = jnp.maximum(b_tot + m_st, jnp.max(w_log, axis=0, keepdims=True))
        w_s = jnp.exp(w_log - m_new) * scale
        carry = jnp.exp(b_tot + m_st - m_new)
        kw = kh.astype(F32) * w_s
        c_ref[h] = carry * c_st + lax.dot_general(kw.astype(BF16), vh, TN_DIMS,
                                                  preferred_element_type=F32)
        n_ref[h] = carry * n_st + jnp.sum(kw, axis=0, keepdims=True)
        m_ref[h] = jnp.broadcast_to(m_new, m_ref.shape[1:])

        yn = h_out * lax.rsqrt(jnp.mean(h_out * h_out, axis=-1, keepdims=True) + EPS)
        yn = yn * ng_ref[:, h * ML_DV:(h + 1) * ML_DV]
        y_ref[:, h * ML_DV:(h + 1) * ML_DV] = (yn * jax.nn.sigmoid(oh)).astype(BF16)


def ml_mix(proj, gcol, grow, norm_g, batch, seq):
    m = batch * seq
    nc = seq // CHUNK
    row = lambda b, c: b * nc + c
    return pl.pallas_call(
        _ml_kernel,
        grid=(batch, nc),
        in_specs=[pl.BlockSpec((CHUNK, ML_QK), lambda b, c: (row(b, c), 0)),
                  pl.BlockSpec((CHUNK, ML_QK), lambda b, c: (row(b, c), 1)),
                  pl.BlockSpec((CHUNK, ML_V), lambda b, c: (row(b, c), 1)),
                  pl.BlockSpec((CHUNK, ML_V), lambda b, c: (row(b, c), 2)),
                  pl.BlockSpec((CHUNK, GATE_LANES), lambda b, c: (row(b, c), 0)),
                  pl.BlockSpec((2 * ML_HEADS, CHUNK), lambda b, c: (0, row(b, c))),
                  pl.BlockSpec((1, ML_V), lambda b, c: (0, 0))],
        out_specs=pl.BlockSpec((CHUNK, ML_V), lambda b, c: (row(b, c), 0)),
        out_shape=jax.ShapeDtypeStruct((m, ML_V), BF16),
        scratch_shapes=[pltpu.VMEM((ML_HEADS, ML_DQK, ML_DV), F32),
                        pltpu.VMEM((ML_HEADS, 1, ML_DQK), F32),
                        pltpu.VMEM((ML_HEADS, 8, 128), F32)],
        compiler_params=_params(("parallel", "arbitrary")),
        name="ml_mix",
    )(proj, proj, proj, proj, gcol, grow, norm_g)


def _out_kernel(y_ref, w_ref, g_ref, x_ref, o_ref):
    k = pl.program_id(1)

    @pl.when(k == 0)
    def _():
        o_ref[...] = jnp.dot(y_ref[...], w_ref[...], preferred_element_type=F32)

    @pl.when(k > 0)
    def _():
        o_ref[...] += jnp.dot(y_ref[...], w_ref[...], preferred_element_type=F32)

    @pl.when(k == pl.num_programs(1) - 1)
    def _():
        o_ref[...] = x_ref[...] + _rms(o_ref[...], g_ref[...])


def out_proj(y, w, g, x, tm=1024, tk=1024):
    m, kdim = y.shape
    return pl.pallas_call(
        _out_kernel,
        grid=(m // tm, kdim // tk),
        in_specs=[pl.BlockSpec((tm, tk), lambda i, k: (i, k)),
                  pl.BlockSpec((tk, D_MODEL), lambda i, k: (k, 0)),
                  pl.BlockSpec((1, D_MODEL), lambda i, k: (0, 0)),
                  pl.BlockSpec((tm, D_MODEL), lambda i, k: (i, 0))],
        out_specs=pl.BlockSpec((tm, D_MODEL), lambda i, k: (i, 0)),
        out_shape=jax.ShapeDtypeStruct((m, D_MODEL), F32),
        compiler_params=_params(("parallel", "arbitrary")),
        name="out_proj",
    )(y, w, g, x)


def _mlp_kernel(x_ref, g_in_ref, w1_ref, w2_ref, g_out_ref, o_ref, h_ref):
    f = pl.program_id(1)

    @pl.when(f == 0)
    def _():
        h_ref[...] = _rms(x_ref[...], g_in_ref[...]).astype(BF16)
        o_ref[...] = jnp.zeros_like(o_ref)

    hid = jnp.maximum(jnp.dot(h_ref[...], w1_ref[...], preferred_element_type=F32), 0.0)
    hid = (hid * hid).astype(BF16)
    o_ref[...] += jnp.dot(hid, w2_ref[...], preferred_element_type=F32)

    @pl.when(f == pl.num_programs(1) - 1)
    def _():
        o_ref[...] = x_ref[...] + _rms(o_ref[...], g_out_ref[...])


def mlp(x, g_in, w1, w2, g_out, tm=1024, tf=512):
    m = x.shape[0]
    return pl.pallas_call(
        _mlp_kernel,
        grid=(m // tm, D_FF // tf),
        in_specs=[pl.BlockSpec((tm, D_MODEL), lambda i, f: (i, 0)),
                  pl.BlockSpec((1, D_MODEL), lambda i, f: (0, 0)),
                  pl.BlockSpec((D_MODEL, tf), lambda i, f: (0, f)),
                  pl.BlockSpec((tf, D_MODEL), lambda i, f: (f, 0)),
                  pl.BlockSpec((1, D_MODEL), lambda i, f: (0, 0))],
        out_specs=pl.BlockSpec((tm, D_MODEL), lambda i, f: (i, 0)),
        out_shape=jax.ShapeDtypeStruct((m, D_MODEL), F32),
        scratch_shapes=[pltpu.VMEM((tm, D_MODEL), BF16)],
        compiler_params=_params(("parallel", "arbitrary")),
        name="mlp",
    )(x, g_in, w1, w2, g_out)


def kernel(x, positions, norm_g, ret_w_in, ret_w_out, mlstm_w_in, mlstm_b_gate,
           mlstm_norm_g, mlstm_w_out, mlp_w1, mlp_w2):
    batch, seq, d = x.shape
    m = batch * seq
    xf = x.reshape(m, d)

    inv_freq = jnp.power(ROPE_BASE, -jnp.linspace(0.0, 1.0, RET_DK // 2, dtype=F32))
    cos, sin = rope_tables(positions, inv_freq)

    for i in range(DEPTH):
        g = norm_g[i].reshape(4, 1, d)
        j = i // 2
        if i % 2 == 0:
            w_in = ret_w_in[j].astype(BF16)
            qk = rot_proj(xf, g[0], w_in, cos, sin)
            vg = plain_proj(xf, g[0], w_in, 2 * RET_QK, 2 * RET_V)
            y = ret_mix(qk, vg, batch, seq)
            w_out = ret_w_out[j].astype(BF16)
        else:
            w_in = mlstm_w_in[j]
            w_gate = w_in[:, ML_MAIN:].astype(BF16)
            wg = jnp.pad(w_gate, ((0, 0), (0, GATE_LANES - 2 * ML_HEADS)))
            bias = mlstm_b_gate[j].astype(F32)
            b = jnp.pad(bias, (0, GATE_LANES - 2 * ML_HEADS)).reshape(1, GATE_LANES)
            proj, gcol, grow = ml_proj(xf, g[0], w_in[:, :ML_MAIN].astype(BF16), wg,
                                       w_gate.T, b, bias.reshape(2 * ML_HEADS, 1))
            y = ml_mix(proj, gcol, grow, mlstm_norm_g[j].reshape(1, ML_V).astype(F32),
                       batch, seq)
            w_out = mlstm_w_out[j].astype(BF16)
        xf = out_proj(y, w_out, g[1], xf)
        xf = mlp(xf, g[2], mlp_w1[i].astype(BF16), mlp_w2[i].astype(BF16), g[3])
    return xf.reshape(batch, seq, d)
```

```python
import functools
import math

import jax
import jax.numpy as jnp
from jax import lax
from jax.experimental import pallas as pl
from jax.experimental.pallas import tpu as pltpu

F32 = jnp.float32
BF16 = jnp.bfloat16

D_MODEL = 2048
DEPTH = 4
CHUNK = 128
EPS = 1e-6
D_FF = 4 * D_MODEL

RET_HEADS = 8
RET_DK = 256
RET_QK = RET_HEADS * RET_DK
RET_V = 2 * D_MODEL
RET_DV = RET_V // RET_HEADS
RET_IN = 2 * RET_QK + 2 * RET_V
ROPE_BASE = 10000.0

ML_HEADS = 8
ML_QK = D_MODEL // 2
ML_DQK = ML_QK // ML_HEADS
ML_V = D_MODEL
ML_DV = ML_V // ML_HEADS
ML_MAIN = 2 * ML_QK + 2 * ML_V
GATE_SOFTCAP = 15.0
GATE_LANES = 128

VMEM_LIMIT = 56 * 1024 * 1024

NT_DIMS = (((1,), (1,)), ((), ()))
TN_DIMS = (((0,), (0,)), ((), ()))


def _rms(x, g):
    ms = jnp.mean(x * x, axis=-1, keepdims=True)
    return x * lax.rsqrt(ms + EPS) * g


NORM_ROWS = 256


def _norm_rows(dst_ref, src_ref, g_ref, res_ref=None):
    g = g_ref[...]

    def body(r, carry):
        rows = pl.ds(pl.multiple_of(r * NORM_ROWS, NORM_ROWS), NORM_ROWS)
        v = _rms(src_ref[rows, :], g)
        if res_ref is not None:
            v = res_ref[rows, :] + v
        dst_ref[rows, :] = v.astype(dst_ref.dtype)
        return carry

    lax.fori_loop(0, src_ref.shape[0] // NORM_ROWS, body, 0)


def _params(sem):
    return pltpu.CompilerParams(dimension_semantics=sem, vmem_limit_bytes=VMEM_LIMIT)


def _rope_kernel(pos_ref, freq_ref, cos_ref, sin_ref):
    ang = pos_ref[...].astype(F32) * freq_ref[...]
    cos_ref[...] = jnp.cos(ang)
    sin_ref[...] = jnp.sin(ang)


def rope_tables(positions, inv_freq, tm=2048):
    m = positions.size
    pos = positions.reshape(m, 1)
    half = inv_freq.shape[-1]
    return pl.pallas_call(
        _rope_kernel,
        grid=(m // tm,),
        in_specs=[pl.BlockSpec((tm, 1), lambda i: (i, 0)),
                  pl.BlockSpec((1, half), lambda i: (0, 0))],
        out_specs=[pl.BlockSpec((tm, half), lambda i: (i, 0)),
                   pl.BlockSpec((tm, half), lambda i: (i, 0))],
        out_shape=[jax.ShapeDtypeStruct((m, half), F32)] * 2,
        compiler_params=_params(("parallel",)),
        name="rope_tables",
    )(pos, inv_freq.reshape(1, half))


def _ret_proj_kernel(x_ref, g_ref, w_ref, cos_ref, sin_ref, o_ref, h_ref, *, tn):
    j = pl.program_id(1)
    n_rot = 2 * RET_QK // tn

    @pl.when(j == 0)
    def _():
        _norm_rows(h_ref, x_ref, g_ref)

    @pl.when(j < n_rot)
    def _():
        res = jnp.dot(h_ref[...], w_ref[...], preferred_element_type=F32)
        scale = jnp.where(j >= n_rot // 2, RET_DK ** -0.5, 1.0).astype(F32)
        c = cos_ref[...] * scale
        s = sin_ref[...] * scale
        half = RET_DK // 2
        for hh in range(tn // RET_DK):
            lo = hh * RET_DK
            t1 = res[:, lo:lo + half]
            t2 = res[:, lo + half:lo + RET_DK]
            o_ref[:, lo:lo + half] = (t1 * c - t2 * s).astype(BF16)
            o_ref[:, lo + half:lo + RET_DK] = (t2 * c + t1 * s).astype(BF16)

    @pl.when(j >= n_rot)
    def _():
        o_ref[...] = jnp.dot(h_ref[...], w_ref[...],
                             preferred_element_type=F32).astype(BF16)


def ret_proj(x, norm_g, g_idx, w, layer, cos, sin, tm=1024, tn=2048):
    m = x.shape[0]
    return pl.pallas_call(
        functools.partial(_ret_proj_kernel, tn=tn),
        grid=(m // tm, RET_IN // tn),
        in_specs=[pl.BlockSpec((tm, D_MODEL), lambda i, j: (i, 0)),
                  pl.BlockSpec((None, 1, D_MODEL), lambda i, j: (g_idx, 0, 0)),
                  pl.BlockSpec((None, D_MODEL, tn), lambda i, j: (layer, 0, j)),
                  pl.BlockSpec((tm, RET_DK // 2), lambda i, j: (i, 0)),
                  pl.BlockSpec((tm, RET_DK // 2), lambda i, j: (i, 0))],
        out_specs=pl.BlockSpec((tm, tn), lambda i, j: (i, j)),
        out_shape=jax.ShapeDtypeStruct((m, RET_IN), BF16),
        scratch_shapes=[pltpu.VMEM((tm, D_MODEL), BF16)],
        compiler_params=_params(("parallel", "arbitrary")),
        name="ret_proj",
    )(x, norm_g, w, cos, sin)


def _ret_log_gamma(h):
    return math.log1p(-(2.0 ** (-5.0 - h)))


def _ret_kernel(q_ref, k_ref, v_ref, g_ref, y_ref, st_ref, dec_ref, xi_ref, zeta_ref):
    c = pl.program_id(1)

    @pl.when(c == 0)
    def _():
        st_ref[...] = jnp.zeros_like(st_ref)
        n = lax.broadcasted_iota(jnp.int32, (CHUNK, CHUNK), 0)
        mcol = lax.broadcasted_iota(jnp.int32, (CHUNK, CHUNK), 1)
        rel = (n - mcol).astype(F32)
        row = lax.broadcasted_iota(jnp.int32, (CHUNK, RET_DK), 0).astype(F32)
        for h in range(RET_HEADS):
            lg = _ret_log_gamma(h)
            dec_ref[h] = jnp.where(rel >= 0, jnp.exp(jnp.maximum(rel, 0.0) * lg), 0.0)
            xi_ref[h] = jnp.exp((row + 1.0) * lg)
            zeta_ref[h] = jnp.exp((CHUNK - 1.0 - row) * lg)

    for h in range(RET_HEADS):
        g_chunk = math.exp(CHUNK * _ret_log_gamma(h))
        qh = q_ref[:, h * RET_DK:(h + 1) * RET_DK]
        kh = k_ref[:, h * RET_DK:(h + 1) * RET_DK]
        vh = v_ref[:, h * RET_DV:(h + 1) * RET_DV]
        gh = g_ref[:, h * RET_DV:(h + 1) * RET_DV].astype(F32)
        st = st_ref[h]

        scores = lax.dot_general(qh, kh, NT_DIMS, preferred_element_type=F32) * dec_ref[h]
        qx = (qh.astype(F32) * xi_ref[h]).astype(BF16)
        lhs = jnp.concatenate([scores.astype(BF16), qx], axis=1)
        rhs = jnp.concatenate([vh, st.astype(BF16)], axis=0)
        y = jnp.dot(lhs, rhs, preferred_element_type=F32)

        kz = (kh.astype(F32) * zeta_ref[h]).astype(BF16)
        st_ref[h] = st * g_chunk + lax.dot_general(kz, vh, TN_DIMS, preferred_element_type=F32)

        mu = jnp.mean(y, axis=-1, keepdims=True)
        yc = y - mu
        var = jnp.mean(yc * yc, axis=-1, keepdims=True)
        yn = yc * lax.rsqrt(var + EPS)
        y_ref[:, h * RET_DV:(h + 1) * RET_DV] = (gh * jax.nn.sigmoid(gh) * yn).astype(BF16)


def ret_mix(proj, batch, seq):
    m = batch * seq
    nc = seq // CHUNK
    row = lambda b, c: b * nc + c
    return pl.pallas_call(
        _ret_kernel,
        grid=(batch, nc),
        in_specs=[pl.BlockSpec((CHUNK, RET_QK), lambda b, c: (row(b, c), 0)),
                  pl.BlockSpec((CHUNK, RET_QK), lambda b, c: (row(b, c), 1)),
                  pl.BlockSpec((CHUNK, RET_V), lambda b, c: (row(b, c), 1)),
                  pl.BlockSpec((CHUNK, RET_V), lambda b, c: (row(b, c), 2))],
        out_specs=pl.BlockSpec((CHUNK, RET_V), lambda b, c: (row(b, c), 0)),
        out_shape=jax.ShapeDtypeStruct((m, RET_V), BF16),
        scratch_shapes=[pltpu.VMEM((RET_HEADS, RET_DK, RET_DV), F32),
                        pltpu.VMEM((RET_HEADS, CHUNK, CHUNK), F32),
                        pltpu.VMEM((RET_HEADS, CHUNK, RET_DK), F32),
                        pltpu.VMEM((RET_HEADS, CHUNK, RET_DK), F32)],
        compiler_params=_params(("parallel", "arbitrary")),
        name="ret_mix",
    )(proj, proj, proj, proj)


def _log_sigmoid(x):
    return jnp.minimum(x, 0.0) - jnp.log1p(jnp.exp(-jnp.abs(x)))


def _gate_logs(pre, is_forget):
    capped = GATE_SOFTCAP * jnp.tanh(pre / GATE_SOFTCAP)
    return jnp.where(is_forget, _log_sigmoid(capped), capped)


def _ml_proj_kernel(x_ref, g_ref, w_ref, wg_ref, wgt_ref, b_ref, bt_ref,
                    o_ref, gcol_ref, grow_ref, h_ref):
    j = pl.program_id(1)

    @pl.when(j == 0)
    def _():
        _norm_rows(h_ref, x_ref, g_ref)
        h = h_ref[...]
        pc = jnp.dot(h, wg_ref[...], preferred_element_type=F32) + b_ref[...]
        lane = lax.broadcasted_iota(jnp.int32, pc.shape, 1)
        gcol_ref[...] = _gate_logs(pc, lane >= ML_HEADS)
        pr = lax.dot_general(wgt_ref[...], h, NT_DIMS, preferred_element_type=F32) + bt_ref[...]
        sub = lax.broadcasted_iota(jnp.int32, pr.shape, 0)
        grow_ref[...] = _gate_logs(pr, sub >= ML_HEADS)

    o_ref[...] = jnp.dot(h_ref[...], w_ref[...], preferred_element_type=F32).astype(BF16)


def ml_proj(x, norm_g, g_idx, w, layer, wg, wgt, b, bt, tm=1024, tn=2048):
    m = x.shape[0]
    ng = 2 * ML_HEADS
    return pl.pallas_call(
        _ml_proj_kernel,
        grid=(m // tm, ML_MAIN // tn),
        in_specs=[pl.BlockSpec((tm, D_MODEL), lambda i, j: (i, 0)),
                  pl.BlockSpec((None, 1, D_MODEL), lambda i, j: (g_idx, 0, 0)),
                  pl.BlockSpec((None, D_MODEL, tn), lambda i, j: (layer, 0, j)),
                  pl.BlockSpec((D_MODEL, GATE_LANES), lambda i, j: (0, 0)),
                  pl.BlockSpec((ng, D_MODEL), lambda i, j: (0, 0)),
                  pl.BlockSpec((1, GATE_LANES), lambda i, j: (0, 0)),
                  pl.BlockSpec((ng, 1), lambda i, j: (0, 0))],
        out_specs=[pl.BlockSpec((tm, tn), lambda i, j: (i, j)),
                   pl.BlockSpec((tm, GATE_LANES), lambda i, j: (i, 0)),
                   pl.BlockSpec((ng, tm), lambda i, j: (0, i))],
        out_shape=[jax.ShapeDtypeStruct((m, ML_MAIN), BF16),
                   jax.ShapeDtypeStruct((m, GATE_LANES), F32),
                   jax.ShapeDtypeStruct((ng, m), F32)],
        scratch_shapes=[pltpu.VMEM((tm, D_MODEL), BF16)],
        compiler_params=_params(("parallel", "arbitrary")),
        name="ml_proj",
    )(x, norm_g, w, wg, wgt, b, bt)


def _ml_kernel(q_ref, k_ref, v_ref, o_ref, gcol_ref, grow_ref, ng_ref, y_ref,
               c_ref, n_ref, m_ref):
    c = pl.program_id(1)

    @pl.when(c == 0)
    def _():
        c_ref[...] = jnp.zeros_like(c_ref)
        n_ref[...] = jnp.zeros_like(n_ref)
        m_ref[...] = jnp.zeros_like(m_ref)

    n_idx = lax.broadcasted_iota(jnp.int32, (CHUNK, CHUNK), 0)
    s_idx = lax.broadcasted_iota(jnp.int32, (CHUNK, CHUNK), 1)
    causal = n_idx >= s_idx
    lower = causal.astype(F32)
    upper = (n_idx <= s_idx).astype(F32)

    gcol = gcol_ref[...]
    grow = grow_ref[...]
    hi = lax.Precision.HIGHEST
    cum_col = jnp.dot(lower, gcol, precision=hi, preferred_element_type=F32)
    cum_row = jnp.dot(grow, upper, precision=hi, preferred_element_type=F32)
    scale = ML_DQK ** -0.5

    for h in range(ML_HEADS):
        qh = q_ref[:, h * ML_DQK:(h + 1) * ML_DQK]
        kh = k_ref[:, h * ML_DQK:(h + 1) * ML_DQK]
        vh = v_ref[:, h * ML_DV:(h + 1) * ML_DV]
        oh = o_ref[:, h * ML_DV:(h + 1) * ML_DV].astype(F32)
        bcol = cum_col[:, ML_HEADS + h:ML_HEADS + h + 1]
        icol = gcol[:, h:h + 1]
        brow = cum_row[ML_HEADS + h:ML_HEADS + h + 1, :]
        irow = grow[h:h + 1, :]
        c_st = c_ref[h]
        n_st = n_ref[h]
        m_st = m_ref[h][0:1, 0:1]

        d_log = jnp.where(causal, bcol - brow + irow, -jnp.inf)
        inter_log = bcol + m_st
        m_out = jnp.maximum(inter_log, jnp.max(d_log, axis=-1, keepdims=True))
        d_w = jnp.exp(d_log - m_out)
        inter_w = jnp.exp(inter_log - m_out)
        qf = qh.astype(F32)
        scores = lax.dot_general(qh, kh, NT_DIMS, preferred_element_type=F32) * (d_w * scale)
        lhs = jnp.concatenate([scores.astype(BF16), (qf * inter_w).astype(BF16)], axis=1)
        rhs = jnp.concatenate([vh, c_st.astype(BF16)], axis=0)
        num = jnp.dot(lhs, rhs, preferred_element_type=F32)
        den = jnp.sum(scores, axis=-1, keepdims=True) \
            + inter_w * jnp.sum(qf * n_st, axis=-1, keepdims=True)
        h_out = num / jnp.maximum(jnp.abs(den), jnp.exp(-m_out))

        b_tot = bcol[CHUNK - 1:CHUNK, :]
        w_log = b_tot - bcol + icol
        m_new = jnp.maximum(b_tot + m_st, jnp.max(w_log, axis=0, keepdims=True))
        w_s = jnp.exp(w_log - m_new) * scale
        carry = jnp.exp(b_tot + m_st - m_new)
        kw = kh.astype(F32) * w_s
        c_ref[h] = carry * c_st + lax.dot_general(kw.astype(BF16), vh, TN_DIMS,
                                                  preferred_element_type=F32)
        n_ref[h] = carry * n_st + jnp.sum(kw, axis=0, keepdims=True)
        m_ref[h] = jnp.broadcast_to(m_new, m_ref.shape[1:])

        yn = h_out * lax.rsqrt(jnp.mean(h_out * h_out, axis=-1, keepdims=True) + EPS)
        yn = yn * ng_ref[:, h * ML_DV:(h + 1) * ML_DV]
        y_ref[:, h * ML_DV:(h + 1) * ML_DV] = (yn * jax.nn.sigmoid(oh)).astype(BF16)


def ml_mix(proj, gcol, grow, norm_g, batch, seq):
    m = batch * seq
    nc = seq // CHUNK
    row = lambda b, c: b * nc + c
    return pl.pallas_call(
        _ml_kernel,
        grid=(batch, nc),
        in_specs=[pl.BlockSpec((CHUNK, ML_QK), lambda b, c: (row(b, c), 0)),
                  pl.BlockSpec((CHUNK, ML_QK), lambda b, c: (row(b, c), 1)),
                  pl.BlockSpec((CHUNK, ML_V), lambda b, c: (row(b, c), 1)),
                  pl.BlockSpec((CHUNK, ML_V), lambda b, c: (row(b, c), 2)),
                  pl.BlockSpec((CHUNK, GATE_LANES), lambda b, c: (row(b, c), 0)),
                  pl.BlockSpec((2 * ML_HEADS, CHUNK), lambda b, c: (0, row(b, c))),
                  pl.BlockSpec((1, ML_V), lambda b, c: (0, 0))],
        out_specs=pl.BlockSpec((CHUNK, ML_V), lambda b, c: (row(b, c), 0)),
        out_shape=jax.ShapeDtypeStruct((m, ML_V), BF16),
        scratch_shapes=[pltpu.VMEM((ML_HEADS, ML_DQK, ML_DV), F32),
                        pltpu.VMEM((ML_HEADS, 1, ML_DQK), F32),
                        pltpu.VMEM((ML_HEADS, 8, 128), F32)],
        compiler_params=_params(("parallel", "arbitrary")),
        name="ml_mix",
    )(proj, proj, proj, proj, gcol, grow, norm_g)


def _out_kernel(y_ref, w_ref, g_ref, x_ref, o_ref):
    k = pl.program_id(1)

    @pl.when(k == 0)
    def _():
        o_ref[...] = jnp.dot(y_ref[...], w_ref[...], preferred_element_type=F32)

    @pl.when(k > 0)
    def _():
        o_ref[...] += jnp.dot(y_ref[...], w_ref[...], preferred_element_type=F32)

    @pl.when(k == pl.num_programs(1) - 1)
    def _():
        _norm_rows(o_ref, o_ref, g_ref, res_ref=x_ref)


def out_proj(y, w, layer, norm_g, g_idx, x, tm=1024, tk=1024):
    m, kdim = y.shape
    return pl.pallas_call(
        _out_kernel,
        grid=(m // tm, kdim // tk),
        in_specs=[pl.BlockSpec((tm, tk), lambda i, k: (i, k)),
                  pl.BlockSpec((None, tk, D_MODEL), lambda i, k: (layer, k, 0)),
                  pl.BlockSpec((None, 1, D_MODEL), lambda i, k: (g_idx, 0, 0)),
                  pl.BlockSpec((tm, D_MODEL), lambda i, k: (i, 0))],
        out_specs=pl.BlockSpec((tm, D_MODEL), lambda i, k: (i, 0)),
        out_shape=jax.ShapeDtypeStruct((m, D_MODEL), F32),
        compiler_params=_params(("parallel", "arbitrary")),
        name="out_proj",
    )(y, w, norm_g, x)


def _mlp_kernel(x_ref, g_in_ref, w1_ref, w2_ref, g_out_ref, o_ref, h_ref):
    f = pl.program_id(1)

    @pl.when(f == 0)
    def _():
        _norm_rows(h_ref, x_ref, g_in_ref)
        o_ref[...] = jnp.zeros_like(o_ref)

    hid = jnp.maximum(jnp.dot(h_ref[...], w1_ref[...], preferred_element_type=F32), 0.0)
    hid = (hid * hid).astype(BF16)
    o_ref[...] += jnp.dot(hid, w2_ref[...], preferred_element_type=F32)

    @pl.when(f == pl.num_programs(1) - 1)
    def _():
        _norm_rows(o_ref, o_ref, g_out_ref, res_ref=x_ref)


def mlp(x, norm_g, g_in_idx, g_out_idx, w1, w2, layer, tm=1024, tf=512):
    m = x.shape[0]
    return pl.pallas_call(
        _mlp_kernel,
        grid=(m // tm, D_FF // tf),
        in_specs=[pl.BlockSpec((tm, D_MODEL), lambda i, f: (i, 0)),
                  pl.BlockSpec((None, 1, D_MODEL), lambda i, f: (g_in_idx, 0, 0)),
                  pl.BlockSpec((None, D_MODEL, tf), lambda i, f: (layer, 0, f)),
                  pl.BlockSpec((None, tf, D_MODEL), lambda i, f: (layer, f, 0)),
                  pl.BlockSpec((None, 1, D_MODEL), lambda i, f: (g_out_idx, 0, 0))],
        out_specs=pl.BlockSpec((tm, D_MODEL), lambda i, f: (i, 0)),
        out_shape=jax.ShapeDtypeStruct((m, D_MODEL), F32),
        scratch_shapes=[pltpu.VMEM((tm, D_MODEL), BF16)],
        compiler_params=_params(("parallel", "arbitrary")),
        name="mlp",
    )(x, norm_g, w1, w2, norm_g)


def kernel(x, positions, norm_g, ret_w_in, ret_w_out, mlstm_w_in, mlstm_b_gate,
           mlstm_norm_g, mlstm_w_out, mlp_w1, mlp_w2):
    batch, seq, d = x.shape
    m = batch * seq
    xf = x.reshape(m, d)

    inv_freq = jnp.power(ROPE_BASE, -jnp.linspace(0.0, 1.0, RET_DK // 2, dtype=F32))
    cos, sin = rope_tables(positions, inv_freq)

    gains = norm_g.astype(F32).reshape(DEPTH * 4, 1, d)
    ret_w_in_b = ret_w_in.astype(BF16)
    ret_w_out_b = ret_w_out.astype(BF16)
    ml_w_in_b = mlstm_w_in.astype(BF16)
    ml_w_out_b = mlstm_w_out.astype(BF16)
    w1_b = mlp_w1.astype(BF16)
    w2_b = mlp_w2.astype(BF16)
    n_gate = 2 * ML_HEADS

    for i in range(DEPTH):
        j = i // 2
        if i % 2 == 0:
            proj = ret_proj(xf, gains, 4 * i, ret_w_in_b, j, cos, sin)
            y = ret_mix(proj, batch, seq)
            xf = out_proj(y, ret_w_out_b, j, gains, 4 * i + 1, xf)
        else:
            w_gate = ml_w_in_b[j, :, ML_MAIN:]
            wg = jnp.pad(w_gate, ((0, 0), (0, GATE_LANES - n_gate)))
            bias = mlstm_b_gate[j].astype(F32)
            b = jnp.pad(bias, (0, GATE_LANES - n_gate)).reshape(1, GATE_LANES)
            proj, gcol, grow = ml_proj(xf, gains, 4 * i, ml_w_in_b, j, wg, w_gate.T, b,
                                       bias.reshape(n_gate, 1))
            y = ml_mix(proj, gcol, grow, mlstm_norm_g[j].reshape(1, ML_V).astype(F32),
                       batch, seq)
            xf = out_proj(y, ml_w_out_b, j, gains, 4 * i + 1, xf)
        xf = mlp(xf, gains, 4 * i + 2, 4 * i + 3, w1_b, w2_b, i)
    return xf.reshape(batch, seq, d)
```

```python
import functools
import math

import jax
import jax.numpy as jnp
from jax import lax
from jax.experimental import pallas as pl
from jax.experimental.pallas import tpu as pltpu

F32 = jnp.float32
BF16 = jnp.bfloat16

D_MODEL = 2048
DEPTH = 4
CHUNK = 128
RET_CHUNK = 256
EPS = 1e-6
D_FF = 4 * D_MODEL

RET_HEADS = 8
RET_DK = 256
RET_QK = RET_HEADS * RET_DK
RET_V = 2 * D_MODEL
RET_DV = RET_V // RET_HEADS
RET_IN = 2 * RET_QK + 2 * RET_V
ROPE_BASE = 10000.0

ML_HEADS = 8
ML_QK = D_MODEL // 2
ML_DQK = ML_QK // ML_HEADS
ML_V = D_MODEL
ML_DV = ML_V // ML_HEADS
ML_MAIN = 2 * ML_QK + 2 * ML_V
GATE_SOFTCAP = 15.0
GATE_LANES = 128

VMEM_LIMIT = 56 * 1024 * 1024

NT_DIMS = (((1,), (1,)), ((), ()))
TN_DIMS = (((0,), (0,)), ((), ()))


def _rms(x, g):
    ms = jnp.mean(x * x, axis=-1, keepdims=True)
    return x * lax.rsqrt(ms + EPS) * g


NORM_ROWS = 256


def _norm_rows(dst_ref, src_ref, g_ref, res_ref=None):
    g = g_ref[...]

    def body(r, carry):
        rows = pl.ds(pl.multiple_of(r * NORM_ROWS, NORM_ROWS), NORM_ROWS)
        v = _rms(src_ref[rows, :], g)
        if res_ref is not None:
            v = res_ref[rows, :] + v
        dst_ref[rows, :] = v.astype(dst_ref.dtype)
        return carry

    lax.fori_loop(0, src_ref.shape[0] // NORM_ROWS, body, 0)


def _params(sem, **kw):
    return pltpu.CompilerParams(dimension_semantics=sem, vmem_limit_bytes=VMEM_LIMIT, **kw)


def _rope_kernel(pos_ref, freq_ref, cos_ref, sin_ref):
    ang = pos_ref[...].astype(F32) * freq_ref[...]
    cos_ref[...] = jnp.cos(ang)
    sin_ref[...] = jnp.sin(ang)


def rope_tables(positions, inv_freq, tm=2048):
    m = positions.size
    pos = positions.reshape(m, 1)
    half = inv_freq.shape[-1]
    return pl.pallas_call(
        _rope_kernel,
        grid=(m // tm,),
        in_specs=[pl.BlockSpec((tm, 1), lambda i: (i, 0)),
                  pl.BlockSpec((1, half), lambda i: (0, 0))],
        out_specs=[pl.BlockSpec((tm, half), lambda i: (i, 0)),
                   pl.BlockSpec((tm, half), lambda i: (i, 0))],
        out_shape=[jax.ShapeDtypeStruct((m, half), F32)] * 2,
        compiler_params=_params(("parallel",)),
        name="rope_tables",
    )(pos, inv_freq.reshape(1, half))


def _ret_proj_kernel(x_ref, g_ref, w_ref, cos_ref, sin_ref, o_ref, h_ref, *, tn):
    j = pl.program_id(1)
    n_rot = 2 * RET_QK // tn

    @pl.when(j == 0)
    def _():
        _norm_rows(h_ref, x_ref, g_ref)

    @pl.when(j < n_rot)
    def _():
        res = jnp.dot(h_ref[...], w_ref[...], preferred_element_type=F32)
        scale = jnp.where(j >= n_rot // 2, RET_DK ** -0.5, 1.0).astype(F32)
        c = cos_ref[...] * scale
        s = sin_ref[...] * scale
        half = RET_DK // 2
        for hh in range(tn // RET_DK):
            lo = hh * RET_DK
            t1 = res[:, lo:lo + half]
            t2 = res[:, lo + half:lo + RET_DK]
            o_ref[:, lo:lo + half] = (t1 * c - t2 * s).astype(BF16)
            o_ref[:, lo + half:lo + RET_DK] = (t2 * c + t1 * s).astype(BF16)

    @pl.when(j >= n_rot)
    def _():
        o_ref[...] = jnp.dot(h_ref[...], w_ref[...],
                             preferred_element_type=F32).astype(BF16)


def ret_proj(x, norm_g, g_idx, w, layer, cos, sin, tm=1024, tn=2048):
    m = x.shape[0]
    return pl.pallas_call(
        functools.partial(_ret_proj_kernel, tn=tn),
        grid=(m // tm, RET_IN // tn),
        in_specs=[pl.BlockSpec((tm, D_MODEL), lambda i, j: (i, 0)),
                  pl.BlockSpec((None, 1, D_MODEL), lambda i, j: (g_idx, 0, 0)),
                  pl.BlockSpec((None, D_MODEL, tn), lambda i, j: (layer, 0, j)),
                  pl.BlockSpec((tm, RET_DK // 2), lambda i, j: (i, 0)),
                  pl.BlockSpec((tm, RET_DK // 2), lambda i, j: (i, 0))],
        out_specs=pl.BlockSpec((tm, tn), lambda i, j: (i, j)),
        out_shape=jax.ShapeDtypeStruct((m, RET_IN), BF16),
        scratch_shapes=[pltpu.VMEM((tm, D_MODEL), BF16)],
        compiler_params=_params(("parallel", "arbitrary")),
        name="ret_proj",
    )(x, norm_g, w, cos, sin)


def _ret_log_gamma(h):
    return math.log1p(-(2.0 ** (-5.0 - h)))


def _ret_kernel(q_ref, k_ref, v_ref, g_ref, y_ref, st_ref, dec_ref, xi_ref, zeta_ref):
    c = pl.program_id(1)

    @pl.when(c == 0)
    def _():
        st_ref[...] = jnp.zeros_like(st_ref)
        n = lax.broadcasted_iota(jnp.int32, (RET_CHUNK, RET_CHUNK), 0)
        mcol = lax.broadcasted_iota(jnp.int32, (RET_CHUNK, RET_CHUNK), 1)
        rel = (n - mcol).astype(F32)
        row = lax.broadcasted_iota(jnp.int32, (RET_CHUNK, RET_DK), 0).astype(F32)
        for h in range(RET_HEADS):
            lg = _ret_log_gamma(h)
            dec_ref[h] = jnp.where(rel >= 0, jnp.exp(jnp.maximum(rel, 0.0) * lg), 0.0)
            xi_ref[h] = jnp.exp((row + 1.0) * lg).astype(BF16)
            zeta_ref[h] = jnp.exp((RET_CHUNK - 1.0 - row) * lg).astype(BF16)

    heads = range(RET_HEADS)
    q = [q_ref[:, h * RET_DK:(h + 1) * RET_DK] for h in heads]
    k = [k_ref[:, h * RET_DK:(h + 1) * RET_DK] for h in heads]
    v = [v_ref[:, h * RET_DV:(h + 1) * RET_DV] for h in heads]


    raw = [lax.dot_general(q[h], k[h], NT_DIMS, preferred_element_type=F32) for h in heads]

    lhs, rhs, kz = [], [], []
    for h in heads:
        lhs.append(jnp.concatenate([(raw[h] * dec_ref[h]).astype(BF16), q[h] * xi_ref[h]],
                                   axis=1))
        rhs.append(jnp.concatenate([v[h], st_ref[h].astype(BF16)], axis=0))
        kz.append(k[h] * zeta_ref[h])

    y = []
    for h in heads:
        g_chunk = math.exp(RET_CHUNK * _ret_log_gamma(h))
        y.append(jnp.dot(lhs[h], rhs[h], preferred_element_type=F32))
        st_ref[h] = st_ref[h] * g_chunk + lax.dot_general(kz[h], v[h], TN_DIMS,
                                                          preferred_element_type=F32)

    for h in heads:
        cols = slice(h * RET_DV, (h + 1) * RET_DV)
        mu = jnp.mean(y[h], axis=-1, keepdims=True)
        yc = y[h] - mu
        var = jnp.mean(yc * yc, axis=-1, keepdims=True)
        yn = yc * lax.rsqrt(var + EPS)
        gh = g_ref[:, cols]
        y_ref[:, cols] = (gh * jax.nn.sigmoid(gh)) * yn.astype(BF16)


def ret_mix(proj, batch, seq):
    m = batch * seq
    nc = seq // RET_CHUNK
    row = lambda b, c: b * nc + c
    return pl.pallas_call(
        _ret_kernel,
        grid=(batch, nc),
        in_specs=[pl.BlockSpec((RET_CHUNK, RET_QK), lambda b, c: (row(b, c), 0)),
                  pl.BlockSpec((RET_CHUNK, RET_QK), lambda b, c: (row(b, c), 1)),
                  pl.BlockSpec((RET_CHUNK, RET_V), lambda b, c: (row(b, c), 1)),
                  pl.BlockSpec((RET_CHUNK, RET_V), lambda b, c: (row(b, c), 2))],
        out_specs=pl.BlockSpec((RET_CHUNK, RET_V), lambda b, c: (row(b, c), 0)),
        out_shape=jax.ShapeDtypeStruct((m, RET_V), BF16),
        scratch_shapes=[pltpu.VMEM((RET_HEADS, RET_DK, RET_DV), F32),
                        pltpu.VMEM((RET_HEADS, RET_CHUNK, RET_CHUNK), F32),
                        pltpu.VMEM((RET_HEADS, RET_CHUNK, RET_DK), BF16),
                        pltpu.VMEM((RET_HEADS, RET_CHUNK, RET_DK), BF16)],
        compiler_params=_params(("parallel", "arbitrary")),
        name="ret_mix",
    )(proj, proj, proj, proj)


def _log_sigmoid(x):
    return jnp.minimum(x, 0.0) - jnp.log1p(jnp.exp(-jnp.abs(x)))


def _gate_logs(pre, is_forget):
    capped = GATE_SOFTCAP * jnp.tanh(pre / GATE_SOFTCAP)
    return jnp.where(is_forget, _log_sigmoid(capped), capped)


def _ml_proj_kernel(x_ref, g_ref, w_ref, wg_ref, wgt_ref, b_ref, bt_ref,
                    o_ref, gcol_ref, grow_ref, h_ref):
    j = pl.program_id(1)

    @pl.when(j == 0)
    def _():
        _norm_rows(h_ref, x_ref, g_ref)
        h = h_ref[...]
        pc = jnp.dot(h, wg_ref[...], preferred_element_type=F32) + b_ref[...]
        lane = lax.broadcasted_iota(jnp.int32, pc.shape, 1)
        gcol_ref[...] = _gate_logs(pc, lane >= ML_HEADS)
        pr = lax.dot_general(wgt_ref[...], h, NT_DIMS, preferred_element_type=F32) + bt_ref[...]
        sub = lax.broadcasted_iota(jnp.int32, pr.shape, 0)
        grow_ref[...] = _gate_logs(pr, sub >= ML_HEADS)

    o_ref[...] = jnp.dot(h_ref[...], w_ref[...], preferred_element_type=F32).astype(BF16)


def ml_proj(x, norm_g, g_idx, w, layer, wg, wgt, b, bt, tm=1024, tn=2048):
    m = x.shape[0]
    ng = 2 * ML_HEADS
    return pl.pallas_call(
        _ml_proj_kernel,
        grid=(m // tm, ML_MAIN // tn),
        in_specs=[pl.BlockSpec((tm, D_MODEL), lambda i, j: (i, 0)),
                  pl.BlockSpec((None, 1, D_MODEL), lambda i, j: (g_idx, 0, 0)),
                  pl.BlockSpec((None, D_MODEL, tn), lambda i, j: (layer, 0, j)),
                  pl.BlockSpec((D_MODEL, GATE_LANES), lambda i, j: (0, 0)),
                  pl.BlockSpec((ng, D_MODEL), lambda i, j: (0, 0)),
                  pl.BlockSpec((1, GATE_LANES), lambda i, j: (0, 0)),
                  pl.BlockSpec((ng, 1), lambda i, j: (0, 0))],
        out_specs=[pl.BlockSpec((tm, tn), lambda i, j: (i, j)),
                   pl.BlockSpec((tm, GATE_LANES), lambda i, j: (i, 0)),
                   pl.BlockSpec((ng, tm), lambda i, j: (0, i))],
        out_shape=[jax.ShapeDtypeStruct((m, ML_MAIN), BF16),
                   jax.ShapeDtypeStruct((m, GATE_LANES), F32),
                   jax.ShapeDtypeStruct((ng, m), F32)],
        scratch_shapes=[pltpu.VMEM((tm, D_MODEL), BF16)],
        compiler_params=_params(("parallel", "arbitrary")),
        name="ml_proj",
    )(x, norm_g, w, wg, wgt, b, bt)


def _ml_kernel(q_ref, k_ref, v_ref, o_ref, gcol_ref, grow_ref, ng_ref, y_ref,
               c_ref, n_ref, m_ref):
    c = pl.program_id(1)

    @pl.when(c == 0)
    def _():
        c_ref[...] = jnp.zeros_like(c_ref)
        n_ref[...] = jnp.zeros_like(n_ref)
        m_ref[...] = jnp.zeros_like(m_ref)

    s_idx = lax.broadcasted_iota(jnp.int32, (CHUNK, CHUNK), 0)
    n_idx = lax.broadcasted_iota(jnp.int32, (CHUNK, CHUNK), 1)
    causal = s_idx <= n_idx
    eye = s_idx == n_idx
    lower = (s_idx >= n_idx).astype(F32)
    upper = causal.astype(F32)

    gcol = gcol_ref[...]
    grow = grow_ref[...]
    hi = lax.Precision.HIGHEST
    cum_col = jnp.dot(lower, gcol, precision=hi, preferred_element_type=F32)
    cum_row = jnp.dot(grow, upper, precision=hi, preferred_element_type=F32)
    key_col = gcol - pltpu.roll(cum_col, GATE_LANES - ML_HEADS, axis=1)
    scale = ML_DQK ** -0.5
    log_scale = math.log(scale)

    heads = range(ML_HEADS)
    q = [q_ref[:, h * ML_DQK:(h + 1) * ML_DQK] for h in heads]
    k = [k_ref[:, h * ML_DQK:(h + 1) * ML_DQK] for h in heads]
    v = [v_ref[:, h * ML_DV:(h + 1) * ML_DV] for h in heads]


    raw, qn, qc = [], [], []
    for h in heads:
        raw.append(lax.dot_general(k[h], q[h], NT_DIMS, preferred_element_type=F32))
        qn.append(lax.dot_general(n_ref[h].astype(BF16), q[h], NT_DIMS,
                                  preferred_element_type=F32)[0:1, :])
        qc.append(jnp.dot(q[h], c_ref[h].astype(BF16),
                          preferred_element_type=F32).astype(BF16))

    lhs_t, kw_t, w_rows, carry, m_new = [], [], [], [], []
    for h in heads:
        brow = cum_row[ML_HEADS + h:ML_HEADS + h + 1, :]
        irow = grow[h:h + 1, :]
        kcol = key_col[:, h:h + 1]
        m_st = m_ref[h][0:1, 0:1]

        d_log = jnp.where(causal, brow + kcol, -jnp.inf)
        inter_log = brow + m_st
        m_out = jnp.maximum(inter_log, jnp.max(d_log, axis=0, keepdims=True))
        d_w = jnp.exp(d_log - (m_out - log_scale))
        inter_w = jnp.exp(inter_log - m_out)
        scores = raw[h] * d_w
        den = jnp.sum(scores, axis=0, keepdims=True) + inter_w * qn[h]
        r = 1.0 / jnp.maximum(jnp.abs(den), jnp.exp(-m_out))
        p_t = (scores * r).astype(BF16)
        diag = jnp.where(eye, inter_w * r, 0.0).astype(BF16)
        lhs_t.append(jnp.concatenate([p_t, diag], axis=0))

        b_tot = brow[:, CHUNK - 1:CHUNK]
        w_log = b_tot - brow + irow
        m_nw = jnp.maximum(b_tot + m_st, jnp.max(w_log, axis=1, keepdims=True))
        w_s = jnp.exp(w_log - m_nw) * scale
        carry.append(jnp.exp(b_tot + m_st - m_nw))
        m_new.append(m_nw)
        kw_t.append((k[h].astype(F32).T * w_s).astype(BF16))
        w_rows.append(jnp.broadcast_to(w_s, (8, CHUNK)).astype(BF16))

    h_out = []
    for h in heads:
        rhs = jnp.concatenate([v[h], qc[h]], axis=0)
        h_out.append(lax.dot_general(lhs_t[h], rhs, TN_DIMS, preferred_element_type=F32))
        c_ref[h] = carry[h] * c_ref[h] + jnp.dot(kw_t[h], v[h], preferred_element_type=F32)
        n_ref[h] = carry[h] * n_ref[h] + jnp.dot(w_rows[h], k[h], preferred_element_type=F32)
        m_ref[h] = jnp.broadcast_to(m_new[h], m_ref.shape[1:])

    for h in heads:
        cols = slice(h * ML_DV, (h + 1) * ML_DV)
        y = h_out[h]
        yn = y * lax.rsqrt(jnp.mean(y * y, axis=-1, keepdims=True) + EPS) * ng_ref[:, cols]
        y_ref[:, cols] = yn.astype(BF16) * jax.nn.sigmoid(o_ref[:, cols])


def ml_mix(proj, gcol, grow, norm_g, batch, seq):
    m = batch * seq
    nc = seq // CHUNK
    row = lambda b, c: b * nc + c
    return pl.pallas_call(
        _ml_kernel,
        grid=(batch, nc),
        in_specs=[pl.BlockSpec((CHUNK, ML_QK), lambda b, c: (row(b, c), 0)),
                  pl.BlockSpec((CHUNK, ML_QK), lambda b, c: (row(b, c), 1)),
                  pl.BlockSpec((CHUNK, ML_V), lambda b, c: (row(b, c), 1)),
                  pl.BlockSpec((CHUNK, ML_V), lambda b, c: (row(b, c), 2)),
                  pl.BlockSpec((CHUNK, GATE_LANES), lambda b, c: (row(b, c), 0)),
                  pl.BlockSpec((2 * ML_HEADS, CHUNK), lambda b, c: (0, row(b, c))),
                  pl.BlockSpec((1, ML_V), lambda b, c: (0, 0))],
        out_specs=pl.BlockSpec((CHUNK, ML_V), lambda b, c: (row(b, c), 0)),
        out_shape=jax.ShapeDtypeStruct((m, ML_V), BF16),
        scratch_shapes=[pltpu.VMEM((ML_HEADS, ML_DQK, ML_DV), F32),
                        pltpu.VMEM((ML_HEADS, 8, ML_DQK), F32),
                        pltpu.VMEM((ML_HEADS, 8, 128), F32)],
        compiler_params=_params(("parallel", "arbitrary")),
        name="ml_mix",
    )(proj, proj, proj, proj, gcol, grow, norm_g)


def _out_kernel(y_ref, w_ref, g_ref, x_ref, o_ref):
    k = pl.program_id(1)

    @pl.when(k == 0)
    def _():
        o_ref[...] = jnp.dot(y_ref[...], w_ref[...], preferred_element_type=F32)

    @pl.when(k > 0)
    def _():
        o_ref[...] += jnp.dot(y_ref[...], w_ref[...], preferred_element_type=F32)

    @pl.when(k == pl.num_programs(1) - 1)
    def _():
        _norm_rows(o_ref, o_ref, g_ref, res_ref=x_ref)


def out_proj(y, w, layer, norm_g, g_idx, x, tm=1024, tk=1024):
    m, kdim = y.shape
    return pl.pallas_call(
        _out_kernel,
        grid=(m // tm, kdim // tk),
        in_specs=[pl.BlockSpec((tm, tk), lambda i, k: (i, k)),
                  pl.BlockSpec((None, tk, D_MODEL), lambda i, k: (layer, k, 0)),
                  pl.BlockSpec((None, 1, D_MODEL), lambda i, k: (g_idx, 0, 0)),
                  pl.BlockSpec((tm, D_MODEL), lambda i, k: (i, 0))],
        out_specs=pl.BlockSpec((tm, D_MODEL), lambda i, k: (i, 0)),
        out_shape=jax.ShapeDtypeStruct((m, D_MODEL), F32),
        compiler_params=_params(("parallel", "arbitrary")),
        name="out_proj",
    )(y, w, norm_g, x)


def _mlp_kernel(x_ref, g_in_ref, w1_ref, w2_ref, g_out_ref, o_ref, h_ref):
    f = pl.program_id(1)

    @pl.when(f == 0)
    def _():
        _norm_rows(h_ref, x_ref, g_in_ref)
        o_ref[...] = jnp.zeros_like(o_ref)

    hid = jnp.maximum(jnp.dot(h_ref[...], w1_ref[...], preferred_element_type=F32), 0.0)
    hid = (hid * hid).astype(BF16)
    o_ref[...] += jnp.dot(hid, w2_ref[...], preferred_element_type=F32)

    @pl.when(f == pl.num_programs(1) - 1)
    def _():
        _norm_rows(o_ref, o_ref, g_out_ref, res_ref=x_ref)


def mlp(x, norm_g, g_in_idx, g_out_idx, w1, w2, layer, tm=1024, tf=512):
    m = x.shape[0]
    return pl.pallas_call(
        _mlp_kernel,
        grid=(m // tm, D_FF // tf),
        in_specs=[pl.BlockSpec((tm, D_MODEL), lambda i, f: (i, 0)),
                  pl.BlockSpec((None, 1, D_MODEL), lambda i, f: (g_in_idx, 0, 0)),
                  pl.BlockSpec((None, D_MODEL, tf), lambda i, f: (layer, 0, f)),
                  pl.BlockSpec((None, tf, D_MODEL), lambda i, f: (layer, f, 0)),
                  pl.BlockSpec((None, 1, D_MODEL), lambda i, f: (g_out_idx, 0, 0))],
        out_specs=pl.BlockSpec((tm, D_MODEL), lambda i, f: (i, 0)),
        out_shape=jax.ShapeDtypeStruct((m, D_MODEL), F32),
        scratch_shapes=[pltpu.VMEM((tm, D_MODEL), BF16)],
        compiler_params=_params(("parallel", "arbitrary")),
        name="mlp",
    )(x, norm_g, w1, w2, norm_g)


def kernel(x, positions, norm_g, ret_w_in, ret_w_out, mlstm_w_in, mlstm_b_gate,
           mlstm_norm_g, mlstm_w_out, mlp_w1, mlp_w2):
    batch, seq, d = x.shape
    m = batch * seq
    xf = x.reshape(m, d)

    inv_freq = jnp.power(ROPE_BASE, -jnp.linspace(0.0, 1.0, RET_DK // 2, dtype=F32))
    cos, sin = rope_tables(positions, inv_freq)

    gains = norm_g.astype(F32).reshape(DEPTH * 4, 1, d)
    ret_w_in_b = ret_w_in.astype(BF16)
    ret_w_out_b = ret_w_out.astype(BF16)
    ml_w_in_b = mlstm_w_in.astype(BF16)
    ml_w_out_b = mlstm_w_out.astype(BF16)
    w1_b = mlp_w1.astype(BF16)
    w2_b = mlp_w2.astype(BF16)
    n_gate = 2 * ML_HEADS

    for i in range(DEPTH):
        j = i // 2
        if i % 2 == 0:
            proj = ret_proj(xf, gains, 4 * i, ret_w_in_b, j, cos, sin)
            y = ret_mix(proj, batch, seq)
            xf = out_proj(y, ret_w_out_b, j, gains, 4 * i + 1, xf)
        else:
            w_gate = ml_w_in_b[j, :, ML_MAIN:]
            wg = jnp.pad(w_gate, ((0, 0), (0, GATE_LANES - n_gate)))
            bias = mlstm_b_gate[j].astype(F32)
            b = jnp.pad(bias, (0, GATE_LANES - n_gate)).reshape(1, GATE_LANES)
            proj, gcol, grow = ml_proj(xf, gains, 4 * i, ml_w_in_b, j, wg, w_gate.T, b,
                                       bias.reshape(n_gate, 1))
            y = ml_mix(proj, gcol, grow, mlstm_norm_g[j].reshape(1, ML_V).astype(F32),
                       batch, seq)
            xf = out_proj(y, ml_w_out_b, j, gains, 4 * i + 1, xf)
        xf = mlp(xf, gains, 4 * i + 2, 4 * i + 3, w1_b, w2_b, i)
    return xf.reshape(batch, seq, d)
```

```python
import functools
import math

import jax
import jax.numpy as jnp
from jax import lax
from jax.experimental import pallas as pl
from jax.experimental.pallas import tpu as pltpu

F32 = jnp.float32
BF16 = jnp.bfloat16

D_MODEL = 2048
DEPTH = 4
CHUNK = 128
RET_CHUNK = 256
EPS = 1e-6
D_FF = 4 * D_MODEL

RET_HEADS = 8
RET_DK = 256
RET_QK = RET_HEADS * RET_DK
RET_V = 2 * D_MODEL
RET_DV = RET_V // RET_HEADS
RET_IN = 2 * RET_QK + 2 * RET_V
ROPE_BASE = 10000.0

ML_HEADS = 8
ML_QK = D_MODEL // 2
ML_DQK = ML_QK // ML_HEADS
ML_V = D_MODEL
ML_DV = ML_V // ML_HEADS
ML_MAIN = 2 * ML_QK + 2 * ML_V
GATE_SOFTCAP = 15.0
GATE_LANES = 128

VMEM_LIMIT = 60 * 1024 * 1024

NT_DIMS = (((1,), (1,)), ((), ()))
TN_DIMS = (((0,), (0,)), ((), ()))


def _rms(x, g):
    ms = jnp.mean(x * x, axis=-1, keepdims=True)
    return x * lax.rsqrt(ms + EPS) * g


NORM_ROWS = 256
NORM_SUB = 64


def _norm_rows(dst_ref, src_ref, g_ref, res_ref=None):
    g = g_ref[...]

    def body(r, carry):
        rows = pl.ds(pl.multiple_of(r * NORM_ROWS, NORM_ROWS), NORM_ROWS)
        sq = src_ref[rows, :]
        inv = lax.rsqrt(jnp.mean(sq * sq, axis=-1, keepdims=True) + EPS)
        for t in range(NORM_ROWS // NORM_SUB):
            sub = pl.ds(pl.multiple_of(r * NORM_ROWS + t * NORM_SUB, NORM_SUB), NORM_SUB)
            v = src_ref[sub, :] * inv[t * NORM_SUB:(t + 1) * NORM_SUB] * g
            if res_ref is not None:
                v = res_ref[sub, :] + v
            dst_ref[sub, :] = v.astype(dst_ref.dtype)
        return carry

    lax.fori_loop(0, src_ref.shape[0] // NORM_ROWS, body, 0)


def _params(sem, **kw):
    return pltpu.CompilerParams(dimension_semantics=sem, vmem_limit_bytes=VMEM_LIMIT, **kw)


def _rope_kernel(pos_ref, freq_ref, cos_ref, sin_ref):
    ang = pos_ref[...].astype(F32) * freq_ref[...]
    cos_ref[...] = jnp.cos(ang)
    sin_ref[...] = jnp.sin(ang)


def rope_tables(positions, inv_freq, tm=2048):
    m = positions.size
    pos = positions.reshape(m, 1)
    half = inv_freq.shape[-1]
    return pl.pallas_call(
        _rope_kernel,
        grid=(m // tm,),
        in_specs=[pl.BlockSpec((tm, 1), lambda i: (i, 0)),
                  pl.BlockSpec((1, half), lambda i: (0, 0))],
        out_specs=[pl.BlockSpec((tm, half), lambda i: (i, 0)),
                   pl.BlockSpec((tm, half), lambda i: (i, 0))],
        out_shape=[jax.ShapeDtypeStruct((m, half), F32)] * 2,
        compiler_params=_params(("parallel",)),
        name="rope_tables",
    )(pos, inv_freq.reshape(1, half))


def _ret_proj_kernel(x_ref, g_ref, w_ref, cos_ref, sin_ref, o_ref, h_ref, *, tn):
    j = pl.program_id(1)
    n_rot = 2 * RET_QK // tn

    @pl.when(j == 0)
    def _():
        _norm_rows(h_ref, x_ref, g_ref)

    @pl.when(j < n_rot)
    def _():
        res = jnp.dot(h_ref[...], w_ref[...], preferred_element_type=F32)
        scale = jnp.where(j >= n_rot // 2, RET_DK ** -0.5, 1.0).astype(F32)
        c = cos_ref[...] * scale
        s = sin_ref[...] * scale
        half = RET_DK // 2
        for hh in range(tn // RET_DK):
            lo = hh * RET_DK
            t1 = res[:, lo:lo + half]
            t2 = res[:, lo + half:lo + RET_DK]
            o_ref[:, lo:lo + half] = (t1 * c - t2 * s).astype(BF16)
            o_ref[:, lo + half:lo + RET_DK] = (t2 * c + t1 * s).astype(BF16)

    @pl.when(j >= n_rot)
    def _():
        o_ref[...] = jnp.dot(h_ref[...], w_ref[...],
                             preferred_element_type=F32).astype(BF16)


def ret_proj(x, norm_g, g_idx, w, layer, cos, sin, tm=1024, tn=2048):
    m = x.shape[0]
    return pl.pallas_call(
        functools.partial(_ret_proj_kernel, tn=tn),
        grid=(m // tm, RET_IN // tn),
        in_specs=[pl.BlockSpec((tm, D_MODEL), lambda i, j: (i, 0)),
                  pl.BlockSpec((None, 1, D_MODEL), lambda i, j: (g_idx, 0, 0)),
                  pl.BlockSpec((None, D_MODEL, tn), lambda i, j: (layer, 0, j)),
                  pl.BlockSpec((tm, RET_DK // 2), lambda i, j: (i, 0)),
                  pl.BlockSpec((tm, RET_DK // 2), lambda i, j: (i, 0))],
        out_specs=pl.BlockSpec((tm, tn), lambda i, j: (i, j)),
        out_shape=jax.ShapeDtypeStruct((m, RET_IN), BF16),
        scratch_shapes=[pltpu.VMEM((tm, D_MODEL), BF16)],
        compiler_params=_params(("parallel", "arbitrary")),
        name="ret_proj",
    )(x, norm_g, w, cos, sin)


def _ret_log_gamma(h):
    return math.log1p(-(2.0 ** (-5.0 - h)))


def _ret_kernel(q_ref, k_ref, v_ref, g_ref, y_ref, st_ref, dec_ref, xi_ref, zeta_ref):
    c = pl.program_id(1)

    @pl.when(c == 0)
    def _():
        st_ref[...] = jnp.zeros_like(st_ref)
        n = lax.broadcasted_iota(jnp.int32, (RET_CHUNK, RET_CHUNK), 0)
        mcol = lax.broadcasted_iota(jnp.int32, (RET_CHUNK, RET_CHUNK), 1)
        rel = (n - mcol).astype(F32)
        row = lax.broadcasted_iota(jnp.int32, (RET_CHUNK, RET_DK), 0).astype(F32)
        for h in range(RET_HEADS):
            lg = _ret_log_gamma(h)
            dec_ref[h] = jnp.where(rel >= 0, jnp.exp(jnp.maximum(rel, 0.0) * lg), 0.0)
            xi_ref[h] = jnp.exp((row + 1.0) * lg).astype(BF16)
            zeta_ref[h] = jnp.exp((RET_CHUNK - 1.0 - row) * lg).astype(BF16)

    heads = range(RET_HEADS)
    q = [q_ref[:, h * RET_DK:(h + 1) * RET_DK] for h in heads]
    k = [k_ref[:, h * RET_DK:(h + 1) * RET_DK] for h in heads]
    v = [v_ref[:, h * RET_DV:(h + 1) * RET_DV] for h in heads]


    raw = [lax.dot_general(q[h], k[h], NT_DIMS, preferred_element_type=F32) for h in heads]

    lhs, rhs, kz = [], [], []
    for h in heads:
        lhs.append(jnp.concatenate([(raw[h] * dec_ref[h]).astype(BF16), q[h] * xi_ref[h]],
                                   axis=1))
        rhs.append(jnp.concatenate([v[h], st_ref[h].astype(BF16)], axis=0))
        kz.append(k[h] * zeta_ref[h])

    y = []
    for h in heads:
        g_chunk = math.exp(RET_CHUNK * _ret_log_gamma(h))
        y.append(jnp.dot(lhs[h], rhs[h], preferred_element_type=F32))
        st_ref[h] = st_ref[h] * g_chunk + lax.dot_general(kz[h], v[h], TN_DIMS,
                                                          preferred_element_type=F32)

    for h in heads:
        cols = slice(h * RET_DV, (h + 1) * RET_DV)
        mu = jnp.mean(y[h], axis=-1, keepdims=True)
        yc = y[h] - mu
        var = jnp.mean(yc * yc, axis=-1, keepdims=True)
        yn = yc * lax.rsqrt(var + EPS)
        gh = g_ref[:, cols]
        y_ref[:, cols] = (gh * jax.nn.sigmoid(gh)) * yn.astype(BF16)


def ret_mix(proj, batch, seq):
    m = batch * seq
    nc = seq // RET_CHUNK
    row = lambda b, c: b * nc + c
    return pl.pallas_call(
        _ret_kernel,
        grid=(batch, nc),
        in_specs=[pl.BlockSpec((RET_CHUNK, RET_QK), lambda b, c: (row(b, c), 0)),
                  pl.BlockSpec((RET_CHUNK, RET_QK), lambda b, c: (row(b, c), 1)),
                  pl.BlockSpec((RET_CHUNK, RET_V), lambda b, c: (row(b, c), 1)),
                  pl.BlockSpec((RET_CHUNK, RET_V), lambda b, c: (row(b, c), 2))],
        out_specs=pl.BlockSpec((RET_CHUNK, RET_V), lambda b, c: (row(b, c), 0)),
        out_shape=jax.ShapeDtypeStruct((m, RET_V), BF16),
        scratch_shapes=[pltpu.VMEM((RET_HEADS, RET_DK, RET_DV), F32),
                        pltpu.VMEM((RET_HEADS, RET_CHUNK, RET_CHUNK), F32),
                        pltpu.VMEM((RET_HEADS, RET_CHUNK, RET_DK), BF16),
                        pltpu.VMEM((RET_HEADS, RET_CHUNK, RET_DK), BF16)],
        compiler_params=_params(("parallel", "arbitrary")),
        name="ret_mix",
    )(proj, proj, proj, proj)


def _log_sigmoid(x):
    return jnp.minimum(x, 0.0) - jnp.log1p(jnp.exp(-jnp.abs(x)))


def _gate_logs(pre, is_forget):
    capped = GATE_SOFTCAP * jnp.tanh(pre / GATE_SOFTCAP)
    return jnp.where(is_forget, _log_sigmoid(capped), capped)


def _ml_proj_kernel(x_ref, g_ref, w_ref, wg_ref, b_ref, o_ref, gcol_ref, grow_ref, h_ref):
    j = pl.program_id(1)

    @pl.when(j == 0)
    def _():
        _norm_rows(h_ref, x_ref, g_ref)
        pre = jnp.dot(h_ref[...], wg_ref[...], preferred_element_type=F32) + b_ref[...]
        lane = lax.broadcasted_iota(jnp.int32, pre.shape, 1)
        logs = _gate_logs(pre, lane >= ML_HEADS)
        gcol_ref[...] = logs
        grow_ref[...] = logs.T[:2 * ML_HEADS, :]

    o_ref[...] = jnp.dot(h_ref[...], w_ref[...], preferred_element_type=F32).astype(BF16)


def ml_proj(x, norm_g, g_idx, w, layer, wg, b, tm=1024, tn=2048):
    m = x.shape[0]
    ng = 2 * ML_HEADS
    return pl.pallas_call(
        _ml_proj_kernel,
        grid=(m // tm, ML_MAIN // tn),
        in_specs=[pl.BlockSpec((tm, D_MODEL), lambda i, j: (i, 0)),
                  pl.BlockSpec((None, 1, D_MODEL), lambda i, j: (g_idx, 0, 0)),
                  pl.BlockSpec((None, D_MODEL, tn), lambda i, j: (layer, 0, j)),
                  pl.BlockSpec((D_MODEL, GATE_LANES), lambda i, j: (0, 0)),
                  pl.BlockSpec((1, GATE_LANES), lambda i, j: (0, 0))],
        out_specs=[pl.BlockSpec((tm, tn), lambda i, j: (i, j)),
                   pl.BlockSpec((tm, GATE_LANES), lambda i, j: (i, 0)),
                   pl.BlockSpec((ng, tm), lambda i, j: (0, i))],
        out_shape=[jax.ShapeDtypeStruct((m, ML_MAIN), BF16),
                   jax.ShapeDtypeStruct((m, GATE_LANES), F32),
                   jax.ShapeDtypeStruct((ng, m), F32)],
        scratch_shapes=[pltpu.VMEM((tm, D_MODEL), BF16)],
        compiler_params=_params(("parallel", "arbitrary")),
        name="ml_proj",
    )(x, norm_g, w, wg, b)


def _ml_kernel(q_ref, k_ref, v_ref, o_ref, gcol_ref, grow_ref, ng_ref, y_ref,
               c_ref, n_ref, m_ref):
    c = pl.program_id(1)

    @pl.when(c == 0)
    def _():
        c_ref[...] = jnp.zeros_like(c_ref)
        n_ref[...] = jnp.zeros_like(n_ref)
        m_ref[...] = jnp.zeros_like(m_ref)

    s_idx = lax.broadcasted_iota(jnp.int32, (CHUNK, CHUNK), 0)
    n_idx = lax.broadcasted_iota(jnp.int32, (CHUNK, CHUNK), 1)
    causal = s_idx <= n_idx
    eye = s_idx == n_idx
    lower = (s_idx >= n_idx).astype(F32)
    upper = causal.astype(F32)

    gcol = gcol_ref[...]
    grow = grow_ref[...]
    hi = lax.Precision.HIGHEST
    cum_col = jnp.dot(lower, gcol, precision=hi, preferred_element_type=F32)
    cum_row = jnp.dot(grow, upper, precision=hi, preferred_element_type=F32)
    key_col = gcol - pltpu.roll(cum_col, GATE_LANES - ML_HEADS, axis=1)
    scale = ML_DQK ** -0.5
    log_scale = math.log(scale)

    heads = range(ML_HEADS)
    q = [q_ref[:, h * ML_DQK:(h + 1) * ML_DQK] for h in heads]
    k = [k_ref[:, h * ML_DQK:(h + 1) * ML_DQK] for h in heads]
    v = [v_ref[:, h * ML_DV:(h + 1) * ML_DV] for h in heads]


    raw, qn, qc = [], [], []
    for h in heads:
        raw.append(lax.dot_general(k[h], q[h], NT_DIMS, preferred_element_type=F32))
        qn.append(lax.dot_general(n_ref[h].astype(BF16), q[h], NT_DIMS,
                                  preferred_element_type=F32)[0:1, :])
        qc.append(jnp.dot(q[h], c_ref[h].astype(BF16),
                          preferred_element_type=F32).astype(BF16))

    lhs_t, kw_t, w_rows, carry, m_new = [], [], [], [], []
    for h in heads:
        brow = cum_row[ML_HEADS + h:ML_HEADS + h + 1, :]
        irow = grow[h:h + 1, :]
        kcol = key_col[:, h:h + 1]
        m_st = m_ref[h][0:1, 0:1]

        d_log = jnp.where(causal, brow + kcol, -jnp.inf)
        inter_log = brow + m_st
        m_out = jnp.maximum(inter_log, jnp.max(d_log, axis=0, keepdims=True))
        d_w = jnp.exp(d_log - (m_out - log_scale))
        inter_w = jnp.exp(inter_log - m_out)
        scores = raw[h] * d_w
        den = jnp.sum(scores, axis=0, keepdims=True) + inter_w * qn[h]
        r = 1.0 / jnp.maximum(jnp.abs(den), jnp.exp(-m_out))
        p_t = (scores * r).astype(BF16)
        diag = jnp.where(eye, inter_w * r, 0.0).astype(BF16)
        lhs_t.append(jnp.concatenate([p_t, diag], axis=0))

        b_tot = brow[:, CHUNK - 1:CHUNK]
        w_log = b_tot - brow + irow
        m_nw = jnp.maximum(b_tot + m_st, jnp.max(w_log, axis=1, keepdims=True))
        w_s = jnp.exp(w_log - m_nw) * scale
        carry.append(jnp.exp(b_tot + m_st - m_nw))
        m_new.append(m_nw)
        kw_t.append((k[h].astype(F32).T * w_s).astype(BF16))
        w_rows.append(jnp.broadcast_to(w_s, (8, CHUNK)).astype(BF16))

    h_out = []
    for h in heads:
        rhs = jnp.concatenate([v[h], qc[h]], axis=0)
        h_out.append(lax.dot_general(lhs_t[h], rhs, TN_DIMS, preferred_element_type=F32))
        c_ref[h] = carry[h] * c_ref[h] + jnp.dot(kw_t[h], v[h], preferred_element_type=F32)
        n_ref[h] = carry[h] * n_ref[h] + jnp.dot(w_rows[h], k[h], preferred_element_type=F32)
        m_ref[h] = jnp.broadcast_to(m_new[h], m_ref.shape[1:])

    for h in heads:
        cols = slice(h * ML_DV, (h + 1) * ML_DV)
        y = h_out[h]
        yn = y * lax.rsqrt(jnp.mean(y * y, axis=-1, keepdims=True) + EPS) * ng_ref[:, cols]
        y_ref[:, cols] = yn.astype(BF16) * jax.nn.sigmoid(o_ref[:, cols])


def ml_mix(proj, gcol, grow, norm_g, batch, seq):
    m = batch * seq
    nc = seq // CHUNK
    row = lambda b, c: b * nc + c
    return pl.pallas_call(
        _ml_kernel,
        grid=(batch, nc),
        in_specs=[pl.BlockSpec((CHUNK, ML_QK), lambda b, c: (row(b, c), 0)),
                  pl.BlockSpec((CHUNK, ML_QK), lambda b, c: (row(b, c), 1)),
                  pl.BlockSpec((CHUNK, ML_V), lambda b, c: (row(b, c), 1)),
                  pl.BlockSpec((CHUNK, ML_V), lambda b, c: (row(b, c), 2)),
                  pl.BlockSpec((CHUNK, GATE_LANES), lambda b, c: (row(b, c), 0)),
                  pl.BlockSpec((2 * ML_HEADS, CHUNK), lambda b, c: (0, row(b, c))),
                  pl.BlockSpec((1, ML_V), lambda b, c: (0, 0))],
        out_specs=pl.BlockSpec((CHUNK, ML_V), lambda b, c: (row(b, c), 0)),
        out_shape=jax.ShapeDtypeStruct((m, ML_V), BF16),
        scratch_shapes=[pltpu.VMEM((ML_HEADS, ML_DQK, ML_DV), F32),
                        pltpu.VMEM((ML_HEADS, 8, ML_DQK), F32),
                        pltpu.VMEM((ML_HEADS, 8, 128), F32)],
        compiler_params=_params(("parallel", "arbitrary")),
        name="ml_mix",
    )(proj, proj, proj, proj, gcol, grow, norm_g)


def _out_kernel(y_ref, w_ref, g_ref, x_ref, o_ref):
    k = pl.program_id(1)

    @pl.when(k == 0)
    def _():
        o_ref[...] = jnp.dot(y_ref[...], w_ref[...], preferred_element_type=F32)

    @pl.when(k > 0)
    def _():
        o_ref[...] += jnp.dot(y_ref[...], w_ref[...], preferred_element_type=F32)

    @pl.when(k == pl.num_programs(1) - 1)
    def _():
        _norm_rows(o_ref, o_ref, g_ref, res_ref=x_ref)


def out_proj(y, w, layer, norm_g, g_idx, x, tm=1024, tk=1024):
    m, kdim = y.shape
    return pl.pallas_call(
        _out_kernel,
        grid=(m // tm, kdim // tk),
        in_specs=[pl.BlockSpec((tm, tk), lambda i, k: (i, k)),
                  pl.BlockSpec((None, tk, D_MODEL), lambda i, k: (layer, k, 0)),
                  pl.BlockSpec((None, 1, D_MODEL), lambda i, k: (g_idx, 0, 0)),
                  pl.BlockSpec((tm, D_MODEL), lambda i, k: (i, 0))],
        out_specs=pl.BlockSpec((tm, D_MODEL), lambda i, k: (i, 0)),
        out_shape=jax.ShapeDtypeStruct((m, D_MODEL), F32),
        compiler_params=_params(("parallel", "arbitrary")),
        name="out_proj",
    )(y, w, norm_g, x)


def _mlp_kernel(x_ref, g_in_ref, w1_ref, w2_ref, g_out_ref, o_ref, h_ref):
    f = pl.program_id(1)

    @pl.when(f == 0)
    def _():
        _norm_rows(h_ref, x_ref, g_in_ref)
        o_ref[...] = jnp.zeros_like(o_ref)

    hid = jnp.maximum(jnp.dot(h_ref[...], w1_ref[...], preferred_element_type=F32), 0.0)
    hid = (hid * hid).astype(BF16)
    o_ref[...] += jnp.dot(hid, w2_ref[...], preferred_element_type=F32)

    @pl.when(f == pl.num_programs(1) - 1)
    def _():
        _norm_rows(o_ref, o_ref, g_out_ref, res_ref=x_ref)


def mlp(x, norm_g, g_in_idx, g_out_idx, w1, w2, layer, tm=1024, tf=1024):
    m = x.shape[0]
    return pl.pallas_call(
        _mlp_kernel,
        grid=(m // tm, D_FF // tf),
        in_specs=[pl.BlockSpec((tm, D_MODEL), lambda i, f: (i, 0)),
                  pl.BlockSpec((None, 1, D_MODEL), lambda i, f: (g_in_idx, 0, 0)),
                  pl.BlockSpec((None, D_MODEL, tf), lambda i, f: (layer, 0, f)),
                  pl.BlockSpec((None, tf, D_MODEL), lambda i, f: (layer, f, 0)),
                  pl.BlockSpec((None, 1, D_MODEL), lambda i, f: (g_out_idx, 0, 0))],
        out_specs=pl.BlockSpec((tm, D_MODEL), lambda i, f: (i, 0)),
        out_shape=jax.ShapeDtypeStruct((m, D_MODEL), F32),
        scratch_shapes=[pltpu.VMEM((tm, D_MODEL), BF16)],
        compiler_params=_params(("parallel", "arbitrary")),
        name="mlp",
    )(x, norm_g, w1, w2, norm_g)


def kernel(x, positions, norm_g, ret_w_in, ret_w_out, mlstm_w_in, mlstm_b_gate,
           mlstm_norm_g, mlstm_w_out, mlp_w1, mlp_w2):
    batch, seq, d = x.shape
    m = batch * seq
    xf = x.reshape(m, d)

    inv_freq = jnp.power(ROPE_BASE, -jnp.linspace(0.0, 1.0, RET_DK // 2, dtype=F32))
    cos, sin = rope_tables(positions, inv_freq)

    gains = norm_g.astype(F32).reshape(DEPTH * 4, 1, d)
    ret_w_in_b = ret_w_in.astype(BF16)
    ret_w_out_b = ret_w_out.astype(BF16)
    ml_w_in_b = mlstm_w_in[:, :, :ML_MAIN].astype(BF16)
    ml_w_out_b = mlstm_w_out.astype(BF16)
    w1_b = mlp_w1.astype(BF16)
    w2_b = mlp_w2.astype(BF16)
    n_gate = 2 * ML_HEADS

    for i in range(DEPTH):
        j = i // 2
        if i % 2 == 0:
            proj = ret_proj(xf, gains, 4 * i, ret_w_in_b, j, cos, sin)
            y = ret_mix(proj, batch, seq)
            xf = out_proj(y, ret_w_out_b, j, gains, 4 * i + 1, xf)
        else:
            w_gate = mlstm_w_in[j, :, ML_MAIN:].astype(BF16)
            wg = jnp.pad(w_gate, ((0, 0), (0, GATE_LANES - n_gate)))
            bias = mlstm_b_gate[j].astype(F32)
            b = jnp.pad(bias, (0, GATE_LANES - n_gate)).reshape(1, GATE_LANES)
            proj, gcol, grow = ml_proj(xf, gains, 4 * i, ml_w_in_b, j, wg, b)
            y = ml_mix(proj, gcol, grow, mlstm_norm_g[j].reshape(1, ML_V).astype(F32),
                       batch, seq)
            xf = out_proj(y, ml_w_out_b, j, gains, 4 * i + 1, xf)
        xf = mlp(xf, gains, 4 * i + 2, 4 * i + 3, w1_b, w2_b, i)
    return xf.reshape(batch, seq, d)
```

```python
import functools
import math

import jax
import jax.numpy as jnp
from jax import lax
from jax.experimental import pallas as pl
from jax.experimental.pallas import tpu as pltpu

F32 = jnp.float32
BF16 = jnp.bfloat16

D_MODEL = 2048
DEPTH = 4
CHUNK = 256
RET_CHUNK = 256
EPS = 1e-6
D_FF = 4 * D_MODEL

RET_HEADS = 8
RET_DK = 256
RET_QK = RET_HEADS * RET_DK
RET_V = 2 * D_MODEL
RET_DV = RET_V // RET_HEADS
RET_IN = 2 * RET_QK + 2 * RET_V
ROPE_BASE = 10000.0

ML_HEADS = 8
ML_QK = D_MODEL // 2
ML_DQK = ML_QK // ML_HEADS
ML_V = D_MODEL
ML_DV = ML_V // ML_HEADS
ML_MAIN = 2 * ML_QK + 2 * ML_V
GATE_SOFTCAP = 15.0
GATE_LANES = 128

VMEM_LIMIT = 60 * 1024 * 1024

NT_DIMS = (((1,), (1,)), ((), ()))
TN_DIMS = (((0,), (0,)), ((), ()))


def _rms(x, g):
    ms = jnp.mean(x * x, axis=-1, keepdims=True)
    return x * lax.rsqrt(ms + EPS) * g


NORM_ROWS = 256
NORM_SUB = 64


def _norm_rows(dst_ref, src_ref, g_ref, res_ref=None):
    g = g_ref[...]

    def body(r, carry):
        rows = pl.ds(pl.multiple_of(r * NORM_ROWS, NORM_ROWS), NORM_ROWS)
        sq = src_ref[rows, :]
        inv = lax.rsqrt(jnp.mean(sq * sq, axis=-1, keepdims=True) + EPS)
        for t in range(NORM_ROWS // NORM_SUB):
            sub = pl.ds(pl.multiple_of(r * NORM_ROWS + t * NORM_SUB, NORM_SUB), NORM_SUB)
            v = src_ref[sub, :] * inv[t * NORM_SUB:(t + 1) * NORM_SUB] * g
            if res_ref is not None:
                v = res_ref[sub, :] + v
            dst_ref[sub, :] = v.astype(dst_ref.dtype)
        return carry

    lax.fori_loop(0, src_ref.shape[0] // NORM_ROWS, body, 0)


def _params(sem, **kw):
    return pltpu.CompilerParams(dimension_semantics=sem, vmem_limit_bytes=VMEM_LIMIT, **kw)


def _rope_kernel(pos_ref, freq_ref, cos_ref, sin_ref):
    ang = pos_ref[...].astype(F32) * freq_ref[...]
    cos_ref[...] = jnp.cos(ang)
    sin_ref[...] = jnp.sin(ang)


def rope_tables(positions, inv_freq, tm=2048):
    m = positions.size
    pos = positions.reshape(m, 1)
    half = inv_freq.shape[-1]
    return pl.pallas_call(
        _rope_kernel,
        grid=(m // tm,),
        in_specs=[pl.BlockSpec((tm, 1), lambda i: (i, 0)),
                  pl.BlockSpec((1, half), lambda i: (0, 0))],
        out_specs=[pl.BlockSpec((tm, half), lambda i: (i, 0)),
                   pl.BlockSpec((tm, half), lambda i: (i, 0))],
        out_shape=[jax.ShapeDtypeStruct((m, half), F32)] * 2,
        compiler_params=_params(("parallel",)),
        name="rope_tables",
    )(pos, inv_freq.reshape(1, half))


def _ret_proj_kernel(x_ref, g_ref, w_ref, cos_ref, sin_ref, o_ref, h_ref, *, tn):
    j = pl.program_id(1)
    n_rot = 2 * RET_QK // tn

    @pl.when(j == 0)
    def _():
        _norm_rows(h_ref, x_ref, g_ref)

    @pl.when(j < n_rot)
    def _():
        res = jnp.dot(h_ref[...], w_ref[...], preferred_element_type=F32)
        scale = jnp.where(j >= n_rot // 2, RET_DK ** -0.5, 1.0).astype(F32)
        c = cos_ref[...] * scale
        s = sin_ref[...] * scale
        half = RET_DK // 2
        for hh in range(tn // RET_DK):
            lo = hh * RET_DK
            t1 = res[:, lo:lo + half]
            t2 = res[:, lo + half:lo + RET_DK]
            o_ref[:, lo:lo + half] = (t1 * c - t2 * s).astype(BF16)
            o_ref[:, lo + half:lo + RET_DK] = (t2 * c + t1 * s).astype(BF16)

    @pl.when(j >= n_rot)
    def _():
        o_ref[...] = jnp.dot(h_ref[...], w_ref[...],
                             preferred_element_type=F32).astype(BF16)


def ret_proj(x, norm_g, g_idx, w, layer, cos, sin, tm=1024, tn=2048):
    m = x.shape[0]
    return pl.pallas_call(
        functools.partial(_ret_proj_kernel, tn=tn),
        grid=(m // tm, RET_IN // tn),
        in_specs=[pl.BlockSpec((tm, D_MODEL), lambda i, j: (i, 0)),
                  pl.BlockSpec((None, 1, D_MODEL), lambda i, j: (g_idx, 0, 0)),
                  pl.BlockSpec((None, D_MODEL, tn), lambda i, j: (layer, 0, j)),
                  pl.BlockSpec((tm, RET_DK // 2), lambda i, j: (i, 0)),
                  pl.BlockSpec((tm, RET_DK // 2), lambda i, j: (i, 0))],
        out_specs=pl.BlockSpec((tm, tn), lambda i, j: (i, j)),
        out_shape=jax.ShapeDtypeStruct((m, RET_IN), BF16),
        scratch_shapes=[pltpu.VMEM((tm, D_MODEL), BF16)],
        compiler_params=_params(("parallel", "arbitrary")),
        name="ret_proj",
    )(x, norm_g, w, cos, sin)


def _ret_log_gamma(h):
    return math.log1p(-(2.0 ** (-5.0 - h)))


def _ret_kernel(q_ref, k_ref, v_ref, g_ref, y_ref, st_ref, dec_ref, xi_ref, zeta_ref):
    c = pl.program_id(1)

    @pl.when(c == 0)
    def _():
        st_ref[...] = jnp.zeros_like(st_ref)
        n = lax.broadcasted_iota(jnp.int32, (RET_CHUNK, RET_CHUNK), 0)
        mcol = lax.broadcasted_iota(jnp.int32, (RET_CHUNK, RET_CHUNK), 1)
        rel = (n - mcol).astype(F32)
        row = lax.broadcasted_iota(jnp.int32, (RET_CHUNK, RET_DK), 0).astype(F32)
        for h in range(RET_HEADS):
            lg = _ret_log_gamma(h)
            dec_ref[h] = jnp.where(rel >= 0, jnp.exp(jnp.maximum(rel, 0.0) * lg), 0.0)
            xi_ref[h] = jnp.exp((row + 1.0) * lg).astype(BF16)
            zeta_ref[h] = jnp.exp((RET_CHUNK - 1.0 - row) * lg).astype(BF16)

    heads = range(RET_HEADS)
    q = [q_ref[:, h * RET_DK:(h + 1) * RET_DK] for h in heads]
    k = [k_ref[:, h * RET_DK:(h + 1) * RET_DK] for h in heads]
    v = [v_ref[:, h * RET_DV:(h + 1) * RET_DV] for h in heads]


    raw = [lax.dot_general(q[h], k[h], NT_DIMS, preferred_element_type=F32) for h in heads]

    lhs, rhs, kz = [], [], []
    for h in heads:
        lhs.append(jnp.concatenate([(raw[h] * dec_ref[h]).astype(BF16), q[h] * xi_ref[h]],
                                   axis=1))
        rhs.append(jnp.concatenate([v[h], st_ref[h].astype(BF16)], axis=0))
        kz.append(k[h] * zeta_ref[h])

    y = []
    for h in heads:
        g_chunk = math.exp(RET_CHUNK * _ret_log_gamma(h))
        y.append(jnp.dot(lhs[h], rhs[h], preferred_element_type=F32))
        st_ref[h] = st_ref[h] * g_chunk + lax.dot_general(kz[h], v[h], TN_DIMS,
                                                          preferred_element_type=F32)

    for h in heads:
        cols = slice(h * RET_DV, (h + 1) * RET_DV)
        mu = jnp.mean(y[h], axis=-1, keepdims=True)
        yc = y[h] - mu
        var = jnp.mean(yc * yc, axis=-1, keepdims=True)
        yn = yc * lax.rsqrt(var + EPS)
        gh = g_ref[:, cols]
        y_ref[:, cols] = (gh * jax.nn.sigmoid(gh)) * yn.astype(BF16)


def ret_mix(proj, batch, seq):
    m = batch * seq
    nc = seq // RET_CHUNK
    row = lambda b, c: b * nc + c
    return pl.pallas_call(
        _ret_kernel,
        grid=(batch, nc),
        in_specs=[pl.BlockSpec((RET_CHUNK, RET_QK), lambda b, c: (row(b, c), 0)),
                  pl.BlockSpec((RET_CHUNK, RET_QK), lambda b, c: (row(b, c), 1)),
                  pl.BlockSpec((RET_CHUNK, RET_V), lambda b, c: (row(b, c), 1)),
                  pl.BlockSpec((RET_CHUNK, RET_V), lambda b, c: (row(b, c), 2))],
        out_specs=pl.BlockSpec((RET_CHUNK, RET_V), lambda b, c: (row(b, c), 0)),
        out_shape=jax.ShapeDtypeStruct((m, RET_V), BF16),
        scratch_shapes=[pltpu.VMEM((RET_HEADS, RET_DK, RET_DV), F32),
                        pltpu.VMEM((RET_HEADS, RET_CHUNK, RET_CHUNK), F32),
                        pltpu.VMEM((RET_HEADS, RET_CHUNK, RET_DK), BF16),
                        pltpu.VMEM((RET_HEADS, RET_CHUNK, RET_DK), BF16)],
        compiler_params=_params(("parallel", "arbitrary")),
        name="ret_mix",
    )(proj, proj, proj, proj)


def _log_sigmoid(x):
    return jnp.minimum(x, 0.0) - jnp.log1p(jnp.exp(-jnp.abs(x)))


def _gate_logs(pre, is_forget):
    capped = GATE_SOFTCAP * jnp.tanh(pre / GATE_SOFTCAP)
    return jnp.where(is_forget, _log_sigmoid(capped), capped)


def _ml_proj_kernel(x_ref, g_ref, w_ref, wg_ref, b_ref, o_ref, gcol_ref, grow_ref, h_ref):
    j = pl.program_id(1)

    @pl.when(j == 0)
    def _():
        _norm_rows(h_ref, x_ref, g_ref)
        pre = jnp.dot(h_ref[...], wg_ref[...], preferred_element_type=F32) + b_ref[...]
        lane = lax.broadcasted_iota(jnp.int32, pre.shape, 1)
        logs = _gate_logs(pre, lane >= ML_HEADS)
        gcol_ref[...] = logs
        grow_ref[...] = logs.T[:2 * ML_HEADS, :]

    o_ref[...] = jnp.dot(h_ref[...], w_ref[...].astype(BF16),
                         preferred_element_type=F32).astype(BF16)


def ml_proj(x, norm_g, g_idx, w, layer, wg, b, tm=1024, tn=1024):
    m = x.shape[0]
    ng = 2 * ML_HEADS
    return pl.pallas_call(
        _ml_proj_kernel,
        grid=(m // tm, ML_MAIN // tn),
        in_specs=[pl.BlockSpec((tm, D_MODEL), lambda i, j: (i, 0)),
                  pl.BlockSpec((None, 1, D_MODEL), lambda i, j: (g_idx, 0, 0)),
                  pl.BlockSpec((None, D_MODEL, tn), lambda i, j: (layer, 0, j)),
                  pl.BlockSpec((D_MODEL, GATE_LANES), lambda i, j: (0, 0)),
                  pl.BlockSpec((1, GATE_LANES), lambda i, j: (0, 0))],
        out_specs=[pl.BlockSpec((tm, tn), lambda i, j: (i, j)),
                   pl.BlockSpec((tm, GATE_LANES), lambda i, j: (i, 0)),
                   pl.BlockSpec((ng, tm), lambda i, j: (0, i))],
        out_shape=[jax.ShapeDtypeStruct((m, ML_MAIN), BF16),
                   jax.ShapeDtypeStruct((m, GATE_LANES), F32),
                   jax.ShapeDtypeStruct((ng, m), F32)],
        scratch_shapes=[pltpu.VMEM((tm, D_MODEL), BF16)],
        compiler_params=_params(("parallel", "arbitrary")),
        name="ml_proj",
    )(x, norm_g, w, wg, b)


def _ml_kernel(q_ref, k_ref, v_ref, o_ref, gcol_ref, grow_ref, ng_ref, y_ref,
               c_ref, n_ref, m_ref):
    c = pl.program_id(1)

    @pl.when(c == 0)
    def _():
        c_ref[...] = jnp.zeros_like(c_ref)
        n_ref[...] = jnp.zeros_like(n_ref)
        m_ref[...] = jnp.zeros_like(m_ref)

    s_idx = lax.broadcasted_iota(jnp.int32, (CHUNK, CHUNK), 0)
    n_idx = lax.broadcasted_iota(jnp.int32, (CHUNK, CHUNK), 1)
    causal = s_idx <= n_idx
    eye = s_idx == n_idx
    lower = (s_idx >= n_idx).astype(F32)
    upper = causal.astype(F32)

    gcol = gcol_ref[...]
    grow = grow_ref[...]
    hi = lax.Precision.HIGHEST
    cum_col = jnp.dot(lower, gcol, precision=hi, preferred_element_type=F32)
    cum_row = jnp.dot(grow, upper, precision=hi, preferred_element_type=F32)
    key_col = gcol - pltpu.roll(cum_col, GATE_LANES - ML_HEADS, axis=1)
    scale = ML_DQK ** -0.5
    log_scale = math.log(scale)

    heads = range(ML_HEADS)
    q = [q_ref[:, h * ML_DQK:(h + 1) * ML_DQK] for h in heads]
    k = [k_ref[:, h * ML_DQK:(h + 1) * ML_DQK] for h in heads]
    v = [v_ref[:, h * ML_DV:(h + 1) * ML_DV] for h in heads]


    raw, qn, qc = [], [], []
    for h in heads:
        raw.append(lax.dot_general(k[h], q[h], NT_DIMS, preferred_element_type=F32))
        qn.append(lax.dot_general(n_ref[h].astype(BF16), q[h], NT_DIMS,
                                  preferred_element_type=F32)[0:1, :])
        qc.append(jnp.dot(q[h], c_ref[h].astype(BF16),
                          preferred_element_type=F32).astype(BF16))

    lhs_t, kw_t, w_rows, carry, m_new = [], [], [], [], []
    for h in heads:
        brow = cum_row[ML_HEADS + h:ML_HEADS + h + 1, :]
        irow = grow[h:h + 1, :]
        kcol = key_col[:, h:h + 1]
        m_st = m_ref[h][0:1, 0:1]

        d_log = jnp.where(causal, brow + kcol, -jnp.inf)
        inter_log = brow + m_st
        m_out = jnp.maximum(inter_log, jnp.max(d_log, axis=0, keepdims=True))
        d_w = jnp.exp(d_log - (m_out - log_scale))
        inter_w = jnp.exp(inter_log - m_out)
        scores = raw[h] * d_w
        den = jnp.sum(scores, axis=0, keepdims=True) + inter_w * qn[h]
        r = 1.0 / jnp.maximum(jnp.abs(den), jnp.exp(-m_out))
        p_t = (scores * r).astype(BF16)
        diag = jnp.where(eye, inter_w * r, 0.0).astype(BF16)
        lhs_t.append(jnp.concatenate([p_t, diag], axis=0))

        b_tot = brow[:, CHUNK - 1:CHUNK]
        w_log = b_tot - brow + irow
        m_nw = jnp.maximum(b_tot + m_st, jnp.max(w_log, axis=1, keepdims=True))
        w_s = jnp.exp(w_log - m_nw) * scale
        carry.append(jnp.exp(b_tot + m_st - m_nw))
        m_new.append(m_nw)
        kw_t.append((k[h].astype(F32).T * w_s).astype(BF16))
        w_rows.append(jnp.broadcast_to(w_s, (8, CHUNK)).astype(BF16))

    h_out = []
    for h in heads:
        rhs = jnp.concatenate([v[h], qc[h]], axis=0)
        h_out.append(lax.dot_general(lhs_t[h], rhs, TN_DIMS, preferred_element_type=F32))
        c_ref[h] = carry[h] * c_ref[h] + jnp.dot(kw_t[h], v[h], preferred_element_type=F32)
        n_ref[h] = carry[h] * n_ref[h] + jnp.dot(w_rows[h], k[h], preferred_element_type=F32)
        m_ref[h] = jnp.broadcast_to(m_new[h], m_ref.shape[1:])

    for h in heads:
        cols = slice(h * ML_DV, (h + 1) * ML_DV)
        y = h_out[h]
        yn = y * lax.rsqrt(jnp.mean(y * y, axis=-1, keepdims=True) + EPS) * ng_ref[:, cols]
        y_ref[:, cols] = yn.astype(BF16) * jax.nn.sigmoid(o_ref[:, cols])


def ml_mix(proj, gcol, grow, norm_g, batch, seq):
    m = batch * seq
    nc = seq // CHUNK
    row = lambda b, c: b * nc + c
    return pl.pallas_call(
        _ml_kernel,
        grid=(batch, nc),
        in_specs=[pl.BlockSpec((CHUNK, ML_QK), lambda b, c: (row(b, c), 0)),
                  pl.BlockSpec((CHUNK, ML_QK), lambda b, c: (row(b, c), 1)),
                  pl.BlockSpec((CHUNK, ML_V), lambda b, c: (row(b, c), 1)),
                  pl.BlockSpec((CHUNK, ML_V), lambda b, c: (row(b, c), 2)),
                  pl.BlockSpec((CHUNK, GATE_LANES), lambda b, c: (row(b, c), 0)),
                  pl.BlockSpec((2 * ML_HEADS, CHUNK), lambda b, c: (0, row(b, c))),
                  pl.BlockSpec((1, ML_V), lambda b, c: (0, 0))],
        out_specs=pl.BlockSpec((CHUNK, ML_V), lambda b, c: (row(b, c), 0)),
        out_shape=jax.ShapeDtypeStruct((m, ML_V), BF16),
        scratch_shapes=[pltpu.VMEM((ML_HEADS, ML_DQK, ML_DV), F32),
                        pltpu.VMEM((ML_HEADS, 8, ML_DQK), F32),
                        pltpu.VMEM((ML_HEADS, 8, 128), F32)],
        compiler_params=_params(("parallel", "arbitrary")),
        name="ml_mix",
    )(proj, proj, proj, proj, gcol, grow, norm_g)


def _out_kernel(y_ref, w_ref, g_ref, x_ref, o_ref):
    k = pl.program_id(1)

    @pl.when(k == 0)
    def _():
        o_ref[...] = jnp.dot(y_ref[...], w_ref[...].astype(BF16),
                             preferred_element_type=F32)

    @pl.when(k > 0)
    def _():
        o_ref[...] += jnp.dot(y_ref[...], w_ref[...].astype(BF16),
                              preferred_element_type=F32)

    @pl.when(k == pl.num_programs(1) - 1)
    def _():
        _norm_rows(o_ref, o_ref, g_ref, res_ref=x_ref)


def out_proj(y, w, layer, norm_g, g_idx, x, tm=1024, tk=1024):
    m, kdim = y.shape
    return pl.pallas_call(
        _out_kernel,
        grid=(m // tm, kdim // tk),
        in_specs=[pl.BlockSpec((tm, tk), lambda i, k: (i, k)),
                  pl.BlockSpec((None, tk, D_MODEL), lambda i, k: (layer, k, 0)),
                  pl.BlockSpec((None, 1, D_MODEL), lambda i, k: (g_idx, 0, 0)),
                  pl.BlockSpec((tm, D_MODEL), lambda i, k: (i, 0))],
        out_specs=pl.BlockSpec((tm, D_MODEL), lambda i, k: (i, 0)),
        out_shape=jax.ShapeDtypeStruct((m, D_MODEL), F32),
        compiler_params=_params(("parallel", "arbitrary")),
        name="out_proj",
    )(y, w, norm_g, x)


def _mlp_kernel(x_ref, g_in_ref, w1_ref, w2_ref, g_out_ref, o_ref, h_ref):
    f = pl.program_id(1)

    @pl.when(f == 0)
    def _():
        _norm_rows(h_ref, x_ref, g_in_ref)
        o_ref[...] = jnp.zeros_like(o_ref)

    hid = jnp.maximum(jnp.dot(h_ref[...], w1_ref[...], preferred_element_type=F32), 0.0)
    hid = (hid * hid).astype(BF16)
    o_ref[...] += jnp.dot(hid, w2_ref[...], preferred_element_type=F32)

    @pl.when(f == pl.num_programs(1) - 1)
    def _():
        _norm_rows(o_ref, o_ref, g_out_ref, res_ref=x_ref)


def mlp(x, norm_g, g_in_idx, g_out_idx, w1, w2, layer, tm=1024, tf=1024):
    m = x.shape[0]
    return pl.pallas_call(
        _mlp_kernel,
        grid=(m // tm, D_FF // tf),
        in_specs=[pl.BlockSpec((tm, D_MODEL), lambda i, f: (i, 0)),
                  pl.BlockSpec((None, 1, D_MODEL), lambda i, f: (g_in_idx, 0, 0)),
                  pl.BlockSpec((None, D_MODEL, tf), lambda i, f: (layer, 0, f)),
                  pl.BlockSpec((None, tf, D_MODEL), lambda i, f: (layer, f, 0)),
                  pl.BlockSpec((None, 1, D_MODEL), lambda i, f: (g_out_idx, 0, 0))],
        out_specs=pl.BlockSpec((tm, D_MODEL), lambda i, f: (i, 0)),
        out_shape=jax.ShapeDtypeStruct((m, D_MODEL), F32),
        scratch_shapes=[pltpu.VMEM((tm, D_MODEL), BF16)],
        compiler_params=_params(("parallel", "arbitrary")),
        name="mlp",
    )(x, norm_g, w1, w2, norm_g)


def kernel(x, positions, norm_g, ret_w_in, ret_w_out, mlstm_w_in, mlstm_b_gate,
           mlstm_norm_g, mlstm_w_out, mlp_w1, mlp_w2):
    batch, seq, d = x.shape
    m = batch * seq
    xf = x.reshape(m, d)

    inv_freq = jnp.power(ROPE_BASE, -jnp.linspace(0.0, 1.0, RET_DK // 2, dtype=F32))
    cos, sin = rope_tables(positions, inv_freq)

    gains = norm_g.astype(F32).reshape(DEPTH * 4, 1, d)
    ret_w_in_b = ret_w_in.astype(BF16)
    w1_b = mlp_w1.astype(BF16)
    w2_b = mlp_w2.astype(BF16)
    n_gate = 2 * ML_HEADS

    for i in range(DEPTH):
        j = i // 2
        if i % 2 == 0:
            proj = ret_proj(xf, gains, 4 * i, ret_w_in_b, j, cos, sin)
            y = ret_mix(proj, batch, seq)
            xf = out_proj(y, ret_w_out, j, gains, 4 * i + 1, xf)
        else:
            w_gate = mlstm_w_in[j, :, ML_MAIN:].astype(BF16)
            wg = jnp.pad(w_gate, ((0, 0), (0, GATE_LANES - n_gate)))
            bias = mlstm_b_gate[j].astype(F32)
            b = jnp.pad(bias, (0, GATE_LANES - n_gate)).reshape(1, GATE_LANES)
            proj, gcol, grow = ml_proj(xf, gains, 4 * i, mlstm_w_in, j, wg, b)
            y = ml_mix(proj, gcol, grow, mlstm_norm_g[j].reshape(1, ML_V).astype(F32),
                       batch, seq)
            xf = out_proj(y, mlstm_w_out, j, gains, 4 * i + 1, xf)
        xf = mlp(xf, gains, 4 * i + 2, 4 * i + 3, w1_b, w2_b, i)
    return xf.reshape(batch, seq, d)
```

```python
import functools
import math

import jax
import jax.numpy as jnp
from jax import lax
from jax.experimental import pallas as pl
from jax.experimental.pallas import tpu as pltpu

F32 = jnp.float32
BF16 = jnp.bfloat16

D_MODEL = 2048
DEPTH = 4
CHUNK = 256
RET_CHUNK = 256
EPS = 1e-6
D_FF = 4 * D_MODEL

RET_HEADS = 8
RET_DK = 256
RET_QK = RET_HEADS * RET_DK
RET_V = 2 * D_MODEL
RET_DV = RET_V // RET_HEADS
RET_IN = 2 * RET_QK + 2 * RET_V
ROPE_BASE = 10000.0

ML_HEADS = 8
ML_QK = D_MODEL // 2
ML_DQK = ML_QK // ML_HEADS
ML_V = D_MODEL
ML_DV = ML_V // ML_HEADS
ML_MAIN = 2 * ML_QK + 2 * ML_V
GATE_SOFTCAP = 15.0
GATE_LANES = 128

VMEM_LIMIT = 60 * 1024 * 1024

NT_DIMS = (((1,), (1,)), ((), ()))
TN_DIMS = (((0,), (0,)), ((), ()))


def _rms(x, g):
    ms = jnp.mean(x * x, axis=-1, keepdims=True)
    return x * lax.rsqrt(ms + EPS) * g


NORM_ROWS = 256
NORM_SUB = 64


def _norm_rows(dst_ref, src_ref, g_ref, res_ref=None):
    g = g_ref[...]

    def body(r, carry):
        rows = pl.ds(pl.multiple_of(r * NORM_ROWS, NORM_ROWS), NORM_ROWS)
        sq = src_ref[rows, :]
        inv = lax.rsqrt(jnp.mean(sq * sq, axis=-1, keepdims=True) + EPS)
        for t in range(NORM_ROWS // NORM_SUB):
            sub = pl.ds(pl.multiple_of(r * NORM_ROWS + t * NORM_SUB, NORM_SUB), NORM_SUB)
            v = src_ref[sub, :] * inv[t * NORM_SUB:(t + 1) * NORM_SUB] * g
            if res_ref is not None:
                v = res_ref[sub, :] + v
            dst_ref[sub, :] = v.astype(dst_ref.dtype)
        return carry

    lax.fori_loop(0, src_ref.shape[0] // NORM_ROWS, body, 0)


def _params(sem, **kw):
    return pltpu.CompilerParams(dimension_semantics=sem, vmem_limit_bytes=VMEM_LIMIT, **kw)


def _rope_kernel(pos_ref, freq_ref, cos_ref, sin_ref):
    ang = pos_ref[...].astype(F32) * freq_ref[...]
    cos_ref[...] = jnp.cos(ang)
    sin_ref[...] = jnp.sin(ang)


def rope_tables(positions, inv_freq, tm=2048):
    m = positions.size
    pos = positions.reshape(m, 1)
    half = inv_freq.shape[-1]
    return pl.pallas_call(
        _rope_kernel,
        grid=(m // tm,),
        in_specs=[pl.BlockSpec((tm, 1), lambda i: (i, 0)),
                  pl.BlockSpec((1, half), lambda i: (0, 0))],
        out_specs=[pl.BlockSpec((tm, half), lambda i: (i, 0)),
                   pl.BlockSpec((tm, half), lambda i: (i, 0))],
        out_shape=[jax.ShapeDtypeStruct((m, half), F32)] * 2,
        compiler_params=_params(("parallel",)),
        name="rope_tables",
    )(pos, inv_freq.reshape(1, half))


def _ret_proj_kernel(x_ref, g_ref, w_ref, cos_ref, sin_ref, o_ref, h_ref, *, tn):
    j = pl.program_id(1)
    n_rot = 2 * RET_QK // tn

    @pl.when(j == 0)
    def _():
        _norm_rows(h_ref, x_ref, g_ref)

    @pl.when(j < n_rot)
    def _():
        res = jnp.dot(h_ref[...], w_ref[...], preferred_element_type=F32)
        scale = jnp.where(j >= n_rot // 2, RET_DK ** -0.5, 1.0).astype(F32)
        c = cos_ref[...] * scale
        s = sin_ref[...] * scale
        half = RET_DK // 2
        for hh in range(tn // RET_DK):
            lo = hh * RET_DK
            t1 = res[:, lo:lo + half]
            t2 = res[:, lo + half:lo + RET_DK]
            o_ref[:, lo:lo + half] = (t1 * c - t2 * s).astype(BF16)
            o_ref[:, lo + half:lo + RET_DK] = (t2 * c + t1 * s).astype(BF16)

    @pl.when(j >= n_rot)
    def _():
        o_ref[...] = jnp.dot(h_ref[...], w_ref[...],
                             preferred_element_type=F32).astype(BF16)


def ret_proj(x, norm_g, g_idx, w, layer, cos, sin, tm=1024, tn=2048):
    m = x.shape[0]
    return pl.pallas_call(
        functools.partial(_ret_proj_kernel, tn=tn),
        grid=(m // tm, RET_IN // tn),
        in_specs=[pl.BlockSpec((tm, D_MODEL), lambda i, j: (i, 0)),
                  pl.BlockSpec((None, 1, D_MODEL), lambda i, j: (g_idx, 0, 0)),
                  pl.BlockSpec((None, D_MODEL, tn), lambda i, j: (layer, 0, j)),
                  pl.BlockSpec((tm, RET_DK // 2), lambda i, j: (i, 0)),
                  pl.BlockSpec((tm, RET_DK // 2), lambda i, j: (i, 0))],
        out_specs=pl.BlockSpec((tm, tn), lambda i, j: (i, j)),
        out_shape=jax.ShapeDtypeStruct((m, RET_IN), BF16),
        scratch_shapes=[pltpu.VMEM((tm, D_MODEL), BF16)],
        compiler_params=_params(("parallel", "arbitrary")),
        name="ret_proj",
    )(x, norm_g, w, cos, sin)


def _ret_log_gamma(h):
    return math.log1p(-(2.0 ** (-5.0 - h)))


def _ret_kernel(q_ref, k_ref, v_ref, g_ref, y_ref, st_ref, dec_ref, xi_ref, zeta_ref):
    c = pl.program_id(1)

    @pl.when(c == 0)
    def _():
        st_ref[...] = jnp.zeros_like(st_ref)
        n = lax.broadcasted_iota(jnp.int32, (RET_CHUNK, RET_CHUNK), 0)
        mcol = lax.broadcasted_iota(jnp.int32, (RET_CHUNK, RET_CHUNK), 1)
        rel = (n - mcol).astype(F32)
        row = lax.broadcasted_iota(jnp.int32, (RET_CHUNK, RET_DK), 0).astype(F32)
        for h in range(RET_HEADS):
            lg = _ret_log_gamma(h)
            dec_ref[h] = jnp.where(rel >= 0, jnp.exp(jnp.maximum(rel, 0.0) * lg), 0.0)
            xi_ref[h] = jnp.exp((row + 1.0) * lg).astype(BF16)
            zeta_ref[h] = jnp.exp((RET_CHUNK - 1.0 - row) * lg).astype(BF16)

    heads = range(RET_HEADS)
    q = [q_ref[:, h * RET_DK:(h + 1) * RET_DK] for h in heads]
    k = [k_ref[:, h * RET_DK:(h + 1) * RET_DK] for h in heads]
    v = [v_ref[:, h * RET_DV:(h + 1) * RET_DV] for h in heads]


    raw = [lax.dot_general(q[h], k[h], NT_DIMS, preferred_element_type=F32) for h in heads]

    lhs, rhs, kz = [], [], []
    for h in heads:
        lhs.append(jnp.concatenate([(raw[h] * dec_ref[h]).astype(BF16), q[h] * xi_ref[h]],
                                   axis=1))
        rhs.append(jnp.concatenate([v[h], st_ref[h].astype(BF16)], axis=0))
        kz.append(k[h] * zeta_ref[h])

    y = []
    for h in heads:
        g_chunk = math.exp(RET_CHUNK * _ret_log_gamma(h))
        y.append(jnp.dot(lhs[h], rhs[h], preferred_element_type=F32))
        st_ref[h] = st_ref[h] * g_chunk + lax.dot_general(kz[h], v[h], TN_DIMS,
                                                          preferred_element_type=F32)

    for h in heads:
        cols = slice(h * RET_DV, (h + 1) * RET_DV)
        mu = jnp.mean(y[h], axis=-1, keepdims=True)
        yc = y[h] - mu
        var = jnp.mean(yc * yc, axis=-1, keepdims=True)
        yn = yc * lax.rsqrt(var + EPS)
        gh = g_ref[:, cols]
        y_ref[:, cols] = (gh * jax.nn.sigmoid(gh)) * yn.astype(BF16)


def ret_mix(proj, batch, seq):
    m = batch * seq
    nc = seq // RET_CHUNK
    row = lambda b, c: b * nc + c
    return pl.pallas_call(
        _ret_kernel,
        grid=(batch, nc),
        in_specs=[pl.BlockSpec((RET_CHUNK, RET_QK), lambda b, c: (row(b, c), 0)),
                  pl.BlockSpec((RET_CHUNK, RET_QK), lambda b, c: (row(b, c), 1)),
                  pl.BlockSpec((RET_CHUNK, RET_V), lambda b, c: (row(b, c), 1)),
                  pl.BlockSpec((RET_CHUNK, RET_V), lambda b, c: (row(b, c), 2))],
        out_specs=pl.BlockSpec((RET_CHUNK, RET_V), lambda b, c: (row(b, c), 0)),
        out_shape=jax.ShapeDtypeStruct((m, RET_V), BF16),
        scratch_shapes=[pltpu.VMEM((RET_HEADS, RET_DK, RET_DV), F32),
                        pltpu.VMEM((RET_HEADS, RET_CHUNK, RET_CHUNK), F32),
                        pltpu.VMEM((RET_HEADS, RET_CHUNK, RET_DK), BF16),
                        pltpu.VMEM((RET_HEADS, RET_CHUNK, RET_DK), BF16)],
        compiler_params=_params(("parallel", "arbitrary")),
        name="ret_mix",
    )(proj, proj, proj, proj)


def _log_sigmoid(x):
    return jnp.minimum(x, 0.0) - jnp.log1p(jnp.exp(-jnp.abs(x)))


def _gate_logs(pre, is_forget):
    capped = GATE_SOFTCAP * jnp.tanh(pre / GATE_SOFTCAP)
    return jnp.where(is_forget, _log_sigmoid(capped), capped)


def _ml_proj_kernel(x_ref, g_ref, w_ref, wg_ref, b_ref, o_ref, gcol_ref, grow_ref, h_ref):
    j = pl.program_id(1)

    @pl.when(j == 0)
    def _():
        _norm_rows(h_ref, x_ref, g_ref)
        pre = jnp.dot(h_ref[...], wg_ref[...], preferred_element_type=F32) + b_ref[...]
        lane = lax.broadcasted_iota(jnp.int32, pre.shape, 1)
        logs = _gate_logs(pre, lane >= ML_HEADS)
        gcol_ref[...] = logs
        grow_ref[...] = logs.T[:2 * ML_HEADS, :]

    o_ref[...] = jnp.dot(h_ref[...], w_ref[...], preferred_element_type=F32).astype(BF16)


def ml_proj(x, norm_g, g_idx, w, layer, wg, b, tm=1024, tn=2048):
    m = x.shape[0]
    ng = 2 * ML_HEADS
    return pl.pallas_call(
        _ml_proj_kernel,
        grid=(m // tm, ML_MAIN // tn),
        in_specs=[pl.BlockSpec((tm, D_MODEL), lambda i, j: (i, 0)),
                  pl.BlockSpec((None, 1, D_MODEL), lambda i, j: (g_idx, 0, 0)),
                  pl.BlockSpec((None, D_MODEL, tn), lambda i, j: (layer, 0, j)),
                  pl.BlockSpec((D_MODEL, GATE_LANES), lambda i, j: (0, 0)),
                  pl.BlockSpec((1, GATE_LANES), lambda i, j: (0, 0))],
        out_specs=[pl.BlockSpec((tm, tn), lambda i, j: (i, j)),
                   pl.BlockSpec((tm, GATE_LANES), lambda i, j: (i, 0)),
                   pl.BlockSpec((ng, tm), lambda i, j: (0, i))],
        out_shape=[jax.ShapeDtypeStruct((m, ML_MAIN), BF16),
                   jax.ShapeDtypeStruct((m, GATE_LANES), F32),
                   jax.ShapeDtypeStruct((ng, m), F32)],
        scratch_shapes=[pltpu.VMEM((tm, D_MODEL), BF16)],
        compiler_params=_params(("parallel", "arbitrary")),
        name="ml_proj",
    )(x, norm_g, w, wg, b)


def _ml_kernel(q_ref, k_ref, v_ref, o_ref, gcol_ref, grow_ref, ng_ref, y_ref,
               c_ref, n_ref, m_ref):
    c = pl.program_id(1)

    @pl.when(c == 0)
    def _():
        c_ref[...] = jnp.zeros_like(c_ref)
        n_ref[...] = jnp.zeros_like(n_ref)
        m_ref[...] = jnp.zeros_like(m_ref)

    s_idx = lax.broadcasted_iota(jnp.int32, (CHUNK, CHUNK), 0)
    n_idx = lax.broadcasted_iota(jnp.int32, (CHUNK, CHUNK), 1)
    causal = s_idx <= n_idx
    eye = s_idx == n_idx
    lower = (s_idx >= n_idx).astype(F32)
    upper = causal.astype(F32)

    gcol = gcol_ref[...]
    grow = grow_ref[...]
    hi = lax.Precision.HIGHEST
    cum_col = jnp.dot(lower, gcol, precision=hi, preferred_element_type=F32)
    cum_row = jnp.dot(grow, upper, precision=hi, preferred_element_type=F32)
    key_col = gcol - pltpu.roll(cum_col, GATE_LANES - ML_HEADS, axis=1)
    scale = ML_DQK ** -0.5
    log_scale = math.log(scale)

    heads = range(ML_HEADS)
    q = [q_ref[:, h * ML_DQK:(h + 1) * ML_DQK] for h in heads]
    k = [k_ref[:, h * ML_DQK:(h + 1) * ML_DQK] for h in heads]
    v = [v_ref[:, h * ML_DV:(h + 1) * ML_DV] for h in heads]


    raw, qn, qc = [], [], []
    for h in heads:
        raw.append(lax.dot_general(k[h], q[h], NT_DIMS, preferred_element_type=F32))
        qn.append(lax.dot_general(n_ref[h].astype(BF16), q[h], NT_DIMS,
                                  preferred_element_type=F32)[0:1, :])
        qc.append(jnp.dot(q[h], c_ref[h].astype(BF16),
                          preferred_element_type=F32).astype(BF16))

    lhs_t, kw_t, w_rows, carry, m_new = [], [], [], [], []
    for h in heads:
        brow = cum_row[ML_HEADS + h:ML_HEADS + h + 1, :]
        irow = grow[h:h + 1, :]
        kcol = key_col[:, h:h + 1]
        m_st = m_ref[h][0:1, 0:1]

        d_log = jnp.where(causal, brow + kcol, -jnp.inf)
        inter_log = brow + m_st
        m_out = jnp.maximum(inter_log, jnp.max(d_log, axis=0, keepdims=True))
        d_w = jnp.exp(d_log - (m_out - log_scale))
        inter_w = jnp.exp(inter_log - m_out)
        scores = raw[h] * d_w
        den = jnp.sum(scores, axis=0, keepdims=True) + inter_w * qn[h]
        r = 1.0 / jnp.maximum(jnp.abs(den), jnp.exp(-m_out))
        p_t = (scores * r).astype(BF16)
        diag = jnp.where(eye, inter_w * r, 0.0).astype(BF16)
        lhs_t.append(jnp.concatenate([p_t, diag], axis=0))

        b_tot = brow[:, CHUNK - 1:CHUNK]
        w_log = b_tot - brow + irow
        m_nw = jnp.maximum(b_tot + m_st, jnp.max(w_log, axis=1, keepdims=True))
        w_s = jnp.exp(w_log - m_nw) * scale
        carry.append(jnp.exp(b_tot + m_st - m_nw))
        m_new.append(m_nw)
        kw_t.append((k[h].astype(F32).T * w_s).astype(BF16))
        w_rows.append(jnp.broadcast_to(w_s, (8, CHUNK)).astype(BF16))

    h_out = []
    for h in heads:
        rhs = jnp.concatenate([v[h], qc[h]], axis=0)
        h_out.append(lax.dot_general(lhs_t[h], rhs, TN_DIMS, preferred_element_type=F32))
        c_ref[h] = carry[h] * c_ref[h] + jnp.dot(kw_t[h], v[h], preferred_element_type=F32)
        n_ref[h] = carry[h] * n_ref[h] + jnp.dot(w_rows[h], k[h], preferred_element_type=F32)
        m_ref[h] = jnp.broadcast_to(m_new[h], m_ref.shape[1:])

    for h in heads:
        cols = slice(h * ML_DV, (h + 1) * ML_DV)
        y = h_out[h]
        yn = y * lax.rsqrt(jnp.mean(y * y, axis=-1, keepdims=True) + EPS) * ng_ref[:, cols]
        y_ref[:, cols] = yn.astype(BF16) * jax.nn.sigmoid(o_ref[:, cols])


def ml_mix(proj, gcol, grow, norm_g, batch, seq):
    m = batch * seq
    nc = seq // CHUNK
    row = lambda b, c: b * nc + c
    return pl.pallas_call(
        _ml_kernel,
        grid=(batch, nc),
        in_specs=[pl.BlockSpec((CHUNK, ML_QK), lambda b, c: (row(b, c), 0)),
                  pl.BlockSpec((CHUNK, ML_QK), lambda b, c: (row(b, c), 1)),
                  pl.BlockSpec((CHUNK, ML_V), lambda b, c: (row(b, c), 1)),
                  pl.BlockSpec((CHUNK, ML_V), lambda b, c: (row(b, c), 2)),
                  pl.BlockSpec((CHUNK, GATE_LANES), lambda b, c: (row(b, c), 0)),
                  pl.BlockSpec((2 * ML_HEADS, CHUNK), lambda b, c: (0, row(b, c))),
                  pl.BlockSpec((1, ML_V), lambda b, c: (0, 0))],
        out_specs=pl.BlockSpec((CHUNK, ML_V), lambda b, c: (row(b, c), 0)),
        out_shape=jax.ShapeDtypeStruct((m, ML_V), BF16),
        scratch_shapes=[pltpu.VMEM((ML_HEADS, ML_DQK, ML_DV), F32),
                        pltpu.VMEM((ML_HEADS, 8, ML_DQK), F32),
                        pltpu.VMEM((ML_HEADS, 8, 128), F32)],
        compiler_params=_params(("parallel", "arbitrary")),
        name="ml_mix",
    )(proj, proj, proj, proj, gcol, grow, norm_g)


def _out_kernel(y_ref, w_ref, g_ref, x_ref, o_ref):
    k = pl.program_id(1)

    @pl.when(k == 0)
    def _():
        o_ref[...] = jnp.dot(y_ref[...], w_ref[...], preferred_element_type=F32)

    @pl.when(k > 0)
    def _():
        o_ref[...] += jnp.dot(y_ref[...], w_ref[...], preferred_element_type=F32)

    @pl.when(k == pl.num_programs(1) - 1)
    def _():
        _norm_rows(o_ref, o_ref, g_ref, res_ref=x_ref)


def out_proj(y, w, layer, norm_g, g_idx, x, tm=1024, tk=1024):
    m, kdim = y.shape
    return pl.pallas_call(
        _out_kernel,
        grid=(m // tm, kdim // tk),
        in_specs=[pl.BlockSpec((tm, tk), lambda i, k: (i, k)),
                  pl.BlockSpec((None, tk, D_MODEL), lambda i, k: (layer, k, 0)),
                  pl.BlockSpec((None, 1, D_MODEL), lambda i, k: (g_idx, 0, 0)),
                  pl.BlockSpec((tm, D_MODEL), lambda i, k: (i, 0))],
        out_specs=pl.BlockSpec((tm, D_MODEL), lambda i, k: (i, 0)),
        out_shape=jax.ShapeDtypeStruct((m, D_MODEL), F32),
        compiler_params=_params(("parallel", "arbitrary")),
        name="out_proj",
    )(y, w, norm_g, x)


def _mlp_kernel(x_ref, g_in_ref, w1_ref, w2_ref, g_out_ref, o_ref, h_ref):
    f = pl.program_id(1)

    def chunk_out():
        hid = jnp.maximum(jnp.dot(h_ref[...], w1_ref[...], preferred_element_type=F32), 0.0)
        return jnp.dot((hid * hid).astype(BF16), w2_ref[...], preferred_element_type=F32)

    @pl.when(f == 0)
    def _():
        _norm_rows(h_ref, x_ref, g_in_ref)
        o_ref[...] = chunk_out()

    @pl.when(f > 0)
    def _():
        o_ref[...] += chunk_out()

    @pl.when(f == pl.num_programs(1) - 1)
    def _():
        _norm_rows(o_ref, o_ref, g_out_ref, res_ref=x_ref)


def mlp(x, norm_g, g_in_idx, g_out_idx, w1, w2, layer, tm=1024, tf=1024):
    m = x.shape[0]
    return pl.pallas_call(
        _mlp_kernel,
        grid=(m // tm, D_FF // tf),
        in_specs=[pl.BlockSpec((tm, D_MODEL), lambda i, f: (i, 0)),
                  pl.BlockSpec((None, 1, D_MODEL), lambda i, f: (g_in_idx, 0, 0)),
                  pl.BlockSpec((None, D_MODEL, tf), lambda i, f: (layer, 0, f)),
                  pl.BlockSpec((None, tf, D_MODEL), lambda i, f: (layer, f, 0)),
                  pl.BlockSpec((None, 1, D_MODEL), lambda i, f: (g_out_idx, 0, 0))],
        out_specs=pl.BlockSpec((tm, D_MODEL), lambda i, f: (i, 0)),
        out_shape=jax.ShapeDtypeStruct((m, D_MODEL), F32),
        scratch_shapes=[pltpu.VMEM((tm, D_MODEL), BF16)],
        compiler_params=_params(("parallel", "arbitrary")),
        name="mlp",
    )(x, norm_g, w1, w2, norm_g)


def kernel(x, positions, norm_g, ret_w_in, ret_w_out, mlstm_w_in, mlstm_b_gate,
           mlstm_norm_g, mlstm_w_out, mlp_w1, mlp_w2):
    batch, seq, d = x.shape
    m = batch * seq
    xf = x.reshape(m, d)

    inv_freq = jnp.power(ROPE_BASE, -jnp.linspace(0.0, 1.0, RET_DK // 2, dtype=F32))
    cos, sin = rope_tables(positions, inv_freq)

    gains = norm_g.astype(F32).reshape(DEPTH * 4, 1, d)
    ret_w_in_b = ret_w_in.astype(BF16)
    ret_w_out_b = ret_w_out.astype(BF16)
    ml_w_in_b = mlstm_w_in.astype(BF16)
    ml_w_out_b = mlstm_w_out.astype(BF16)
    w1_b = mlp_w1.astype(BF16)
    w2_b = mlp_w2.astype(BF16)
    n_gate = 2 * ML_HEADS

    for i in range(DEPTH):
        j = i // 2
        if i % 2 == 0:
            proj = ret_proj(xf, gains, 4 * i, ret_w_in_b, j, cos, sin)
            y = ret_mix(proj, batch, seq)
            xf = out_proj(y, ret_w_out_b, j, gains, 4 * i + 1, xf)
        else:
            w_gate = ml_w_in_b[j, :, ML_MAIN:]
            wg = jnp.pad(w_gate, ((0, 0), (0, GATE_LANES - n_gate)))
            bias = mlstm_b_gate[j].astype(F32)
            b = jnp.pad(bias, (0, GATE_LANES - n_gate)).reshape(1, GATE_LANES)
            proj, gcol, grow = ml_proj(xf, gains, 4 * i, ml_w_in_b, j, wg, b)
            y = ml_mix(proj, gcol, grow, mlstm_norm_g[j].reshape(1, ML_V).astype(F32),
                       batch, seq)
            xf = out_proj(y, ml_w_out_b, j, gains, 4 * i + 1, xf)
        xf = mlp(xf, gains, 4 * i + 2, 4 * i + 3, w1_b, w2_b, i)
    return xf.reshape(batch, seq, d)
```

```python
import functools
import math

import jax
import jax.numpy as jnp
from jax import lax
from jax.experimental import pallas as pl
from jax.experimental.pallas import tpu as pltpu

F32 = jnp.float32
BF16 = jnp.bfloat16

D_MODEL = 2048
DEPTH = 4
CHUNK = 256
RET_CHUNK = 256
EPS = 1e-6
D_FF = 4 * D_MODEL

RET_HEADS = 8
RET_DK = 256
RET_QK = RET_HEADS * RET_DK
RET_V = 2 * D_MODEL
RET_DV = RET_V // RET_HEADS
RET_IN = 2 * RET_QK + 2 * RET_V
ROPE_BASE = 10000.0

ML_HEADS = 8
ML_QK = D_MODEL // 2
ML_DQK = ML_QK // ML_HEADS
ML_V = D_MODEL
ML_DV = ML_V // ML_HEADS
ML_MAIN = 2 * ML_QK + 2 * ML_V
GATE_SOFTCAP = 15.0
GATE_LANES = 128

VMEM_LIMIT = 60 * 1024 * 1024

NT_DIMS = (((1,), (1,)), ((), ()))
TN_DIMS = (((0,), (0,)), ((), ()))


def _rms(x, g):
    ms = jnp.mean(x * x, axis=-1, keepdims=True)
    return x * lax.rsqrt(ms + EPS) * g


NORM_ROWS = 256
NORM_SUB = 64


def _norm_rows(dst_ref, src_ref, g_ref, res_ref=None):
    g = g_ref[...]

    def body(r, carry):
        rows = pl.ds(pl.multiple_of(r * NORM_ROWS, NORM_ROWS), NORM_ROWS)
        sq = src_ref[rows, :]
        inv = lax.rsqrt(jnp.mean(sq * sq, axis=-1, keepdims=True) + EPS)
        for t in range(NORM_ROWS // NORM_SUB):
            sub = pl.ds(pl.multiple_of(r * NORM_ROWS + t * NORM_SUB, NORM_SUB), NORM_SUB)
            v = src_ref[sub, :] * inv[t * NORM_SUB:(t + 1) * NORM_SUB] * g
            if res_ref is not None:
                v = res_ref[sub, :] + v
            dst_ref[sub, :] = v.astype(dst_ref.dtype)
        return carry

    lax.fori_loop(0, src_ref.shape[0] // NORM_ROWS, body, 0)


def _params(sem, **kw):
    return pltpu.CompilerParams(dimension_semantics=sem, vmem_limit_bytes=VMEM_LIMIT, **kw)


def _rope_kernel(pos_ref, freq_ref, cos_ref, sin_ref):
    ang = pos_ref[...].astype(F32) * freq_ref[...]
    cos_ref[...] = jnp.cos(ang)
    sin_ref[...] = jnp.sin(ang)


def rope_tables(positions, inv_freq, tm=2048):
    m = positions.size
    pos = positions.reshape(m, 1)
    half = inv_freq.shape[-1]
    return pl.pallas_call(
        _rope_kernel,
        grid=(m // tm,),
        in_specs=[pl.BlockSpec((tm, 1), lambda i: (i, 0)),
                  pl.BlockSpec((1, half), lambda i: (0, 0))],
        out_specs=[pl.BlockSpec((tm, half), lambda i: (i, 0)),
                   pl.BlockSpec((tm, half), lambda i: (i, 0))],
        out_shape=[jax.ShapeDtypeStruct((m, half), F32)] * 2,
        compiler_params=_params(("parallel",)),
        name="rope_tables",
    )(pos, inv_freq.reshape(1, half))


def _ret_proj_kernel(x_ref, g_ref, w_ref, cos_ref, sin_ref, o_ref, h_ref, *, tn):
    j = pl.program_id(1)
    n_rot = 2 * RET_QK // tn

    @pl.when(j == 0)
    def _():
        _norm_rows(h_ref, x_ref, g_ref)

    @pl.when(j < n_rot)
    def _():
        res = jnp.dot(h_ref[...], w_ref[...], preferred_element_type=F32)
        scale = jnp.where(j >= n_rot // 2, RET_DK ** -0.5, 1.0).astype(F32)
        c = cos_ref[...] * scale
        s = sin_ref[...] * scale
        half = RET_DK // 2
        for hh in range(tn // RET_DK):
            lo = hh * RET_DK
            t1 = res[:, lo:lo + half]
            t2 = res[:, lo + half:lo + RET_DK]
            o_ref[:, lo:lo + half] = (t1 * c - t2 * s).astype(BF16)
            o_ref[:, lo + half:lo + RET_DK] = (t2 * c + t1 * s).astype(BF16)

    @pl.when(j >= n_rot)
    def _():
        o_ref[...] = jnp.dot(h_ref[...], w_ref[...],
                             preferred_element_type=F32).astype(BF16)


def ret_proj(x, norm_g, g_idx, w, layer, cos, sin, tm=1024, tn=2048):
    m = x.shape[0]
    return pl.pallas_call(
        functools.partial(_ret_proj_kernel, tn=tn),
        grid=(m // tm, RET_IN // tn),
        in_specs=[pl.BlockSpec((tm, D_MODEL), lambda i, j: (i, 0)),
                  pl.BlockSpec((None, 1, D_MODEL), lambda i, j: (g_idx, 0, 0)),
                  pl.BlockSpec((None, D_MODEL, tn), lambda i, j: (layer, 0, j)),
                  pl.BlockSpec((tm, RET_DK // 2), lambda i, j: (i, 0)),
                  pl.BlockSpec((tm, RET_DK // 2), lambda i, j: (i, 0))],
        out_specs=pl.BlockSpec((tm, tn), lambda i, j: (i, j)),
        out_shape=jax.ShapeDtypeStruct((m, RET_IN), BF16),
        scratch_shapes=[pltpu.VMEM((tm, D_MODEL), BF16)],
        compiler_params=_params(("parallel", "arbitrary")),
        name="ret_proj",
    )(x, norm_g, w, cos, sin)


def _ret_log_gamma(h):
    return math.log1p(-(2.0 ** (-5.0 - h)))


OUT_PIECE = 256


def _ret_kernel(q_ref, k_ref, v_ref, g_ref, x_ref, w_ref, gain_ref, o_ref,
                st_ref, dec_ref, xi_ref, zeta_ref, y_ref, *, nc):
    t = pl.program_id(0)
    slot = t % 2

    @pl.when(t == 0)
    def _():
        y_ref[...] = jnp.zeros_like(y_ref)

    @pl.when(t % nc == 0)
    def _():
        st_ref[...] = jnp.zeros_like(st_ref)
        n = lax.broadcasted_iota(jnp.int32, (RET_CHUNK, RET_CHUNK), 0)
        mcol = lax.broadcasted_iota(jnp.int32, (RET_CHUNK, RET_CHUNK), 1)
        rel = (n - mcol).astype(F32)
        row = lax.broadcasted_iota(jnp.int32, (RET_CHUNK, RET_DK), 0).astype(F32)
        for h in range(RET_HEADS):
            lg = _ret_log_gamma(h)
            dec_ref[h] = jnp.where(rel >= 0, jnp.exp(jnp.maximum(rel, 0.0) * lg), 0.0)
            xi_ref[h] = jnp.exp((row + 1.0) * lg).astype(BF16)
            zeta_ref[h] = jnp.exp((RET_CHUNK - 1.0 - row) * lg).astype(BF16)

    heads = range(RET_HEADS)
    q = [q_ref[:, h * RET_DK:(h + 1) * RET_DK] for h in heads]
    k = [k_ref[:, h * RET_DK:(h + 1) * RET_DK] for h in heads]
    v = [v_ref[:, h * RET_DV:(h + 1) * RET_DV] for h in heads]


    y_prev = y_ref[1 - slot]
    pieces = []

    def out_piece():
        p = len(pieces)
        pieces.append(jnp.dot(y_prev, w_ref[:, p * OUT_PIECE:(p + 1) * OUT_PIECE],
                              preferred_element_type=F32))

    raw = [lax.dot_general(q[h], k[h], NT_DIMS, preferred_element_type=F32) for h in heads]

    lhs, rhs, kz = [], [], []
    for h in heads:
        if h % 2 == 0:
            out_piece()
        lhs.append(jnp.concatenate([(raw[h] * dec_ref[h]).astype(BF16), q[h] * xi_ref[h]],
                                   axis=1))
        rhs.append(jnp.concatenate([v[h], st_ref[h].astype(BF16)], axis=0))
        kz.append(k[h] * zeta_ref[h])

    y = []
    for h in heads:
        g_chunk = math.exp(RET_CHUNK * _ret_log_gamma(h))
        y.append(jnp.dot(lhs[h], rhs[h], preferred_element_type=F32))
        st_ref[h] = st_ref[h] * g_chunk + lax.dot_general(kz[h], v[h], TN_DIMS,
                                                          preferred_element_type=F32)

    for h in heads:
        if h % 2 == 0:
            out_piece()
        cols = slice(h * RET_DV, (h + 1) * RET_DV)
        mu = jnp.mean(y[h], axis=-1, keepdims=True)
        yc = y[h] - mu
        var = jnp.mean(yc * yc, axis=-1, keepdims=True)
        yn = yc * lax.rsqrt(var + EPS)
        gh = g_ref[:, cols]
        y_ref[slot, :, cols] = (gh * jax.nn.sigmoid(gh)) * yn.astype(BF16)

    out = jnp.concatenate(pieces, axis=1)
    o_ref[...] = x_ref[...] + _rms(out, gain_ref[...])


def ret_mix_out(proj, x, w_out, layer, norm_g, g_idx, batch, seq):
    assert D_MODEL == RET_HEADS * OUT_PIECE
    m = batch * seq
    n_chunks = m // RET_CHUNK
    cur = lambda t: jnp.minimum(t, n_chunks - 1)
    prev = lambda t: jnp.maximum(t - 1, 0)
    return pl.pallas_call(
        functools.partial(_ret_kernel, nc=seq // RET_CHUNK),
        grid=(n_chunks + 1,),
        in_specs=[pl.BlockSpec((RET_CHUNK, RET_QK), lambda t: (cur(t), 0)),
                  pl.BlockSpec((RET_CHUNK, RET_QK), lambda t: (cur(t), 1)),
                  pl.BlockSpec((RET_CHUNK, RET_V), lambda t: (cur(t), 1)),
                  pl.BlockSpec((RET_CHUNK, RET_V), lambda t: (cur(t), 2)),
                  pl.BlockSpec((RET_CHUNK, D_MODEL), lambda t: (prev(t), 0)),
                  pl.BlockSpec((None, RET_V, D_MODEL), lambda t: (layer, 0, 0),
                               pipeline_mode=pl.Buffered(1)),
                  pl.BlockSpec((None, 1, D_MODEL), lambda t: (g_idx, 0, 0))],
        out_specs=pl.BlockSpec((RET_CHUNK, D_MODEL), lambda t: (prev(t), 0)),
        out_shape=jax.ShapeDtypeStruct((m, D_MODEL), F32),
        scratch_shapes=[pltpu.VMEM((RET_HEADS, RET_DK, RET_DV), F32),
                        pltpu.VMEM((RET_HEADS, RET_CHUNK, RET_CHUNK), F32),
                        pltpu.VMEM((RET_HEADS, RET_CHUNK, RET_DK), BF16),
                        pltpu.VMEM((RET_HEADS, RET_CHUNK, RET_DK), BF16),
                        pltpu.VMEM((2, RET_CHUNK, RET_V), BF16)],
        compiler_params=_params(("arbitrary",)),
        name="ret_mix_out",
    )(proj, proj, proj, proj, x, w_out, norm_g)


def _log_sigmoid(x):
    return jnp.minimum(x, 0.0) - jnp.log1p(jnp.exp(-jnp.abs(x)))


def _gate_logs(pre, is_forget):
    capped = GATE_SOFTCAP * jnp.tanh(pre / GATE_SOFTCAP)
    return jnp.where(is_forget, _log_sigmoid(capped), capped)


def _ml_proj_kernel(x_ref, g_ref, w_ref, wg_ref, b_ref, o_ref, gcol_ref, grow_ref, h_ref):
    j = pl.program_id(1)

    @pl.when(j == 0)
    def _():
        _norm_rows(h_ref, x_ref, g_ref)
        pre = jnp.dot(h_ref[...], wg_ref[...], preferred_element_type=F32) + b_ref[...]
        lane = lax.broadcasted_iota(jnp.int32, pre.shape, 1)
        logs = _gate_logs(pre, lane >= ML_HEADS)
        gcol_ref[...] = logs
        grow_ref[...] = logs.T[:2 * ML_HEADS, :]

    o_ref[...] = jnp.dot(h_ref[...], w_ref[...], preferred_element_type=F32).astype(BF16)


def ml_proj(x, norm_g, g_idx, w, layer, wg, b, tm=1024, tn=2048):
    m = x.shape[0]
    ng = 2 * ML_HEADS
    return pl.pallas_call(
        _ml_proj_kernel,
        grid=(m // tm, ML_MAIN // tn),
        in_specs=[pl.BlockSpec((tm, D_MODEL), lambda i, j: (i, 0)),
                  pl.BlockSpec((None, 1, D_MODEL), lambda i, j: (g_idx, 0, 0)),
                  pl.BlockSpec((None, D_MODEL, tn), lambda i, j: (layer, 0, j)),
                  pl.BlockSpec((D_MODEL, GATE_LANES), lambda i, j: (0, 0)),
                  pl.BlockSpec((1, GATE_LANES), lambda i, j: (0, 0))],
        out_specs=[pl.BlockSpec((tm, tn), lambda i, j: (i, j)),
                   pl.BlockSpec((tm, GATE_LANES), lambda i, j: (i, 0)),
                   pl.BlockSpec((ng, tm), lambda i, j: (0, i))],
        out_shape=[jax.ShapeDtypeStruct((m, ML_MAIN), BF16),
                   jax.ShapeDtypeStruct((m, GATE_LANES), F32),
                   jax.ShapeDtypeStruct((ng, m), F32)],
        scratch_shapes=[pltpu.VMEM((tm, D_MODEL), BF16)],
        compiler_params=_params(("parallel", "arbitrary")),
        name="ml_proj",
    )(x, norm_g, w, wg, b)


def _ml_kernel(q_ref, k_ref, v_ref, o_ref, gcol_ref, grow_ref, ng_ref, y_ref,
               c_ref, n_ref, m_ref):
    c = pl.program_id(1)

    @pl.when(c == 0)
    def _():
        c_ref[...] = jnp.zeros_like(c_ref)
        n_ref[...] = jnp.zeros_like(n_ref)
        m_ref[...] = jnp.zeros_like(m_ref)

    s_idx = lax.broadcasted_iota(jnp.int32, (CHUNK, CHUNK), 0)
    n_idx = lax.broadcasted_iota(jnp.int32, (CHUNK, CHUNK), 1)
    causal = s_idx <= n_idx
    eye = s_idx == n_idx
    lower = (s_idx >= n_idx).astype(F32)
    upper = causal.astype(F32)

    gcol = gcol_ref[...]
    grow = grow_ref[...]
    hi = lax.Precision.HIGHEST
    cum_col = jnp.dot(lower, gcol, precision=hi, preferred_element_type=F32)
    cum_row = jnp.dot(grow, upper, precision=hi, preferred_element_type=F32)
    key_col = gcol - pltpu.roll(cum_col, GATE_LANES - ML_HEADS, axis=1)
    scale = ML_DQK ** -0.5
    log_scale = math.log(scale)

    heads = range(ML_HEADS)
    q = [q_ref[:, h * ML_DQK:(h + 1) * ML_DQK] for h in heads]
    k = [k_ref[:, h * ML_DQK:(h + 1) * ML_DQK] for h in heads]
    v = [v_ref[:, h * ML_DV:(h + 1) * ML_DV] for h in heads]


    raw, qn, qc = [], [], []
    for h in heads:
        raw.append(lax.dot_general(k[h], q[h], NT_DIMS, preferred_element_type=F32))
        qn.append(lax.dot_general(n_ref[h].astype(BF16), q[h], NT_DIMS,
                                  preferred_element_type=F32)[0:1, :])
        qc.append(jnp.dot(q[h], c_ref[h].astype(BF16),
                          preferred_element_type=F32).astype(BF16))

    lhs_t, kw_t, w_rows, carry, m_new = [], [], [], [], []
    for h in heads:
        brow = cum_row[ML_HEADS + h:ML_HEADS + h + 1, :]
        irow = grow[h:h + 1, :]
        kcol = key_col[:, h:h + 1]
        m_st = m_ref[h][0:1, 0:1]

        d_log = jnp.where(causal, brow + kcol, -jnp.inf)
        inter_log = brow + m_st
        m_out = jnp.maximum(inter_log, jnp.max(d_log, axis=0, keepdims=True))
        d_w = jnp.exp(d_log - (m_out - log_scale))
        inter_w = jnp.exp(inter_log - m_out)
        scores = raw[h] * d_w
        den = jnp.sum(scores, axis=0, keepdims=True) + inter_w * qn[h]
        r = 1.0 / jnp.maximum(jnp.abs(den), jnp.exp(-m_out))
        p_t = (scores * r).astype(BF16)
        diag = jnp.where(eye, inter_w * r, 0.0).astype(BF16)
        lhs_t.append(jnp.concatenate([p_t, diag], axis=0))

        b_tot = brow[:, CHUNK - 1:CHUNK]
        w_log = b_tot - brow + irow
        m_nw = jnp.maximum(b_tot + m_st, jnp.max(w_log, axis=1, keepdims=True))
        w_s = jnp.exp(w_log - m_nw) * scale
        carry.append(jnp.exp(b_tot + m_st - m_nw))
        m_new.append(m_nw)
        kw_t.append((k[h].astype(F32).T * w_s).astype(BF16))
        w_rows.append(jnp.broadcast_to(w_s, (8, CHUNK)).astype(BF16))

    h_out = []
    for h in heads:
        rhs = jnp.concatenate([v[h], qc[h]], axis=0)
        h_out.append(lax.dot_general(lhs_t[h], rhs, TN_DIMS, preferred_element_type=F32))
        c_ref[h] = carry[h] * c_ref[h] + jnp.dot(kw_t[h], v[h], preferred_element_type=F32)
        n_ref[h] = carry[h] * n_ref[h] + jnp.dot(w_rows[h], k[h], preferred_element_type=F32)
        m_ref[h] = jnp.broadcast_to(m_new[h], m_ref.shape[1:])

    for h in heads:
        cols = slice(h * ML_DV, (h + 1) * ML_DV)
        y = h_out[h]
        yn = y * lax.rsqrt(jnp.mean(y * y, axis=-1, keepdims=True) + EPS) * ng_ref[:, cols]
        y_ref[:, cols] = yn.astype(BF16) * jax.nn.sigmoid(o_ref[:, cols])


def ml_mix(proj, gcol, grow, norm_g, batch, seq):
    m = batch * seq
    nc = seq // CHUNK
    row = lambda b, c: b * nc + c
    return pl.pallas_call(
        _ml_kernel,
        grid=(batch, nc),
        in_specs=[pl.BlockSpec((CHUNK, ML_QK), lambda b, c: (row(b, c), 0)),
                  pl.BlockSpec((CHUNK, ML_QK), lambda b, c: (row(b, c), 1)),
                  pl.BlockSpec((CHUNK, ML_V), lambda b, c: (row(b, c), 1)),
                  pl.BlockSpec((CHUNK, ML_V), lambda b, c: (row(b, c), 2)),
                  pl.BlockSpec((CHUNK, GATE_LANES), lambda b, c: (row(b, c), 0)),
                  pl.BlockSpec((2 * ML_HEADS, CHUNK), lambda b, c: (0, row(b, c))),
                  pl.BlockSpec((1, ML_V), lambda b, c: (0, 0))],
        out_specs=pl.BlockSpec((CHUNK, ML_V), lambda b, c: (row(b, c), 0)),
        out_shape=jax.ShapeDtypeStruct((m, ML_V), BF16),
        scratch_shapes=[pltpu.VMEM((ML_HEADS, ML_DQK, ML_DV), F32),
                        pltpu.VMEM((ML_HEADS, 8, ML_DQK), F32),
                        pltpu.VMEM((ML_HEADS, 8, 128), F32)],
        compiler_params=_params(("parallel", "arbitrary")),
        name="ml_mix",
    )(proj, proj, proj, proj, gcol, grow, norm_g)


def _out_kernel(y_ref, w_ref, g_ref, x_ref, o_ref):
    k = pl.program_id(1)

    @pl.when(k == 0)
    def _():
        o_ref[...] = jnp.dot(y_ref[...], w_ref[...], preferred_element_type=F32)

    @pl.when(k > 0)
    def _():
        o_ref[...] += jnp.dot(y_ref[...], w_ref[...], preferred_element_type=F32)

    @pl.when(k == pl.num_programs(1) - 1)
    def _():
        _norm_rows(o_ref, o_ref, g_ref, res_ref=x_ref)


def out_proj(y, w, layer, norm_g, g_idx, x, tm=1024, tk=1024):
    m, kdim = y.shape
    return pl.pallas_call(
        _out_kernel,
        grid=(m // tm, kdim // tk),
        in_specs=[pl.BlockSpec((tm, tk), lambda i, k: (i, k)),
                  pl.BlockSpec((None, tk, D_MODEL), lambda i, k: (layer, k, 0)),
                  pl.BlockSpec((None, 1, D_MODEL), lambda i, k: (g_idx, 0, 0)),
                  pl.BlockSpec((tm, D_MODEL), lambda i, k: (i, 0))],
        out_specs=pl.BlockSpec((tm, D_MODEL), lambda i, k: (i, 0)),
        out_shape=jax.ShapeDtypeStruct((m, D_MODEL), F32),
        compiler_params=_params(("parallel", "arbitrary")),
        name="out_proj",
    )(y, w, norm_g, x)


def _mlp_kernel(x_ref, g_in_ref, w1_ref, w2_ref, g_out_ref, o_ref, h_ref):
    f = pl.program_id(1)

    def chunk_out():
        hid = jnp.maximum(jnp.dot(h_ref[...], w1_ref[...], preferred_element_type=F32), 0.0)
        return jnp.dot((hid * hid).astype(BF16), w2_ref[...], preferred_element_type=F32)

    @pl.when(f == 0)
    def _():
        _norm_rows(h_ref, x_ref, g_in_ref)
        o_ref[...] = chunk_out()

    @pl.when(f > 0)
    def _():
        o_ref[...] += chunk_out()

    @pl.when(f == pl.num_programs(1) - 1)
    def _():
        _norm_rows(o_ref, o_ref, g_out_ref, res_ref=x_ref)


def mlp(x, norm_g, g_in_idx, g_out_idx, w1, w2, layer, tm=1024, tf=1024):
    m = x.shape[0]
    return pl.pallas_call(
        _mlp_kernel,
        grid=(m // tm, D_FF // tf),
        in_specs=[pl.BlockSpec((tm, D_MODEL), lambda i, f: (i, 0)),
                  pl.BlockSpec((None, 1, D_MODEL), lambda i, f: (g_in_idx, 0, 0)),
                  pl.BlockSpec((None, D_MODEL, tf), lambda i, f: (layer, 0, f)),
                  pl.BlockSpec((None, tf, D_MODEL), lambda i, f: (layer, f, 0)),
                  pl.BlockSpec((None, 1, D_MODEL), lambda i, f: (g_out_idx, 0, 0))],
        out_specs=pl.BlockSpec((tm, D_MODEL), lambda i, f: (i, 0)),
        out_shape=jax.ShapeDtypeStruct((m, D_MODEL), F32),
        scratch_shapes=[pltpu.VMEM((tm, D_MODEL), BF16)],
        compiler_params=_params(("parallel", "arbitrary")),
        name="mlp",
    )(x, norm_g, w1, w2, norm_g)


def kernel(x, positions, norm_g, ret_w_in, ret_w_out, mlstm_w_in, mlstm_b_gate,
           mlstm_norm_g, mlstm_w_out, mlp_w1, mlp_w2):
    batch, seq, d = x.shape
    m = batch * seq
    xf = x.reshape(m, d)

    inv_freq = jnp.power(ROPE_BASE, -jnp.linspace(0.0, 1.0, RET_DK // 2, dtype=F32))
    cos, sin = rope_tables(positions, inv_freq)

    gains = norm_g.astype(F32).reshape(DEPTH * 4, 1, d)
    ret_w_in_b = ret_w_in.astype(BF16)
    ret_w_out_b = ret_w_out.astype(BF16)
    ml_w_in_b = mlstm_w_in.astype(BF16)
    ml_w_out_b = mlstm_w_out.astype(BF16)
    w1_b = mlp_w1.astype(BF16)
    w2_b = mlp_w2.astype(BF16)
    n_gate = 2 * ML_HEADS

    for i in range(DEPTH):
        j = i // 2
        if i % 2 == 0:
            proj = ret_proj(xf, gains, 4 * i, ret_w_in_b, j, cos, sin)
            xf = ret_mix_out(proj, xf, ret_w_out_b, j, gains, 4 * i + 1, batch, seq)
        else:
            w_gate = ml_w_in_b[j, :, ML_MAIN:]
            wg = jnp.pad(w_gate, ((0, 0), (0, GATE_LANES - n_gate)))
            bias = mlstm_b_gate[j].astype(F32)
            b = jnp.pad(bias, (0, GATE_LANES - n_gate)).reshape(1, GATE_LANES)
            proj, gcol, grow = ml_proj(xf, gains, 4 * i, ml_w_in_b, j, wg, b)
            y = ml_mix(proj, gcol, grow, mlstm_norm_g[j].reshape(1, ML_V).astype(F32),
                       batch, seq)
            xf = out_proj(y, ml_w_out_b, j, gains, 4 * i + 1, xf)
        xf = mlp(xf, gains, 4 * i + 2, 4 * i + 3, w1_b, w2_b, i)
    return xf.reshape(batch, seq, d)
```

```python
import functools
import math

import jax
import jax.numpy as jnp
from jax import lax
from jax.experimental import pallas as pl
from jax.experimental.pallas import tpu as pltpu

F32 = jnp.float32
BF16 = jnp.bfloat16

D_MODEL = 2048
DEPTH = 4
CHUNK = 256
RET_CHUNK = 256
EPS = 1e-6
D_FF = 4 * D_MODEL

RET_HEADS = 8
RET_DK = 256
RET_QK = RET_HEADS * RET_DK
RET_V = 2 * D_MODEL
RET_DV = RET_V // RET_HEADS
RET_IN = 2 * RET_QK + 2 * RET_V
ROPE_BASE = 10000.0

ML_HEADS = 8
ML_QK = D_MODEL // 2
ML_DQK = ML_QK // ML_HEADS
ML_V = D_MODEL
ML_DV = ML_V // ML_HEADS
ML_MAIN = 2 * ML_QK + 2 * ML_V
GATE_SOFTCAP = 15.0
GATE_LANES = 128

VMEM_LIMIT = 60 * 1024 * 1024

NT_DIMS = (((1,), (1,)), ((), ()))
TN_DIMS = (((0,), (0,)), ((), ()))


def _rms(x, g):
    ms = jnp.mean(x * x, axis=-1, keepdims=True)
    return x * lax.rsqrt(ms + EPS) * g


NORM_ROWS = 256
NORM_SUB = 64


def _norm_rows(dst_ref, src_ref, g_ref, res_ref=None):
    g = g_ref[...]

    def body(r, carry):
        rows = pl.ds(pl.multiple_of(r * NORM_ROWS, NORM_ROWS), NORM_ROWS)
        sq = src_ref[rows, :]
        inv = lax.rsqrt(jnp.mean(sq * sq, axis=-1, keepdims=True) + EPS)
        for t in range(NORM_ROWS // NORM_SUB):
            sub = pl.ds(pl.multiple_of(r * NORM_ROWS + t * NORM_SUB, NORM_SUB), NORM_SUB)
            v = src_ref[sub, :] * inv[t * NORM_SUB:(t + 1) * NORM_SUB] * g
            if res_ref is not None:
                v = res_ref[sub, :] + v
            dst_ref[sub, :] = v.astype(dst_ref.dtype)
        return carry

    lax.fori_loop(0, src_ref.shape[0] // NORM_ROWS, body, 0)


def _params(sem, **kw):
    return pltpu.CompilerParams(dimension_semantics=sem, vmem_limit_bytes=VMEM_LIMIT, **kw)


def _rope_kernel(pos_ref, freq_ref, cos_ref, sin_ref):
    ang = pos_ref[...].astype(F32) * freq_ref[...]
    cos_ref[...] = jnp.cos(ang)
    sin_ref[...] = jnp.sin(ang)


def rope_tables(positions, inv_freq, tm=2048):
    m = positions.size
    pos = positions.reshape(m, 1)
    half = inv_freq.shape[-1]
    return pl.pallas_call(
        _rope_kernel,
        grid=(m // tm,),
        in_specs=[pl.BlockSpec((tm, 1), lambda i: (i, 0)),
                  pl.BlockSpec((1, half), lambda i: (0, 0))],
        out_specs=[pl.BlockSpec((tm, half), lambda i: (i, 0)),
                   pl.BlockSpec((tm, half), lambda i: (i, 0))],
        out_shape=[jax.ShapeDtypeStruct((m, half), F32)] * 2,
        compiler_params=_params(("parallel",)),
        name="rope_tables",
    )(pos, inv_freq.reshape(1, half))


def _ret_proj_kernel(x_ref, g_ref, w_ref, cos_ref, sin_ref, o_ref, h_ref, *, tn):
    j = pl.program_id(1)
    n_rot = 2 * RET_QK // tn

    @pl.when(j == 0)
    def _():
        _norm_rows(h_ref, x_ref, g_ref)

    @pl.when(j < n_rot)
    def _():
        res = jnp.dot(h_ref[...], w_ref[...], preferred_element_type=F32)
        scale = jnp.where(j >= n_rot // 2, RET_DK ** -0.5, 1.0).astype(F32)
        c = cos_ref[...] * scale
        s = sin_ref[...] * scale
        half = RET_DK // 2
        for hh in range(tn // RET_DK):
            lo = hh * RET_DK
            t1 = res[:, lo:lo + half]
            t2 = res[:, lo + half:lo + RET_DK]
            o_ref[:, lo:lo + half] = (t1 * c - t2 * s).astype(BF16)
            o_ref[:, lo + half:lo + RET_DK] = (t2 * c + t1 * s).astype(BF16)

    @pl.when(j >= n_rot)
    def _():
        o_ref[...] = jnp.dot(h_ref[...], w_ref[...],
                             preferred_element_type=F32).astype(BF16)


def ret_proj(x, norm_g, g_idx, w, layer, cos, sin, tm=1024, tn=2048):
    m = x.shape[0]
    return pl.pallas_call(
        functools.partial(_ret_proj_kernel, tn=tn),
        grid=(m // tm, RET_IN // tn),
        in_specs=[pl.BlockSpec((tm, D_MODEL), lambda i, j: (i, 0)),
                  pl.BlockSpec((None, 1, D_MODEL), lambda i, j: (g_idx, 0, 0)),
                  pl.BlockSpec((None, D_MODEL, tn), lambda i, j: (layer, 0, j)),
                  pl.BlockSpec((tm, RET_DK // 2), lambda i, j: (i, 0)),
                  pl.BlockSpec((tm, RET_DK // 2), lambda i, j: (i, 0))],
        out_specs=pl.BlockSpec((tm, tn), lambda i, j: (i, j)),
        out_shape=jax.ShapeDtypeStruct((m, RET_IN), BF16),
        scratch_shapes=[pltpu.VMEM((tm, D_MODEL), BF16)],
        compiler_params=_params(("parallel", "arbitrary")),
        name="ret_proj",
    )(x, norm_g, w, cos, sin)


def _ret_log_gamma(h):
    return math.log1p(-(2.0 ** (-5.0 - h)))


OUT_PIECE = 256


def _ret_kernel(q_ref, k_ref, v_ref, g_ref, x_ref, w_ref, gain_ref, o_ref,
                st_ref, dec_ref, xi_ref, zeta_ref, y_ref, *, nc):
    t = pl.program_id(0)
    slot = t % 2

    @pl.when(t == 0)
    def _():
        y_ref[...] = jnp.zeros_like(y_ref)

    @pl.when(t % nc == 0)
    def _():
        st_ref[...] = jnp.zeros_like(st_ref)
        n = lax.broadcasted_iota(jnp.int32, (RET_CHUNK, RET_CHUNK), 0)
        mcol = lax.broadcasted_iota(jnp.int32, (RET_CHUNK, RET_CHUNK), 1)
        rel = (n - mcol).astype(F32)
        row = lax.broadcasted_iota(jnp.int32, (RET_CHUNK, RET_DK), 0).astype(F32)
        for h in range(RET_HEADS):
            lg = _ret_log_gamma(h)
            dec_ref[h] = jnp.where(rel >= 0, jnp.exp(jnp.maximum(rel, 0.0) * lg), 0.0)
            xi_ref[h] = jnp.exp((row + 1.0) * lg).astype(BF16)
            zeta_ref[h] = jnp.exp((RET_CHUNK - 1.0 - row) * lg).astype(BF16)

    heads = range(RET_HEADS)
    q = [q_ref[:, h * RET_DK:(h + 1) * RET_DK] for h in heads]
    k = [k_ref[:, h * RET_DK:(h + 1) * RET_DK] for h in heads]
    v = [v_ref[:, h * RET_DV:(h + 1) * RET_DV] for h in heads]


    y_prev = y_ref[1 - slot]
    pieces = []

    def out_piece():
        p = len(pieces)
        pieces.append(jnp.dot(y_prev, w_ref[:, p * OUT_PIECE:(p + 1) * OUT_PIECE],
                              preferred_element_type=F32))

    raw = [lax.dot_general(q[h], k[h], NT_DIMS, preferred_element_type=F32) for h in heads]

    lhs, rhs, kz = [], [], []
    for h in heads:
        if h % 2 == 0:
            out_piece()
        lhs.append(jnp.concatenate([(raw[h] * dec_ref[h]).astype(BF16), q[h] * xi_ref[h]],
                                   axis=1))
        rhs.append(jnp.concatenate([v[h], st_ref[h].astype(BF16)], axis=0))
        kz.append(k[h] * zeta_ref[h])

    y = []
    for h in heads:
        g_chunk = math.exp(RET_CHUNK * _ret_log_gamma(h))
        y.append(jnp.dot(lhs[h], rhs[h], preferred_element_type=F32))
        st_ref[h] = st_ref[h] * g_chunk + lax.dot_general(kz[h], v[h], TN_DIMS,
                                                          preferred_element_type=F32)

    for h in heads:
        if h % 2 == 0:
            out_piece()
        cols = slice(h * RET_DV, (h + 1) * RET_DV)
        mu = jnp.mean(y[h], axis=-1, keepdims=True)
        yc = y[h] - mu
        var = jnp.mean(yc * yc, axis=-1, keepdims=True)
        yn = yc * lax.rsqrt(var + EPS)
        gh = g_ref[:, cols]
        y_ref[slot, :, cols] = (gh * jax.nn.sigmoid(gh)) * yn.astype(BF16)

    out = jnp.concatenate(pieces, axis=1)
    o_ref[...] = x_ref[...] + _rms(out, gain_ref[...])


def ret_mix_out(proj, x, w_out, layer, norm_g, g_idx, batch, seq):
    assert D_MODEL == RET_HEADS * OUT_PIECE
    m = batch * seq
    n_chunks = m // RET_CHUNK
    cur = lambda t: jnp.minimum(t, n_chunks - 1)
    prev = lambda t: jnp.maximum(t - 1, 0)
    return pl.pallas_call(
        functools.partial(_ret_kernel, nc=seq // RET_CHUNK),
        grid=(n_chunks + 1,),
        in_specs=[pl.BlockSpec((RET_CHUNK, RET_QK), lambda t: (cur(t), 0)),
                  pl.BlockSpec((RET_CHUNK, RET_QK), lambda t: (cur(t), 1)),
                  pl.BlockSpec((RET_CHUNK, RET_V), lambda t: (cur(t), 1)),
                  pl.BlockSpec((RET_CHUNK, RET_V), lambda t: (cur(t), 2)),
                  pl.BlockSpec((RET_CHUNK, D_MODEL), lambda t: (prev(t), 0)),
                  pl.BlockSpec((None, RET_V, D_MODEL), lambda t: (layer, 0, 0),
                               pipeline_mode=pl.Buffered(1)),
                  pl.BlockSpec((None, 1, D_MODEL), lambda t: (g_idx, 0, 0))],
        out_specs=pl.BlockSpec((RET_CHUNK, D_MODEL), lambda t: (prev(t), 0)),
        out_shape=jax.ShapeDtypeStruct((m, D_MODEL), F32),
        scratch_shapes=[pltpu.VMEM((RET_HEADS, RET_DK, RET_DV), F32),
                        pltpu.VMEM((RET_HEADS, RET_CHUNK, RET_CHUNK), F32),
                        pltpu.VMEM((RET_HEADS, RET_CHUNK, RET_DK), BF16),
                        pltpu.VMEM((RET_HEADS, RET_CHUNK, RET_DK), BF16),
                        pltpu.VMEM((2, RET_CHUNK, RET_V), BF16)],
        compiler_params=_params(("arbitrary",)),
        name="ret_mix_out",
    )(proj, proj, proj, proj, x, w_out, norm_g)


def _log_sigmoid(x):
    return jnp.minimum(x, 0.0) - jnp.log1p(jnp.exp(-jnp.abs(x)))


def _gate_logs(pre, is_forget):
    capped = GATE_SOFTCAP * jnp.tanh(pre / GATE_SOFTCAP)
    return jnp.where(is_forget, _log_sigmoid(capped), capped)


def _ml_proj_kernel(x_ref, g_ref, w_ref, wg_ref, b_ref, o_ref, gcol_ref, grow_ref, h_ref):
    j = pl.program_id(1)

    @pl.when(j == 0)
    def _():
        _norm_rows(h_ref, x_ref, g_ref)
        pre = jnp.dot(h_ref[...], wg_ref[...], preferred_element_type=F32) + b_ref[...]
        lane = lax.broadcasted_iota(jnp.int32, pre.shape, 1)
        logs = _gate_logs(pre, lane >= ML_HEADS)
        gcol_ref[...] = logs
        grow_ref[...] = logs.T[:2 * ML_HEADS, :]

    o_ref[...] = jnp.dot(h_ref[...], w_ref[...], preferred_element_type=F32).astype(BF16)


def ml_proj(x, norm_g, g_idx, w, layer, wg, b, tm=1024, tn=2048):
    m = x.shape[0]
    ng = 2 * ML_HEADS
    return pl.pallas_call(
        _ml_proj_kernel,
        grid=(m // tm, ML_MAIN // tn),
        in_specs=[pl.BlockSpec((tm, D_MODEL), lambda i, j: (i, 0)),
                  pl.BlockSpec((None, 1, D_MODEL), lambda i, j: (g_idx, 0, 0)),
                  pl.BlockSpec((None, D_MODEL, tn), lambda i, j: (layer, 0, j)),
                  pl.BlockSpec((D_MODEL, GATE_LANES), lambda i, j: (0, 0)),
                  pl.BlockSpec((1, GATE_LANES), lambda i, j: (0, 0))],
        out_specs=[pl.BlockSpec((tm, tn), lambda i, j: (i, j)),
                   pl.BlockSpec((tm, GATE_LANES), lambda i, j: (i, 0)),
                   pl.BlockSpec((ng, tm), lambda i, j: (0, i))],
        out_shape=[jax.ShapeDtypeStruct((m, ML_MAIN), BF16),
                   jax.ShapeDtypeStruct((m, GATE_LANES), F32),
                   jax.ShapeDtypeStruct((ng, m), F32)],
        scratch_shapes=[pltpu.VMEM((tm, D_MODEL), BF16)],
        compiler_params=_params(("parallel", "arbitrary")),
        name="ml_proj",
    )(x, norm_g, w, wg, b)


def _ml_kernel(q_ref, k_ref, v_ref, o_ref, gcol_ref, grow_ref, ng_ref, x_ref, w_ref, gain_ref,
               out_ref, c_ref, n_ref, m_ref, y_ref, *, nc):
    t = pl.program_id(0)
    slot = t % 2

    @pl.when(t == 0)
    def _():
        y_ref[...] = jnp.zeros_like(y_ref)

    @pl.when(t % nc == 0)
    def _():
        c_ref[...] = jnp.zeros_like(c_ref)
        n_ref[...] = jnp.zeros_like(n_ref)
        m_ref[...] = jnp.zeros_like(m_ref)

    s_idx = lax.broadcasted_iota(jnp.int32, (CHUNK, CHUNK), 0)
    n_idx = lax.broadcasted_iota(jnp.int32, (CHUNK, CHUNK), 1)
    causal = s_idx <= n_idx
    eye = s_idx == n_idx
    lower = (s_idx >= n_idx).astype(F32)
    upper = causal.astype(F32)

    gcol = gcol_ref[...]
    grow = grow_ref[...]
    hi = lax.Precision.HIGHEST
    cum_col = jnp.dot(lower, gcol, precision=hi, preferred_element_type=F32)
    cum_row = jnp.dot(grow, upper, precision=hi, preferred_element_type=F32)
    key_col = gcol - pltpu.roll(cum_col, GATE_LANES - ML_HEADS, axis=1)
    scale = ML_DQK ** -0.5
    log_scale = math.log(scale)

    heads = range(ML_HEADS)
    q = [q_ref[:, h * ML_DQK:(h + 1) * ML_DQK] for h in heads]
    k = [k_ref[:, h * ML_DQK:(h + 1) * ML_DQK] for h in heads]
    v = [v_ref[:, h * ML_DV:(h + 1) * ML_DV] for h in heads]


    raw, qn, qc = [], [], []
    for h in heads:
        raw.append(lax.dot_general(k[h], q[h], NT_DIMS, preferred_element_type=F32))
        qn.append(lax.dot_general(n_ref[h].astype(BF16), q[h], NT_DIMS,
                                  preferred_element_type=F32)[0:1, :])
        qc.append(jnp.dot(q[h], c_ref[h].astype(BF16),
                          preferred_element_type=F32).astype(BF16))

    y_prev = y_ref[1 - slot]
    pieces = []

    def out_piece():
        p = len(pieces)
        pieces.append(jnp.dot(y_prev, w_ref[:, p * OUT_PIECE:(p + 1) * OUT_PIECE],
                              preferred_element_type=F32))

    lhs_t, kw_t, w_rows, carry, m_new = [], [], [], [], []
    for h in heads:
        if h % 2 == 0:
            out_piece()
        brow = cum_row[ML_HEADS + h:ML_HEADS + h + 1, :]
        irow = grow[h:h + 1, :]
        kcol = key_col[:, h:h + 1]
        m_st = m_ref[h][0:1, 0:1]

        d_log = jnp.where(causal, brow + kcol, -jnp.inf)
        inter_log = brow + m_st
        m_out = jnp.maximum(inter_log, jnp.max(d_log, axis=0, keepdims=True))
        d_w = jnp.exp(d_log - (m_out - log_scale))
        inter_w = jnp.exp(inter_log - m_out)
        scores = raw[h] * d_w
        den = jnp.sum(scores, axis=0, keepdims=True) + inter_w * qn[h]
        r = 1.0 / jnp.maximum(jnp.abs(den), jnp.exp(-m_out))
        p_t = (scores * r).astype(BF16)
        diag = jnp.where(eye, inter_w * r, 0.0).astype(BF16)
        lhs_t.append(jnp.concatenate([p_t, diag], axis=0))

        b_tot = brow[:, CHUNK - 1:CHUNK]
        w_log = b_tot - brow + irow
        m_nw = jnp.maximum(b_tot + m_st, jnp.max(w_log, axis=1, keepdims=True))
        w_s = jnp.exp(w_log - m_nw) * scale
        carry.append(jnp.exp(b_tot + m_st - m_nw))
        m_new.append(m_nw)
        kw_t.append((k[h].astype(F32).T * w_s).astype(BF16))
        w_rows.append(jnp.broadcast_to(w_s, (8, CHUNK)).astype(BF16))

    h_out = []
    for h in heads:
        rhs = jnp.concatenate([v[h], qc[h]], axis=0)
        h_out.append(lax.dot_general(lhs_t[h], rhs, TN_DIMS, preferred_element_type=F32))
        c_ref[h] = carry[h] * c_ref[h] + jnp.dot(kw_t[h], v[h], preferred_element_type=F32)
        n_ref[h] = carry[h] * n_ref[h] + jnp.dot(w_rows[h], k[h], preferred_element_type=F32)
        m_ref[h] = jnp.broadcast_to(m_new[h], m_ref.shape[1:])

    for h in heads:
        if h % 2 == 0:
            out_piece()
        cols = slice(h * ML_DV, (h + 1) * ML_DV)
        y = h_out[h]
        yn = y * lax.rsqrt(jnp.mean(y * y, axis=-1, keepdims=True) + EPS) * ng_ref[:, cols]
        y_ref[slot, :, cols] = yn.astype(BF16) * jax.nn.sigmoid(o_ref[:, cols])

    out = jnp.concatenate(pieces, axis=1)
    out_ref[...] = x_ref[...] + _rms(out, gain_ref[...])


def ml_mix_out(proj, gcol, grow, head_g, x, w_out, layer, norm_g, g_idx, batch, seq):
    assert D_MODEL == ML_HEADS * OUT_PIECE
    m = batch * seq
    n_chunks = m // CHUNK
    cur = lambda t: jnp.minimum(t, n_chunks - 1)
    prev = lambda t: jnp.maximum(t - 1, 0)
    return pl.pallas_call(
        functools.partial(_ml_kernel, nc=seq // CHUNK),
        grid=(n_chunks + 1,),
        in_specs=[pl.BlockSpec((CHUNK, ML_QK), lambda t: (cur(t), 0)),
                  pl.BlockSpec((CHUNK, ML_QK), lambda t: (cur(t), 1)),
                  pl.BlockSpec((CHUNK, ML_V), lambda t: (cur(t), 1)),
                  pl.BlockSpec((CHUNK, ML_V), lambda t: (cur(t), 2)),
                  pl.BlockSpec((CHUNK, GATE_LANES), lambda t: (cur(t), 0)),
                  pl.BlockSpec((2 * ML_HEADS, CHUNK), lambda t: (0, cur(t))),
                  pl.BlockSpec((1, ML_V), lambda t: (0, 0)),
                  pl.BlockSpec((CHUNK, D_MODEL), lambda t: (prev(t), 0)),
                  pl.BlockSpec((None, ML_V, D_MODEL), lambda t: (layer, 0, 0),
                               pipeline_mode=pl.Buffered(1)),
                  pl.BlockSpec((None, 1, D_MODEL), lambda t: (g_idx, 0, 0))],
        out_specs=pl.BlockSpec((CHUNK, D_MODEL), lambda t: (prev(t), 0)),
        out_shape=jax.ShapeDtypeStruct((m, D_MODEL), F32),
        scratch_shapes=[pltpu.VMEM((ML_HEADS, ML_DQK, ML_DV), F32),
                        pltpu.VMEM((ML_HEADS, 8, ML_DQK), F32),
                        pltpu.VMEM((ML_HEADS, 8, 128), F32),
                        pltpu.VMEM((2, CHUNK, ML_V), BF16)],
        compiler_params=_params(("arbitrary",)),
        name="ml_mix_out",
    )(proj, proj, proj, proj, gcol, grow, head_g, x, w_out, norm_g)


def _mlp_kernel(x_ref, g_in_ref, w1_ref, w2_ref, g_out_ref, o_ref, h_ref):
    f = pl.program_id(1)

    def chunk_out():
        hid = jnp.maximum(jnp.dot(h_ref[...], w1_ref[...], preferred_element_type=F32), 0.0)
        return jnp.dot((hid * hid).astype(BF16), w2_ref[...], preferred_element_type=F32)

    @pl.when(f == 0)
    def _():
        _norm_rows(h_ref, x_ref, g_in_ref)
        o_ref[...] = chunk_out()

    @pl.when(f > 0)
    def _():
        o_ref[...] += chunk_out()

    @pl.when(f == pl.num_programs(1) - 1)
    def _():
        _norm_rows(o_ref, o_ref, g_out_ref, res_ref=x_ref)


def mlp(x, norm_g, g_in_idx, g_out_idx, w1, w2, layer, tm=1024, tf=1024):
    m = x.shape[0]
    return pl.pallas_call(
        _mlp_kernel,
        grid=(m // tm, D_FF // tf),
        in_specs=[pl.BlockSpec((tm, D_MODEL), lambda i, f: (i, 0)),
                  pl.BlockSpec((None, 1, D_MODEL), lambda i, f: (g_in_idx, 0, 0)),
                  pl.BlockSpec((None, D_MODEL, tf), lambda i, f: (layer, 0, f)),
                  pl.BlockSpec((None, tf, D_MODEL), lambda i, f: (layer, f, 0)),
                  pl.BlockSpec((None, 1, D_MODEL), lambda i, f: (g_out_idx, 0, 0))],
        out_specs=pl.BlockSpec((tm, D_MODEL), lambda i, f: (i, 0)),
        out_shape=jax.ShapeDtypeStruct((m, D_MODEL), F32),
        scratch_shapes=[pltpu.VMEM((tm, D_MODEL), BF16)],
        compiler_params=_params(("parallel", "arbitrary")),
        name="mlp",
    )(x, norm_g, w1, w2, norm_g)


def kernel(x, positions, norm_g, ret_w_in, ret_w_out, mlstm_w_in, mlstm_b_gate,
           mlstm_norm_g, mlstm_w_out, mlp_w1, mlp_w2):
    batch, seq, d = x.shape
    m = batch * seq
    xf = x.reshape(m, d)

    inv_freq = jnp.power(ROPE_BASE, -jnp.linspace(0.0, 1.0, RET_DK // 2, dtype=F32))
    cos, sin = rope_tables(positions, inv_freq)

    gains = norm_g.astype(F32).reshape(DEPTH * 4, 1, d)
    ret_w_in_b = ret_w_in.astype(BF16)
    ret_w_out_b = ret_w_out.astype(BF16)
    ml_w_in_b = mlstm_w_in.astype(BF16)
    ml_w_out_b = mlstm_w_out.astype(BF16)
    w1_b = mlp_w1.astype(BF16)
    w2_b = mlp_w2.astype(BF16)
    n_gate = 2 * ML_HEADS

    for i in range(DEPTH):
        j = i // 2
        if i % 2 == 0:
            proj = ret_proj(xf, gains, 4 * i, ret_w_in_b, j, cos, sin)
            xf = ret_mix_out(proj, xf, ret_w_out_b, j, gains, 4 * i + 1, batch, seq)
        else:
            w_gate = ml_w_in_b[j, :, ML_MAIN:]
            wg = jnp.pad(w_gate, ((0, 0), (0, GATE_LANES - n_gate)))
            bias = mlstm_b_gate[j].astype(F32)
            b = jnp.pad(bias, (0, GATE_LANES - n_gate)).reshape(1, GATE_LANES)
            proj, gcol, grow = ml_proj(xf, gains, 4 * i, ml_w_in_b, j, wg, b)
            head_g = mlstm_norm_g[j].reshape(1, ML_V).astype(F32)
            xf = ml_mix_out(proj, gcol, grow, head_g, xf, ml_w_out_b, j, gains, 4 * i + 1,
                            batch, seq)
        xf = mlp(xf, gains, 4 * i + 2, 4 * i + 3, w1_b, w2_b, i)
    return xf.reshape(batch, seq, d)
```

```python
import functools
import math

import jax
import jax.numpy as jnp
from jax import lax
from jax.experimental import pallas as pl
from jax.experimental.pallas import tpu as pltpu

F32 = jnp.float32
BF16 = jnp.bfloat16

D_MODEL = 2048
DEPTH = 4
CHUNK = 256
RET_CHUNK = 256
EPS = 1e-6
D_FF = 4 * D_MODEL

RET_HEADS = 8
RET_DK = 256
RET_QK = RET_HEADS * RET_DK
RET_V = 2 * D_MODEL
RET_DV = RET_V // RET_HEADS
RET_IN = 2 * RET_QK + 2 * RET_V
ROPE_BASE = 10000.0

ML_HEADS = 8
ML_QK = D_MODEL // 2
ML_DQK = ML_QK // ML_HEADS
ML_V = D_MODEL
ML_DV = ML_V // ML_HEADS
ML_MAIN = 2 * ML_QK + 2 * ML_V
GATE_SOFTCAP = 15.0
GATE_LANES = 128

VMEM_LIMIT = 60 * 1024 * 1024

NT_DIMS = (((1,), (1,)), ((), ()))
TN_DIMS = (((0,), (0,)), ((), ()))


def _rms(x, g):
    ms = jnp.mean(x * x, axis=-1, keepdims=True)
    return x * lax.rsqrt(ms + EPS) * g


CAST_BLOCKS = 64
NORM_ROWS = 256
NORM_SUB = 64


def _norm_rows(dst_ref, src_ref, g_ref, res_ref=None):
    g = g_ref[...]

    def body(r, carry):
        rows = pl.ds(pl.multiple_of(r * NORM_ROWS, NORM_ROWS), NORM_ROWS)
        sq = src_ref[rows, :]
        inv = lax.rsqrt(jnp.mean(sq * sq, axis=-1, keepdims=True) + EPS)
        for t in range(NORM_ROWS // NORM_SUB):
            sub = pl.ds(pl.multiple_of(r * NORM_ROWS + t * NORM_SUB, NORM_SUB), NORM_SUB)
            v = src_ref[sub, :] * inv[t * NORM_SUB:(t + 1) * NORM_SUB] * g
            if res_ref is not None:
                v = res_ref[sub, :] + v
            dst_ref[sub, :] = v.astype(dst_ref.dtype)
        return carry

    lax.fori_loop(0, src_ref.shape[0] // NORM_ROWS, body, 0)


def _params(sem, **kw):
    return pltpu.CompilerParams(dimension_semantics=sem, vmem_limit_bytes=VMEM_LIMIT, **kw)


def _rope_kernel(pos_ref, freq_ref, cos_ref, sin_ref):
    ang = pos_ref[...].astype(F32) * freq_ref[...]
    cos_ref[...] = jnp.cos(ang)
    sin_ref[...] = jnp.sin(ang)


def rope_tables(positions, inv_freq, tm=2048):
    m = positions.size
    pos = positions.reshape(m, 1)
    half = inv_freq.shape[-1]
    return pl.pallas_call(
        _rope_kernel,
        grid=(m // tm,),
        in_specs=[pl.BlockSpec((tm, 1), lambda i: (i, 0)),
                  pl.BlockSpec((1, half), lambda i: (0, 0))],
        out_specs=[pl.BlockSpec((tm, half), lambda i: (i, 0)),
                   pl.BlockSpec((tm, half), lambda i: (i, 0))],
        out_shape=[jax.ShapeDtypeStruct((m, half), F32)] * 2,
        compiler_params=_params(("parallel",)),
        name="rope_tables",
    )(pos, inv_freq.reshape(1, half))


def _cast_blocks(src_refs, dst_refs):
    for src, dst in zip(src_refs, dst_refs):
        dst[...] = src[...].astype(BF16)


def _cast_specs(stack, layer, rows, step_of):
    _, r, c = stack.shape
    last = r // rows - 1
    blk = lambda *ids: jnp.minimum(step_of(*ids), last)
    return (pl.BlockSpec((None, rows, c), lambda *ids: (layer, blk(*ids), 0)),
            pl.BlockSpec((rows, c), lambda *ids: (blk(*ids), 0)),
            jax.ShapeDtypeStruct((r, c), BF16))


def _ret_proj_kernel(x_ref, g_ref, w_ref, cos_ref, sin_ref, w1f_ref, w2f_ref,
                     o_ref, w1b_ref, w2b_ref, h_ref, *, tn):
    j = pl.program_id(1)
    n_rot = 2 * RET_QK // tn

    @pl.when(j == 0)
    def _():
        _norm_rows(h_ref, x_ref, g_ref)

    @pl.when(j < n_rot)
    def _():
        _cast_blocks((w1f_ref, w2f_ref), (w1b_ref, w2b_ref))
        res = jnp.dot(h_ref[...], w_ref[...], preferred_element_type=F32)
        scale = jnp.where(j >= n_rot // 2, RET_DK ** -0.5, 1.0).astype(F32)
        c = cos_ref[...] * scale
        s = sin_ref[...] * scale
        half = RET_DK // 2
        for hh in range(tn // RET_DK):
            lo = hh * RET_DK
            t1 = res[:, lo:lo + half]
            t2 = res[:, lo + half:lo + RET_DK]
            o_ref[:, lo:lo + half] = (t1 * c - t2 * s).astype(BF16)
            o_ref[:, lo + half:lo + RET_DK] = (t2 * c + t1 * s).astype(BF16)

    @pl.when(j >= n_rot)
    def _():
        _cast_blocks((w1f_ref, w2f_ref), (w1b_ref, w2b_ref))
        o_ref[...] = jnp.dot(h_ref[...], w_ref[...],
                             preferred_element_type=F32).astype(BF16)


def ret_proj(x, norm_g, g_idx, w, layer, cos, sin, mlp_w1, mlp_w2, mlp_layer,
             tm=1024, tn=2048):
    m = x.shape[0]
    nj = RET_IN // tn
    n_steps = (m // tm) * nj
    step_of = lambda i, j: i * nj + j
    c1 = _cast_specs(mlp_w1, mlp_layer, mlp_w1.shape[1] // CAST_BLOCKS, step_of)
    c2 = _cast_specs(mlp_w2, mlp_layer, mlp_w2.shape[1] // CAST_BLOCKS, step_of)
    assert CAST_BLOCKS <= n_steps
    return pl.pallas_call(
        functools.partial(_ret_proj_kernel, tn=tn),
        grid=(m // tm, nj),
        in_specs=[pl.BlockSpec((tm, D_MODEL), lambda i, j: (i, 0)),
                  pl.BlockSpec((None, 1, D_MODEL), lambda i, j: (g_idx, 0, 0)),
                  pl.BlockSpec((None, D_MODEL, tn), lambda i, j: (layer, 0, j)),
                  pl.BlockSpec((tm, RET_DK // 2), lambda i, j: (i, 0)),
                  pl.BlockSpec((tm, RET_DK // 2), lambda i, j: (i, 0)),
                  c1[0], c2[0]],
        out_specs=[pl.BlockSpec((tm, tn), lambda i, j: (i, j)), c1[1], c2[1]],
        out_shape=[jax.ShapeDtypeStruct((m, RET_IN), BF16), c1[2], c2[2]],
        scratch_shapes=[pltpu.VMEM((tm, D_MODEL), BF16)],
        compiler_params=_params(("arbitrary", "arbitrary")),
        name="ret_proj",
    )(x, norm_g, w, cos, sin, mlp_w1, mlp_w2)


def _ret_log_gamma(h):
    return math.log1p(-(2.0 ** (-5.0 - h)))


OUT_PIECE = 512
PIECE_EVERY = 2 * RET_HEADS * OUT_PIECE // D_MODEL


def _ret_kernel(q_ref, k_ref, v_ref, g_ref, x_ref, w_ref, gain_ref, o_ref,
                st_ref, dec_ref, xi_ref, zeta_ref, y_ref, *, nc):
    t = pl.program_id(0)
    slot = t % 2

    @pl.when(t == 0)
    def _():
        y_ref[...] = jnp.zeros_like(y_ref)

    @pl.when(t % nc == 0)
    def _():
        st_ref[...] = jnp.zeros_like(st_ref)
        n = lax.broadcasted_iota(jnp.int32, (RET_CHUNK, RET_CHUNK), 0)
        mcol = lax.broadcasted_iota(jnp.int32, (RET_CHUNK, RET_CHUNK), 1)
        rel = (n - mcol).astype(F32)
        row = lax.broadcasted_iota(jnp.int32, (RET_CHUNK, RET_DK), 0).astype(F32)
        for h in range(RET_HEADS):
            lg = _ret_log_gamma(h)
            dec_ref[h] = jnp.where(rel >= 0, jnp.exp(jnp.maximum(rel, 0.0) * lg), 0.0)
            xi_ref[h] = jnp.exp((row + 1.0) * lg).astype(BF16)
            zeta_ref[h] = jnp.exp((RET_CHUNK - 1.0 - row) * lg).astype(BF16)

    heads = range(RET_HEADS)
    q = [q_ref[:, h * RET_DK:(h + 1) * RET_DK] for h in heads]
    k = [k_ref[:, h * RET_DK:(h + 1) * RET_DK] for h in heads]
    v = [v_ref[:, h * RET_DV:(h + 1) * RET_DV] for h in heads]


    y_prev = y_ref[1 - slot]
    pieces = []

    def out_piece():
        p = len(pieces)
        pieces.append(jnp.dot(y_prev, w_ref[:, p * OUT_PIECE:(p + 1) * OUT_PIECE],
                              preferred_element_type=F32))

    raw = [lax.dot_general(q[h], k[h], NT_DIMS, preferred_element_type=F32) for h in heads]

    lhs, rhs, kz = [], [], []
    for h in heads:
        if h % PIECE_EVERY == 0:
            out_piece()
        lhs.append(jnp.concatenate([(raw[h] * dec_ref[h]).astype(BF16), q[h] * xi_ref[h]],
                                   axis=1))
        rhs.append(jnp.concatenate([v[h], st_ref[h].astype(BF16)], axis=0))
        kz.append(k[h] * zeta_ref[h])

    y = []
    for h in heads:
        g_chunk = math.exp(RET_CHUNK * _ret_log_gamma(h))
        y.append(jnp.dot(lhs[h], rhs[h], preferred_element_type=F32))
        st_ref[h] = st_ref[h] * g_chunk + lax.dot_general(kz[h], v[h], TN_DIMS,
                                                          preferred_element_type=F32)

    for h in heads:
        if h % PIECE_EVERY == 0:
            out_piece()
        cols = slice(h * RET_DV, (h + 1) * RET_DV)
        mu = jnp.mean(y[h], axis=-1, keepdims=True)
        yc = y[h] - mu
        var = jnp.mean(yc * yc, axis=-1, keepdims=True)
        yn = yc * lax.rsqrt(var + EPS)
        gh = g_ref[:, cols]
        y_ref[slot, :, cols] = (gh * jax.nn.sigmoid(gh)) * yn.astype(BF16)

    out = jnp.concatenate(pieces, axis=1)
    o_ref[...] = x_ref[...] + _rms(out, gain_ref[...])


def ret_mix_out(proj, x, w_out, layer, norm_g, g_idx, batch, seq):
    assert 2 * RET_HEADS == PIECE_EVERY * (D_MODEL // OUT_PIECE)
    m = batch * seq
    n_chunks = m // RET_CHUNK
    cur = lambda t: jnp.minimum(t, n_chunks - 1)
    prev = lambda t: jnp.maximum(t - 1, 0)
    return pl.pallas_call(
        functools.partial(_ret_kernel, nc=seq // RET_CHUNK),
        grid=(n_chunks + 1,),
        in_specs=[pl.BlockSpec((RET_CHUNK, RET_QK), lambda t: (cur(t), 0)),
                  pl.BlockSpec((RET_CHUNK, RET_QK), lambda t: (cur(t), 1)),
                  pl.BlockSpec((RET_CHUNK, RET_V), lambda t: (cur(t), 1)),
                  pl.BlockSpec((RET_CHUNK, RET_V), lambda t: (cur(t), 2)),
                  pl.BlockSpec((RET_CHUNK, D_MODEL), lambda t: (prev(t), 0)),
                  pl.BlockSpec((None, RET_V, D_MODEL), lambda t: (layer, 0, 0),
                               pipeline_mode=pl.Buffered(1)),
                  pl.BlockSpec((None, 1, D_MODEL), lambda t: (g_idx, 0, 0))],
        out_specs=pl.BlockSpec((RET_CHUNK, D_MODEL), lambda t: (prev(t), 0)),
        out_shape=jax.ShapeDtypeStruct((m, D_MODEL), F32),
        scratch_shapes=[pltpu.VMEM((RET_HEADS, RET_DK, RET_DV), F32),
                        pltpu.VMEM((RET_HEADS, RET_CHUNK, RET_CHUNK), F32),
                        pltpu.VMEM((RET_HEADS, RET_CHUNK, RET_DK), BF16),
                        pltpu.VMEM((RET_HEADS, RET_CHUNK, RET_DK), BF16),
                        pltpu.VMEM((2, RET_CHUNK, RET_V), BF16)],
        compiler_params=_params(("arbitrary",)),
        name="ret_mix_out",
    )(proj, proj, proj, proj, x, w_out, norm_g)


def _log_sigmoid(x):
    return jnp.minimum(x, 0.0) - jnp.log1p(jnp.exp(-jnp.abs(x)))


def _gate_logs(pre, is_forget):
    capped = GATE_SOFTCAP * jnp.tanh(pre / GATE_SOFTCAP)
    return jnp.where(is_forget, _log_sigmoid(capped), capped)


def _ml_proj_kernel(x_ref, g_ref, w_ref, wg_ref, b_ref, o_ref, gcol_ref, grow_ref, h_ref):
    j = pl.program_id(1)

    @pl.when(j == 0)
    def _():
        _norm_rows(h_ref, x_ref, g_ref)
        pre = jnp.dot(h_ref[...], wg_ref[...], preferred_element_type=F32) + b_ref[...]
        lane = lax.broadcasted_iota(jnp.int32, pre.shape, 1)
        logs = _gate_logs(pre, lane >= ML_HEADS)
        gcol_ref[...] = logs
        grow_ref[...] = logs.T[:2 * ML_HEADS, :]

    o_ref[...] = jnp.dot(h_ref[...], w_ref[...], preferred_element_type=F32).astype(BF16)


def ml_proj(x, norm_g, g_idx, w, layer, wg, b, tm=1024, tn=2048):
    m = x.shape[0]
    ng = 2 * ML_HEADS
    return pl.pallas_call(
        _ml_proj_kernel,
        grid=(m // tm, ML_MAIN // tn),
        in_specs=[pl.BlockSpec((tm, D_MODEL), lambda i, j: (i, 0)),
                  pl.BlockSpec((None, 1, D_MODEL), lambda i, j: (g_idx, 0, 0)),
                  pl.BlockSpec((None, D_MODEL, tn), lambda i, j: (layer, 0, j)),
                  pl.BlockSpec((D_MODEL, GATE_LANES), lambda i, j: (0, 0)),
                  pl.BlockSpec((1, GATE_LANES), lambda i, j: (0, 0))],
        out_specs=[pl.BlockSpec((tm, tn), lambda i, j: (i, j)),
                   pl.BlockSpec((tm, GATE_LANES), lambda i, j: (i, 0)),
                   pl.BlockSpec((ng, tm), lambda i, j: (0, i))],
        out_shape=[jax.ShapeDtypeStruct((m, ML_MAIN), BF16),
                   jax.ShapeDtypeStruct((m, GATE_LANES), F32),
                   jax.ShapeDtypeStruct((ng, m), F32)],
        scratch_shapes=[pltpu.VMEM((tm, D_MODEL), BF16)],
        compiler_params=_params(("parallel", "arbitrary")),
        name="ml_proj",
    )(x, norm_g, w, wg, b)


def _ml_kernel(q_ref, k_ref, v_ref, o_ref, gcol_ref, grow_ref, ng_ref, x_ref, w_ref, gain_ref,
               w1f_ref, w2f_ref, out_ref, w1b_ref, w2b_ref, c_ref, n_ref, m_ref, y_ref, *, nc):
    t = pl.program_id(0)
    slot = t % 2

    @pl.when(t == 0)
    def _():
        y_ref[...] = jnp.zeros_like(y_ref)

    @pl.when(t % nc == 0)
    def _():
        c_ref[...] = jnp.zeros_like(c_ref)
        n_ref[...] = jnp.zeros_like(n_ref)
        m_ref[...] = jnp.zeros_like(m_ref)

    _cast_blocks((w1f_ref, w2f_ref), (w1b_ref, w2b_ref))

    s_idx = lax.broadcasted_iota(jnp.int32, (CHUNK, CHUNK), 0)
    n_idx = lax.broadcasted_iota(jnp.int32, (CHUNK, CHUNK), 1)
    causal = s_idx <= n_idx
    eye = s_idx == n_idx
    lower = (s_idx >= n_idx).astype(F32)
    upper = causal.astype(F32)

    gcol = gcol_ref[...]
    grow = grow_ref[...]
    hi = lax.Precision.HIGHEST
    cum_col = jnp.dot(lower, gcol, precision=hi, preferred_element_type=F32)
    cum_row = jnp.dot(grow, upper, precision=hi, preferred_element_type=F32)
    key_col = gcol - pltpu.roll(cum_col, GATE_LANES - ML_HEADS, axis=1)
    scale = ML_DQK ** -0.5
    log_scale = math.log(scale)

    heads = range(ML_HEADS)
    q = [q_ref[:, h * ML_DQK:(h + 1) * ML_DQK] for h in heads]
    k = [k_ref[:, h * ML_DQK:(h + 1) * ML_DQK] for h in heads]
    v = [v_ref[:, h * ML_DV:(h + 1) * ML_DV] for h in heads]


    raw, qn, qc = [], [], []
    for h in heads:
        raw.append(lax.dot_general(k[h], q[h], NT_DIMS, preferred_element_type=F32))
        qn.append(lax.dot_general(n_ref[h].astype(BF16), q[h], NT_DIMS,
                                  preferred_element_type=F32)[0:1, :])
        qc.append(jnp.dot(q[h], c_ref[h].astype(BF16),
                          preferred_element_type=F32).astype(BF16))

    y_prev = y_ref[1 - slot]
    pieces = []

    def out_piece():
        p = len(pieces)
        pieces.append(jnp.dot(y_prev, w_ref[:, p * OUT_PIECE:(p + 1) * OUT_PIECE],
                              preferred_element_type=F32))

    lhs_t, kw_t, w_rows, carry, m_new = [], [], [], [], []
    for h in heads:
        if h % PIECE_EVERY == 0:
            out_piece()
        brow = cum_row[ML_HEADS + h:ML_HEADS + h + 1, :]
        irow = grow[h:h + 1, :]
        kcol = key_col[:, h:h + 1]
        m_st = m_ref[h][0:1, 0:1]

        d_log = jnp.where(causal, brow + kcol, -jnp.inf)
        inter_log = brow + m_st
        m_out = jnp.maximum(inter_log, jnp.max(d_log, axis=0, keepdims=True))
        d_w = jnp.exp(d_log - (m_out - log_scale))
        inter_w = jnp.exp(inter_log - m_out)
        scores = raw[h] * d_w
        den = jnp.sum(scores, axis=0, keepdims=True) + inter_w * qn[h]
        r = 1.0 / jnp.maximum(jnp.abs(den), jnp.exp(-m_out))
        p_t = (scores * r).astype(BF16)
        diag = jnp.where(eye, inter_w * r, 0.0).astype(BF16)
        lhs_t.append(jnp.concatenate([p_t, diag], axis=0))

        b_tot = brow[:, CHUNK - 1:CHUNK]
        w_log = b_tot - brow + irow
        m_nw = jnp.maximum(b_tot + m_st, jnp.max(w_log, axis=1, keepdims=True))
        w_s = jnp.exp(w_log - m_nw) * scale
        carry.append(jnp.exp(b_tot + m_st - m_nw))
        m_new.append(m_nw)
        kw_t.append((k[h].astype(F32).T * w_s).astype(BF16))
        w_rows.append(jnp.broadcast_to(w_s, (8, CHUNK)).astype(BF16))

    h_out = []
    for h in heads:
        rhs = jnp.concatenate([v[h], qc[h]], axis=0)
        h_out.append(lax.dot_general(lhs_t[h], rhs, TN_DIMS, preferred_element_type=F32))
        c_ref[h] = carry[h] * c_ref[h] + jnp.dot(kw_t[h], v[h], preferred_element_type=F32)
        n_ref[h] = carry[h] * n_ref[h] + jnp.dot(w_rows[h], k[h], preferred_element_type=F32)
        m_ref[h] = jnp.broadcast_to(m_new[h], m_ref.shape[1:])

    for h in heads:
        if h % PIECE_EVERY == 0:
            out_piece()
        cols = slice(h * ML_DV, (h + 1) * ML_DV)
        y = h_out[h]
        yn = y * lax.rsqrt(jnp.mean(y * y, axis=-1, keepdims=True) + EPS) * ng_ref[:, cols]
        y_ref[slot, :, cols] = yn.astype(BF16) * jax.nn.sigmoid(o_ref[:, cols])

    out = jnp.concatenate(pieces, axis=1)
    out_ref[...] = x_ref[...] + _rms(out, gain_ref[...])


def ml_mix_out(proj, gcol, grow, head_g, x, w_out, layer, norm_g, g_idx, mlp_w1, mlp_w2,
               mlp_layer, batch, seq):
    assert 2 * ML_HEADS == PIECE_EVERY * (D_MODEL // OUT_PIECE)
    m = batch * seq
    n_chunks = m // CHUNK
    cur = lambda t: jnp.minimum(t, n_chunks - 1)
    prev = lambda t: jnp.maximum(t - 1, 0)
    c1 = _cast_specs(mlp_w1, mlp_layer, mlp_w1.shape[1] // CAST_BLOCKS, lambda t: t)
    c2 = _cast_specs(mlp_w2, mlp_layer, mlp_w2.shape[1] // CAST_BLOCKS, lambda t: t)
    assert CAST_BLOCKS <= n_chunks + 1
    return pl.pallas_call(
        functools.partial(_ml_kernel, nc=seq // CHUNK),
        grid=(n_chunks + 1,),
        in_specs=[pl.BlockSpec((CHUNK, ML_QK), lambda t: (cur(t), 0)),
                  pl.BlockSpec((CHUNK, ML_QK), lambda t: (cur(t), 1)),
                  pl.BlockSpec((CHUNK, ML_V), lambda t: (cur(t), 1)),
                  pl.BlockSpec((CHUNK, ML_V), lambda t: (cur(t), 2)),
                  pl.BlockSpec((CHUNK, GATE_LANES), lambda t: (cur(t), 0)),
                  pl.BlockSpec((2 * ML_HEADS, CHUNK), lambda t: (0, cur(t))),
                  pl.BlockSpec((1, ML_V), lambda t: (0, 0)),
                  pl.BlockSpec((CHUNK, D_MODEL), lambda t: (prev(t), 0)),
                  pl.BlockSpec((None, ML_V, D_MODEL), lambda t: (layer, 0, 0),
                               pipeline_mode=pl.Buffered(1)),
                  pl.BlockSpec((None, 1, D_MODEL), lambda t: (g_idx, 0, 0)),
                  c1[0], c2[0]],
        out_specs=[pl.BlockSpec((CHUNK, D_MODEL), lambda t: (prev(t), 0)), c1[1], c2[1]],
        out_shape=[jax.ShapeDtypeStruct((m, D_MODEL), F32), c1[2], c2[2]],
        scratch_shapes=[pltpu.VMEM((ML_HEADS, ML_DQK, ML_DV), F32),
                        pltpu.VMEM((ML_HEADS, 8, ML_DQK), F32),
                        pltpu.VMEM((ML_HEADS, 8, 128), F32),
                        pltpu.VMEM((2, CHUNK, ML_V), BF16)],
        compiler_params=_params(("arbitrary",)),
        name="ml_mix_out",
    )(proj, proj, proj, proj, gcol, grow, head_g, x, w_out, norm_g, mlp_w1, mlp_w2)


def _mlp_kernel(x_ref, g_in_ref, w1_ref, w2_ref, g_out_ref, o_ref, h_ref):
    f = pl.program_id(1)

    def chunk_out():
        hid = jnp.maximum(jnp.dot(h_ref[...], w1_ref[...], preferred_element_type=F32), 0.0)
        return jnp.dot((hid * hid).astype(BF16), w2_ref[...], preferred_element_type=F32)

    @pl.when(f == 0)
    def _():
        _norm_rows(h_ref, x_ref, g_in_ref)
        o_ref[...] = chunk_out()

    @pl.when(f > 0)
    def _():
        o_ref[...] += chunk_out()

    @pl.when(f == pl.num_programs(1) - 1)
    def _():
        _norm_rows(o_ref, o_ref, g_out_ref, res_ref=x_ref)


def mlp(x, norm_g, g_in_idx, g_out_idx, w1, w2, tm=1024, tf=1024):
    m = x.shape[0]
    return pl.pallas_call(
        _mlp_kernel,
        grid=(m // tm, D_FF // tf),
        in_specs=[pl.BlockSpec((tm, D_MODEL), lambda i, f: (i, 0)),
                  pl.BlockSpec((None, 1, D_MODEL), lambda i, f: (g_in_idx, 0, 0)),
                  pl.BlockSpec((D_MODEL, tf), lambda i, f: (0, f)),
                  pl.BlockSpec((tf, D_MODEL), lambda i, f: (f, 0)),
                  pl.BlockSpec((None, 1, D_MODEL), lambda i, f: (g_out_idx, 0, 0))],
        out_specs=pl.BlockSpec((tm, D_MODEL), lambda i, f: (i, 0)),
        out_shape=jax.ShapeDtypeStruct((m, D_MODEL), F32),
        scratch_shapes=[pltpu.VMEM((tm, D_MODEL), BF16)],
        compiler_params=_params(("parallel", "arbitrary")),
        name="mlp",
    )(x, norm_g, w1, w2, norm_g)


def kernel(x, positions, norm_g, ret_w_in, ret_w_out, mlstm_w_in, mlstm_b_gate,
           mlstm_norm_g, mlstm_w_out, mlp_w1, mlp_w2):
    batch, seq, d = x.shape
    m = batch * seq
    xf = x.reshape(m, d)

    inv_freq = jnp.power(ROPE_BASE, -jnp.linspace(0.0, 1.0, RET_DK // 2, dtype=F32))
    cos, sin = rope_tables(positions, inv_freq)

    gains = norm_g.astype(F32).reshape(DEPTH * 4, 1, d)
    ret_w_in_b = ret_w_in.astype(BF16)
    ret_w_out_b = ret_w_out.astype(BF16)
    ml_w_in_b = mlstm_w_in.astype(BF16)
    ml_w_out_b = mlstm_w_out.astype(BF16)
    n_gate = 2 * ML_HEADS

    for i in range(DEPTH):
        j = i // 2
        if i % 2 == 0:
            proj, w1_b, w2_b = ret_proj(xf, gains, 4 * i, ret_w_in_b, j, cos, sin,
                                        mlp_w1, mlp_w2, i)
            xf = ret_mix_out(proj, xf, ret_w_out_b, j, gains, 4 * i + 1, batch, seq)
        else:
            w_gate = ml_w_in_b[j, :, ML_MAIN:]
            wg = jnp.pad(w_gate, ((0, 0), (0, GATE_LANES - n_gate)))
            bias = mlstm_b_gate[j].astype(F32)
            b = jnp.pad(bias, (0, GATE_LANES - n_gate)).reshape(1, GATE_LANES)
            proj, gcol, grow = ml_proj(xf, gains, 4 * i, ml_w_in_b, j, wg, b)
            head_g = mlstm_norm_g[j].reshape(1, ML_V).astype(F32)
            xf, w1_b, w2_b = ml_mix_out(proj, gcol, grow, head_g, xf, ml_w_out_b, j, gains,
                                        4 * i + 1, mlp_w1, mlp_w2, i, batch, seq)
        xf = mlp(xf, gains, 4 * i + 2, 4 * i + 3, w1_b, w2_b)
    return xf.reshape(batch, seq, d)
```

```python
import functools
import math

import jax
import jax.numpy as jnp
from jax import lax
from jax.experimental import pallas as pl
from jax.experimental.pallas import tpu as pltpu

F32 = jnp.float32
BF16 = jnp.bfloat16

D_MODEL = 2048
DEPTH = 4
CHUNK = 256
RET_CHUNK = 256
EPS = 1e-6
D_FF = 4 * D_MODEL

RET_HEADS = 8
RET_DK = 256
RET_QK = RET_HEADS * RET_DK
RET_V = 2 * D_MODEL
RET_DV = RET_V // RET_HEADS
RET_IN = 2 * RET_QK + 2 * RET_V
ROPE_BASE = 10000.0

ML_HEADS = 8
ML_QK = D_MODEL // 2
ML_DQK = ML_QK // ML_HEADS
ML_V = D_MODEL
ML_DV = ML_V // ML_HEADS
ML_MAIN = 2 * ML_QK + 2 * ML_V
GATE_SOFTCAP = 15.0
GATE_LANES = 128

VMEM_LIMIT = 60 * 1024 * 1024

NT_DIMS = (((1,), (1,)), ((), ()))
TN_DIMS = (((0,), (0,)), ((), ()))


def _rms(x, g):
    ms = jnp.mean(x * x, axis=-1, keepdims=True)
    return x * lax.rsqrt(ms + EPS) * g


CAST_BLOCKS = 64
NORM_ROWS = 256
NORM_SUB = 64


def _norm_rows(dst_ref, src_ref, g_ref, res_ref=None):
    g = g_ref[...]

    def body(r, carry):
        rows = pl.ds(pl.multiple_of(r * NORM_ROWS, NORM_ROWS), NORM_ROWS)
        sq = src_ref[rows, :]
        inv = lax.rsqrt(jnp.mean(sq * sq, axis=-1, keepdims=True) + EPS)
        for t in range(NORM_ROWS // NORM_SUB):
            sub = pl.ds(pl.multiple_of(r * NORM_ROWS + t * NORM_SUB, NORM_SUB), NORM_SUB)
            v = src_ref[sub, :] * inv[t * NORM_SUB:(t + 1) * NORM_SUB] * g
            if res_ref is not None:
                v = res_ref[sub, :] + v
            dst_ref[sub, :] = v.astype(dst_ref.dtype)
        return carry

    lax.fori_loop(0, src_ref.shape[0] // NORM_ROWS, body, 0)


def _params(sem, **kw):
    return pltpu.CompilerParams(dimension_semantics=sem, vmem_limit_bytes=VMEM_LIMIT, **kw)


def _rope_kernel(pos_ref, freq_ref, cos_ref, sin_ref):
    ang = pos_ref[...].astype(F32) * freq_ref[...]
    cos_ref[...] = jnp.cos(ang)
    sin_ref[...] = jnp.sin(ang)


def rope_tables(positions, inv_freq, tm=2048):
    m = positions.size
    pos = positions.reshape(m, 1)
    half = inv_freq.shape[-1]
    return pl.pallas_call(
        _rope_kernel,
        grid=(m // tm,),
        in_specs=[pl.BlockSpec((tm, 1), lambda i: (i, 0)),
                  pl.BlockSpec((1, half), lambda i: (0, 0))],
        out_specs=[pl.BlockSpec((tm, half), lambda i: (i, 0)),
                   pl.BlockSpec((tm, half), lambda i: (i, 0))],
        out_shape=[jax.ShapeDtypeStruct((m, half), F32)] * 2,
        compiler_params=_params(("parallel",)),
        name="rope_tables",
    )(pos, inv_freq.reshape(1, half))


def _cast_blocks(src_refs, dst_refs):
    for src, dst in zip(src_refs, dst_refs):
        dst[...] = src[...].astype(BF16)


def _cast_specs(jobs, n_blocks, step_of):
    blk = lambda *ids: jnp.minimum(step_of(*ids), n_blocks - 1)
    in_specs, out_specs, out_shapes, operands = [], [], [], []
    for stack, layer in jobs:
        _, r, c = stack.shape
        rows = r // n_blocks
        in_specs.append(pl.BlockSpec((None, rows, c),
                                     lambda *ids, layer=layer: (layer, blk(*ids), 0)))
        out_specs.append(pl.BlockSpec((rows, c), lambda *ids: (blk(*ids), 0)))
        out_shapes.append(jax.ShapeDtypeStruct((r, c), BF16))
        operands.append(stack)
    return in_specs, out_specs, out_shapes, operands


def _split_refs(rest, n_cast, n_out):
    a, b = n_cast, n_cast + n_out
    return rest[:a], rest[a:b], rest[b:b + n_cast], rest[b + n_cast:]


def _ret_proj_kernel(x_ref, g_ref, w_ref, cos_ref, sin_ref, *rest, tn, n_cast):
    cast_src, (o_ref,), cast_dst, (h_ref,) = _split_refs(rest, n_cast, 1)
    j = pl.program_id(1)
    n_rot = 2 * RET_QK // tn

    @pl.when(j == 0)
    def _():
        _norm_rows(h_ref, x_ref, g_ref)

    @pl.when(j < n_rot)
    def _():
        _cast_blocks(cast_src, cast_dst)
        res = jnp.dot(h_ref[...], w_ref[...], preferred_element_type=F32)
        scale = jnp.where(j >= n_rot // 2, RET_DK ** -0.5, 1.0).astype(F32)
        c = cos_ref[...] * scale
        s = sin_ref[...] * scale
        half = RET_DK // 2
        for hh in range(tn // RET_DK):
            lo = hh * RET_DK
            t1 = res[:, lo:lo + half]
            t2 = res[:, lo + half:lo + RET_DK]
            o_ref[:, lo:lo + half] = (t1 * c - t2 * s).astype(BF16)
            o_ref[:, lo + half:lo + RET_DK] = (t2 * c + t1 * s).astype(BF16)

    @pl.when(j >= n_rot)
    def _():
        _cast_blocks(cast_src, cast_dst)
        o_ref[...] = jnp.dot(h_ref[...], w_ref[...],
                             preferred_element_type=F32).astype(BF16)


def ret_proj(x, norm_g, g_idx, w, cos, sin, cast_jobs, tm=1024, tn=2048):
    m = x.shape[0]
    nj = RET_IN // tn
    assert CAST_BLOCKS <= (m // tm) * nj
    c_in, c_out, c_shape, c_args = _cast_specs(cast_jobs, CAST_BLOCKS, lambda i, j: i * nj + j)
    return pl.pallas_call(
        functools.partial(_ret_proj_kernel, tn=tn, n_cast=len(cast_jobs)),
        grid=(m // tm, nj),
        in_specs=[pl.BlockSpec((tm, D_MODEL), lambda i, j: (i, 0)),
                  pl.BlockSpec((None, 1, D_MODEL), lambda i, j: (g_idx, 0, 0)),
                  pl.BlockSpec((D_MODEL, tn), lambda i, j: (0, j)),
                  pl.BlockSpec((tm, RET_DK // 2), lambda i, j: (i, 0)),
                  pl.BlockSpec((tm, RET_DK // 2), lambda i, j: (i, 0))] + c_in,
        out_specs=[pl.BlockSpec((tm, tn), lambda i, j: (i, j))] + c_out,
        out_shape=[jax.ShapeDtypeStruct((m, RET_IN), BF16)] + c_shape,
        scratch_shapes=[pltpu.VMEM((tm, D_MODEL), BF16)],
        compiler_params=_params(("arbitrary", "arbitrary")),
        name="ret_proj",
    )(x, norm_g, w, cos, sin, *c_args)


def _ret_log_gamma(h):
    return math.log1p(-(2.0 ** (-5.0 - h)))


OUT_PIECE = 512
PIECE_EVERY = 2 * RET_HEADS * OUT_PIECE // D_MODEL


def _ret_kernel(q_ref, k_ref, v_ref, g_ref, x_ref, w_ref, gain_ref, o_ref,
                st_ref, dec_ref, xi_ref, zeta_ref, y_ref, *, nc):
    t = pl.program_id(0)
    slot = t % 2

    @pl.when(t == 0)
    def _():
        y_ref[...] = jnp.zeros_like(y_ref)

    @pl.when(t % nc == 0)
    def _():
        st_ref[...] = jnp.zeros_like(st_ref)
        n = lax.broadcasted_iota(jnp.int32, (RET_CHUNK, RET_CHUNK), 0)
        mcol = lax.broadcasted_iota(jnp.int32, (RET_CHUNK, RET_CHUNK), 1)
        rel = (n - mcol).astype(F32)
        row = lax.broadcasted_iota(jnp.int32, (RET_CHUNK, RET_DK), 0).astype(F32)
        for h in range(RET_HEADS):
            lg = _ret_log_gamma(h)
            dec_ref[h] = jnp.where(rel >= 0, jnp.exp(jnp.maximum(rel, 0.0) * lg), 0.0)
            xi_ref[h] = jnp.exp((row + 1.0) * lg).astype(BF16)
            zeta_ref[h] = jnp.exp((RET_CHUNK - 1.0 - row) * lg).astype(BF16)

    heads = range(RET_HEADS)
    q = [q_ref[:, h * RET_DK:(h + 1) * RET_DK] for h in heads]
    k = [k_ref[:, h * RET_DK:(h + 1) * RET_DK] for h in heads]
    v = [v_ref[:, h * RET_DV:(h + 1) * RET_DV] for h in heads]


    y_prev = y_ref[1 - slot]
    pieces = []

    def out_piece():
        p = len(pieces)
        pieces.append(jnp.dot(y_prev, w_ref[:, p * OUT_PIECE:(p + 1) * OUT_PIECE],
                              preferred_element_type=F32))

    raw = [lax.dot_general(q[h], k[h], NT_DIMS, preferred_element_type=F32) for h in heads]

    lhs, rhs, kz = [], [], []
    for h in heads:
        if h % PIECE_EVERY == 0:
            out_piece()
        lhs.append(jnp.concatenate([(raw[h] * dec_ref[h]).astype(BF16), q[h] * xi_ref[h]],
                                   axis=1))
        rhs.append(jnp.concatenate([v[h], st_ref[h].astype(BF16)], axis=0))
        kz.append(k[h] * zeta_ref[h])

    y = []
    for h in heads:
        g_chunk = math.exp(RET_CHUNK * _ret_log_gamma(h))
        y.append(jnp.dot(lhs[h], rhs[h], preferred_element_type=F32))
        st_ref[h] = st_ref[h] * g_chunk + lax.dot_general(kz[h], v[h], TN_DIMS,
                                                          preferred_element_type=F32)

    for h in heads:
        if h % PIECE_EVERY == 0:
            out_piece()
        cols = slice(h * RET_DV, (h + 1) * RET_DV)
        mu = jnp.mean(y[h], axis=-1, keepdims=True)
        yc = y[h] - mu
        var = jnp.mean(yc * yc, axis=-1, keepdims=True)
        yn = yc * lax.rsqrt(var + EPS)
        gh = g_ref[:, cols]
        y_ref[slot, :, cols] = (gh * jax.nn.sigmoid(gh)) * yn.astype(BF16)

    out = jnp.concatenate(pieces, axis=1)
    o_ref[...] = x_ref[...] + _rms(out, gain_ref[...])


def ret_mix_out(proj, x, w_out, norm_g, g_idx, batch, seq):
    assert 2 * RET_HEADS == PIECE_EVERY * (D_MODEL // OUT_PIECE)
    m = batch * seq
    n_chunks = m // RET_CHUNK
    cur = lambda t: jnp.minimum(t, n_chunks - 1)
    prev = lambda t: jnp.maximum(t - 1, 0)
    return pl.pallas_call(
        functools.partial(_ret_kernel, nc=seq // RET_CHUNK),
        grid=(n_chunks + 1,),
        in_specs=[pl.BlockSpec((RET_CHUNK, RET_QK), lambda t: (cur(t), 0)),
                  pl.BlockSpec((RET_CHUNK, RET_QK), lambda t: (cur(t), 1)),
                  pl.BlockSpec((RET_CHUNK, RET_V), lambda t: (cur(t), 1)),
                  pl.BlockSpec((RET_CHUNK, RET_V), lambda t: (cur(t), 2)),
                  pl.BlockSpec((RET_CHUNK, D_MODEL), lambda t: (prev(t), 0)),
                  pl.BlockSpec((RET_V, D_MODEL), lambda t: (0, 0),
                               pipeline_mode=pl.Buffered(1)),
                  pl.BlockSpec((None, 1, D_MODEL), lambda t: (g_idx, 0, 0))],
        out_specs=pl.BlockSpec((RET_CHUNK, D_MODEL), lambda t: (prev(t), 0)),
        out_shape=jax.ShapeDtypeStruct((m, D_MODEL), F32),
        scratch_shapes=[pltpu.VMEM((RET_HEADS, RET_DK, RET_DV), F32),
                        pltpu.VMEM((RET_HEADS, RET_CHUNK, RET_CHUNK), F32),
                        pltpu.VMEM((RET_HEADS, RET_CHUNK, RET_DK), BF16),
                        pltpu.VMEM((RET_HEADS, RET_CHUNK, RET_DK), BF16),
                        pltpu.VMEM((2, RET_CHUNK, RET_V), BF16)],
        compiler_params=_params(("arbitrary",)),
        name="ret_mix_out",
    )(proj, proj, proj, proj, x, w_out, norm_g)


def _log_sigmoid(x):
    return jnp.minimum(x, 0.0) - jnp.log1p(jnp.exp(-jnp.abs(x)))


def _gate_logs(pre, is_forget):
    capped = GATE_SOFTCAP * jnp.tanh(pre / GATE_SOFTCAP)
    return jnp.where(is_forget, _log_sigmoid(capped), capped)


def _ml_proj_kernel(x_ref, g_ref, w_ref, wg_ref, b_ref, *rest, n_cast):
    cast_src, (o_ref, gcol_ref, grow_ref), cast_dst, (h_ref,) = _split_refs(rest, n_cast, 3)
    j = pl.program_id(1)

    @pl.when(j == 0)
    def _():
        _norm_rows(h_ref, x_ref, g_ref)
        pre = jnp.dot(h_ref[...], wg_ref[...], preferred_element_type=F32) + b_ref[...]
        lane = lax.broadcasted_iota(jnp.int32, pre.shape, 1)
        logs = _gate_logs(pre, lane >= ML_HEADS)
        gcol_ref[...] = logs
        grow_ref[...] = logs.T[:2 * ML_HEADS, :]

    _cast_blocks(cast_src, cast_dst)
    o_ref[...] = jnp.dot(h_ref[...], w_ref[...], preferred_element_type=F32).astype(BF16)


def ml_proj(x, norm_g, g_idx, w, layer, wg, b, cast_jobs, tm=1024, tn=2048):
    m = x.shape[0]
    ng = 2 * ML_HEADS
    nj = ML_MAIN // tn
    n_blocks = CAST_BLOCKS // 2
    assert n_blocks <= (m // tm) * nj
    c_in, c_out, c_shape, c_args = _cast_specs(cast_jobs, n_blocks, lambda i, j: i * nj + j)
    return pl.pallas_call(
        functools.partial(_ml_proj_kernel, n_cast=len(cast_jobs)),
        grid=(m // tm, nj),
        in_specs=[pl.BlockSpec((tm, D_MODEL), lambda i, j: (i, 0)),
                  pl.BlockSpec((None, 1, D_MODEL), lambda i, j: (g_idx, 0, 0)),
                  pl.BlockSpec((None, D_MODEL, tn), lambda i, j: (layer, 0, j)),
                  pl.BlockSpec((D_MODEL, GATE_LANES), lambda i, j: (0, 0)),
                  pl.BlockSpec((1, GATE_LANES), lambda i, j: (0, 0))] + c_in,
        out_specs=[pl.BlockSpec((tm, tn), lambda i, j: (i, j)),
                   pl.BlockSpec((tm, GATE_LANES), lambda i, j: (i, 0)),
                   pl.BlockSpec((ng, tm), lambda i, j: (0, i))] + c_out,
        out_shape=[jax.ShapeDtypeStruct((m, ML_MAIN), BF16),
                   jax.ShapeDtypeStruct((m, GATE_LANES), F32),
                   jax.ShapeDtypeStruct((ng, m), F32)] + c_shape,
        scratch_shapes=[pltpu.VMEM((tm, D_MODEL), BF16)],
        compiler_params=_params(("arbitrary", "arbitrary")),
        name="ml_proj",
    )(x, norm_g, w, wg, b, *c_args)


def _ml_kernel(q_ref, k_ref, v_ref, o_ref, gcol_ref, grow_ref, ng_ref, x_ref, w_ref, gain_ref,
               *rest, nc, n_cast):
    cast_src, (out_ref,), cast_dst, (c_ref, n_ref, m_ref, y_ref) = _split_refs(rest, n_cast, 1)
    t = pl.program_id(0)
    slot = t % 2

    @pl.when(t == 0)
    def _():
        y_ref[...] = jnp.zeros_like(y_ref)

    @pl.when(t % nc == 0)
    def _():
        c_ref[...] = jnp.zeros_like(c_ref)
        n_ref[...] = jnp.zeros_like(n_ref)
        m_ref[...] = jnp.zeros_like(m_ref)

    _cast_blocks(cast_src, cast_dst)

    s_idx = lax.broadcasted_iota(jnp.int32, (CHUNK, CHUNK), 0)
    n_idx = lax.broadcasted_iota(jnp.int32, (CHUNK, CHUNK), 1)
    causal = s_idx <= n_idx
    eye = s_idx == n_idx
    lower = (s_idx >= n_idx).astype(F32)
    upper = causal.astype(F32)

    gcol = gcol_ref[...]
    grow = grow_ref[...]
    hi = lax.Precision.HIGHEST
    cum_col = jnp.dot(lower, gcol, precision=hi, preferred_element_type=F32)
    cum_row = jnp.dot(grow, upper, precision=hi, preferred_element_type=F32)
    key_col = gcol - pltpu.roll(cum_col, GATE_LANES - ML_HEADS, axis=1)
    scale = ML_DQK ** -0.5
    log_scale = math.log(scale)

    heads = range(ML_HEADS)
    q = [q_ref[:, h * ML_DQK:(h + 1) * ML_DQK] for h in heads]
    k = [k_ref[:, h * ML_DQK:(h + 1) * ML_DQK] for h in heads]
    v = [v_ref[:, h * ML_DV:(h + 1) * ML_DV] for h in heads]


    raw, qn, qc = [], [], []
    for h in heads:
        raw.append(lax.dot_general(k[h], q[h], NT_DIMS, preferred_element_type=F32))
        qn.append(lax.dot_general(n_ref[h].astype(BF16), q[h], NT_DIMS,
                                  preferred_element_type=F32)[0:1, :])
        qc.append(jnp.dot(q[h], c_ref[h].astype(BF16),
                          preferred_element_type=F32).astype(BF16))

    y_prev = y_ref[1 - slot]
    pieces = []

    def out_piece():
        p = len(pieces)
        pieces.append(jnp.dot(y_prev, w_ref[:, p * OUT_PIECE:(p + 1) * OUT_PIECE],
                              preferred_element_type=F32))

    lhs_t, kw_t, w_rows, carry, m_new = [], [], [], [], []
    for h in heads:
        if h % PIECE_EVERY == 0:
            out_piece()
        brow = cum_row[ML_HEADS + h:ML_HEADS + h + 1, :]
        irow = grow[h:h + 1, :]
        kcol = key_col[:, h:h + 1]
        m_st = m_ref[h][0:1, 0:1]

        d_log = jnp.where(causal, brow + kcol, -jnp.inf)
        inter_log = brow + m_st
        m_out = jnp.maximum(inter_log, jnp.max(d_log, axis=0, keepdims=True))
        d_w = jnp.exp(d_log - (m_out - log_scale))
        inter_w = jnp.exp(inter_log - m_out)
        scores = raw[h] * d_w
        den = jnp.sum(scores, axis=0, keepdims=True) + inter_w * qn[h]
        r = 1.0 / jnp.maximum(jnp.abs(den), jnp.exp(-m_out))
        p_t = (scores * r).astype(BF16)
        diag = jnp.where(eye, inter_w * r, 0.0).astype(BF16)
        lhs_t.append(jnp.concatenate([p_t, diag], axis=0))

        b_tot = brow[:, CHUNK - 1:CHUNK]
        w_log = b_tot - brow + irow
        m_nw = jnp.maximum(b_tot + m_st, jnp.max(w_log, axis=1, keepdims=True))
        w_s = jnp.exp(w_log - m_nw) * scale
        carry.append(jnp.exp(b_tot + m_st - m_nw))
        m_new.append(m_nw)
        kw_t.append((k[h].astype(F32).T * w_s).astype(BF16))
        w_rows.append(jnp.broadcast_to(w_s, (8, CHUNK)).astype(BF16))

    h_out = []
    for h in heads:
        rhs = jnp.concatenate([v[h], qc[h]], axis=0)
        h_out.append(lax.dot_general(lhs_t[h], rhs, TN_DIMS, preferred_element_type=F32))
        c_ref[h] = carry[h] * c_ref[h] + jnp.dot(kw_t[h], v[h], preferred_element_type=F32)
        n_ref[h] = carry[h] * n_ref[h] + jnp.dot(w_rows[h], k[h], preferred_element_type=F32)
        m_ref[h] = jnp.broadcast_to(m_new[h], m_ref.shape[1:])

    for h in heads:
        if h % PIECE_EVERY == 0:
            out_piece()
        cols = slice(h * ML_DV, (h + 1) * ML_DV)
        y = h_out[h]
        yn = y * lax.rsqrt(jnp.mean(y * y, axis=-1, keepdims=True) + EPS) * ng_ref[:, cols]
        y_ref[slot, :, cols] = yn.astype(BF16) * jax.nn.sigmoid(o_ref[:, cols])

    out = jnp.concatenate(pieces, axis=1)
    out_ref[...] = x_ref[...] + _rms(out, gain_ref[...])


def ml_mix_out(proj, gcol, grow, head_g, x, w_out, norm_g, g_idx, cast_jobs, batch, seq):
    assert 2 * ML_HEADS == PIECE_EVERY * (D_MODEL // OUT_PIECE)
    m = batch * seq
    n_chunks = m // CHUNK
    cur = lambda t: jnp.minimum(t, n_chunks - 1)
    prev = lambda t: jnp.maximum(t - 1, 0)
    assert CAST_BLOCKS <= n_chunks + 1
    c_in, c_out, c_shape, c_args = _cast_specs(cast_jobs, CAST_BLOCKS, lambda t: t)
    return pl.pallas_call(
        functools.partial(_ml_kernel, nc=seq // CHUNK, n_cast=len(cast_jobs)),
        grid=(n_chunks + 1,),
        in_specs=[pl.BlockSpec((CHUNK, ML_QK), lambda t: (cur(t), 0)),
                  pl.BlockSpec((CHUNK, ML_QK), lambda t: (cur(t), 1)),
                  pl.BlockSpec((CHUNK, ML_V), lambda t: (cur(t), 1)),
                  pl.BlockSpec((CHUNK, ML_V), lambda t: (cur(t), 2)),
                  pl.BlockSpec((CHUNK, GATE_LANES), lambda t: (cur(t), 0)),
                  pl.BlockSpec((2 * ML_HEADS, CHUNK), lambda t: (0, cur(t))),
                  pl.BlockSpec((1, ML_V), lambda t: (0, 0)),
                  pl.BlockSpec((CHUNK, D_MODEL), lambda t: (prev(t), 0)),
                  pl.BlockSpec((ML_V, D_MODEL), lambda t: (0, 0),
                               pipeline_mode=pl.Buffered(1)),
                  pl.BlockSpec((None, 1, D_MODEL), lambda t: (g_idx, 0, 0))] + c_in,
        out_specs=[pl.BlockSpec((CHUNK, D_MODEL), lambda t: (prev(t), 0))] + c_out,
        out_shape=[jax.ShapeDtypeStruct((m, D_MODEL), F32)] + c_shape,
        scratch_shapes=[pltpu.VMEM((ML_HEADS, ML_DQK, ML_DV), F32),
                        pltpu.VMEM((ML_HEADS, 8, ML_DQK), F32),
                        pltpu.VMEM((ML_HEADS, 8, 128), F32),
                        pltpu.VMEM((2, CHUNK, ML_V), BF16)],
        compiler_params=_params(("arbitrary",)),
        name="ml_mix_out",
    )(proj, proj, proj, proj, gcol, grow, head_g, x, w_out, norm_g, *c_args)


def _mlp_kernel(x_ref, g_in_ref, w1_ref, w2_ref, g_out_ref, o_ref, h_ref):
    f = pl.program_id(1)

    def chunk_out():
        hid = jnp.maximum(jnp.dot(h_ref[...], w1_ref[...], preferred_element_type=F32), 0.0)
        return jnp.dot((hid * hid).astype(BF16), w2_ref[...], preferred_element_type=F32)

    @pl.when(f == 0)
    def _():
        _norm_rows(h_ref, x_ref, g_in_ref)
        o_ref[...] = chunk_out()

    @pl.when(f > 0)
    def _():
        o_ref[...] += chunk_out()

    @pl.when(f == pl.num_programs(1) - 1)
    def _():
        _norm_rows(o_ref, o_ref, g_out_ref, res_ref=x_ref)


def mlp(x, norm_g, g_in_idx, g_out_idx, w1, w2, tm=1024, tf=1024):
    m = x.shape[0]
    return pl.pallas_call(
        _mlp_kernel,
        grid=(m // tm, D_FF // tf),
        in_specs=[pl.BlockSpec((tm, D_MODEL), lambda i, f: (i, 0)),
                  pl.BlockSpec((None, 1, D_MODEL), lambda i, f: (g_in_idx, 0, 0)),
                  pl.BlockSpec((D_MODEL, tf), lambda i, f: (0, f)),
                  pl.BlockSpec((tf, D_MODEL), lambda i, f: (f, 0)),
                  pl.BlockSpec((None, 1, D_MODEL), lambda i, f: (g_out_idx, 0, 0))],
        out_specs=pl.BlockSpec((tm, D_MODEL), lambda i, f: (i, 0)),
        out_shape=jax.ShapeDtypeStruct((m, D_MODEL), F32),
        scratch_shapes=[pltpu.VMEM((tm, D_MODEL), BF16)],
        compiler_params=_params(("parallel", "arbitrary")),
        name="mlp",
    )(x, norm_g, w1, w2, norm_g)


def kernel(x, positions, norm_g, ret_w_in, ret_w_out, mlstm_w_in, mlstm_b_gate,
           mlstm_norm_g, mlstm_w_out, mlp_w1, mlp_w2):
    batch, seq, d = x.shape
    m = batch * seq
    xf = x.reshape(m, d)

    inv_freq = jnp.power(ROPE_BASE, -jnp.linspace(0.0, 1.0, RET_DK // 2, dtype=F32))
    cos, sin = rope_tables(positions, inv_freq)

    gains = norm_g.astype(F32).reshape(DEPTH * 4, 1, d)
    ml_w_in_b = mlstm_w_in.astype(BF16)
    n_gate = 2 * ML_HEADS
    ret_w_in_b = ret_w_in[0].astype(BF16)

    for i in range(DEPTH):
        j = i // 2
        if i % 2 == 0:
            proj, w1_b, w2_b, w_out_b = ret_proj(
                xf, gains, 4 * i, ret_w_in_b, cos, sin,
                [(mlp_w1, i), (mlp_w2, i), (ret_w_out, j)])
            xf = ret_mix_out(proj, xf, w_out_b, gains, 4 * i + 1, batch, seq)
        else:
            w_gate = ml_w_in_b[j, :, ML_MAIN:]
            wg = jnp.pad(w_gate, ((0, 0), (0, GATE_LANES - n_gate)))
            bias = mlstm_b_gate[j].astype(F32)
            b = jnp.pad(bias, (0, GATE_LANES - n_gate)).reshape(1, GATE_LANES)
            proj, gcol, grow, w_out_b = ml_proj(xf, gains, 4 * i, ml_w_in_b, j, wg, b,
                                                [(mlstm_w_out, j)])
            head_g = mlstm_norm_g[j].reshape(1, ML_V).astype(F32)
            jobs = [(mlp_w1, i), (mlp_w2, i)]
            if i + 1 < DEPTH:
                jobs.append((ret_w_in, j + 1))
            xf, w1_b, w2_b, *nxt = ml_mix_out(proj, gcol, grow, head_g, xf, w_out_b, gains,
                                              4 * i + 1, jobs, batch, seq)
            if nxt:
                ret_w_in_b = nxt[0]
        xf = mlp(xf, gains, 4 * i + 2, 4 * i + 3, w1_b, w2_b)
    return xf.reshape(batch, seq, d)
```

```python
import functools
import math

import jax
import jax.numpy as jnp
from jax import lax
from jax.experimental import pallas as pl
from jax.experimental.pallas import tpu as pltpu

F32 = jnp.float32
BF16 = jnp.bfloat16

D_MODEL = 2048
DEPTH = 4
CHUNK = 256
RET_CHUNK = 256
EPS = 1e-6
D_FF = 4 * D_MODEL

RET_HEADS = 8
RET_DK = 256
RET_QK = RET_HEADS * RET_DK
RET_V = 2 * D_MODEL
RET_DV = RET_V // RET_HEADS
RET_IN = 2 * RET_QK + 2 * RET_V
ROPE_BASE = 10000.0

ML_HEADS = 8
ML_QK = D_MODEL // 2
ML_DQK = ML_QK // ML_HEADS
ML_V = D_MODEL
ML_DV = ML_V // ML_HEADS
ML_MAIN = 2 * ML_QK + 2 * ML_V
GATE_SOFTCAP = 15.0
GATE_LANES = 128

VMEM_LIMIT = 60 * 1024 * 1024

NT_DIMS = (((1,), (1,)), ((), ()))
TN_DIMS = (((0,), (0,)), ((), ()))


def _rms(x, g):
    ms = jnp.mean(x * x, axis=-1, keepdims=True)
    return x * lax.rsqrt(ms + EPS) * g


CAST_BLOCKS = 64
NORM_ROWS = 256
NORM_SUB = 64


def _norm_rows(dst_ref, src_ref, g_ref, res_ref=None):
    g = g_ref[...]

    def body(r, carry):
        rows = pl.ds(pl.multiple_of(r * NORM_ROWS, NORM_ROWS), NORM_ROWS)
        sq = src_ref[rows, :]
        inv = lax.rsqrt(jnp.mean(sq * sq, axis=-1, keepdims=True) + EPS)
        for t in range(NORM_ROWS // NORM_SUB):
            sub = pl.ds(pl.multiple_of(r * NORM_ROWS + t * NORM_SUB, NORM_SUB), NORM_SUB)
            v = src_ref[sub, :] * inv[t * NORM_SUB:(t + 1) * NORM_SUB] * g
            if res_ref is not None:
                v = res_ref[sub, :] + v
            dst_ref[sub, :] = v.astype(dst_ref.dtype)
        return carry

    lax.fori_loop(0, src_ref.shape[0] // NORM_ROWS, body, 0)


def _params(sem, **kw):
    return pltpu.CompilerParams(dimension_semantics=sem, vmem_limit_bytes=VMEM_LIMIT, **kw)


def _rope_kernel(pos_ref, freq_ref, cos_ref, sin_ref):
    ang = pos_ref[...].astype(F32) * freq_ref[...]
    cos_ref[...] = jnp.cos(ang)
    sin_ref[...] = jnp.sin(ang)


def rope_tables(positions, inv_freq, tm=2048):
    m = positions.size
    pos = positions.reshape(m, 1)
    half = inv_freq.shape[-1]
    return pl.pallas_call(
        _rope_kernel,
        grid=(m // tm,),
        in_specs=[pl.BlockSpec((tm, 1), lambda i: (i, 0)),
                  pl.BlockSpec((1, half), lambda i: (0, 0))],
        out_specs=[pl.BlockSpec((tm, half), lambda i: (i, 0)),
                   pl.BlockSpec((tm, half), lambda i: (i, 0))],
        out_shape=[jax.ShapeDtypeStruct((m, half), F32)] * 2,
        compiler_params=_params(("parallel",)),
        name="rope_tables",
    )(pos, inv_freq.reshape(1, half))


def _cast_blocks(src_refs, dst_refs):
    for src, dst in zip(src_refs, dst_refs):
        dst[...] = src[...].astype(BF16)


def _cast_specs(jobs, n_blocks, step_of):
    blk = lambda *ids: jnp.minimum(step_of(*ids), n_blocks - 1)
    in_specs, out_specs, out_shapes, operands = [], [], [], []
    for stack, layer in jobs:
        _, r, c = stack.shape
        rows = r // n_blocks
        in_specs.append(pl.BlockSpec((None, rows, c),
                                     lambda *ids, layer=layer: (layer, blk(*ids), 0)))
        out_specs.append(pl.BlockSpec((rows, c), lambda *ids: (blk(*ids), 0)))
        out_shapes.append(jax.ShapeDtypeStruct((r, c), BF16))
        operands.append(stack)
    return in_specs, out_specs, out_shapes, operands


def _split_refs(rest, n_cast, n_out):
    a, b = n_cast, n_cast + n_out
    return rest[:a], rest[a:b], rest[b:b + n_cast], rest[b + n_cast:]


def _ret_proj_kernel(x_ref, g_ref, w_ref, cos_ref, sin_ref, *rest, tn, n_cast):
    cast_src, (o_ref,), cast_dst, (h_ref,) = _split_refs(rest, n_cast, 1)
    j = pl.program_id(1)
    n_rot = 2 * RET_QK // tn

    @pl.when(j == 0)
    def _():
        _norm_rows(h_ref, x_ref, g_ref)

    @pl.when(j < n_rot)
    def _():
        _cast_blocks(cast_src, cast_dst)
        res = jnp.dot(h_ref[...], w_ref[...], preferred_element_type=F32)
        scale = jnp.where(j >= n_rot // 2, RET_DK ** -0.5, 1.0).astype(F32)
        c = cos_ref[...] * scale
        s = sin_ref[...] * scale
        half = RET_DK // 2
        for hh in range(tn // RET_DK):
            lo = hh * RET_DK
            t1 = res[:, lo:lo + half]
            t2 = res[:, lo + half:lo + RET_DK]
            o_ref[:, lo:lo + half] = (t1 * c - t2 * s).astype(BF16)
            o_ref[:, lo + half:lo + RET_DK] = (t2 * c + t1 * s).astype(BF16)

    @pl.when(j >= n_rot)
    def _():
        _cast_blocks(cast_src, cast_dst)
        o_ref[...] = jnp.dot(h_ref[...], w_ref[...],
                             preferred_element_type=F32).astype(BF16)


def ret_proj(x, norm_g, g_idx, w, cos, sin, cast_jobs, tm=1024, tn=2048):
    m = x.shape[0]
    nj = RET_IN // tn
    assert CAST_BLOCKS <= (m // tm) * nj
    c_in, c_out, c_shape, c_args = _cast_specs(cast_jobs, CAST_BLOCKS, lambda i, j: i * nj + j)
    return pl.pallas_call(
        functools.partial(_ret_proj_kernel, tn=tn, n_cast=len(cast_jobs)),
        grid=(m // tm, nj),
        in_specs=[pl.BlockSpec((tm, D_MODEL), lambda i, j: (i, 0)),
                  pl.BlockSpec((None, 1, D_MODEL), lambda i, j: (g_idx, 0, 0)),
                  pl.BlockSpec((D_MODEL, tn), lambda i, j: (0, j)),
                  pl.BlockSpec((tm, RET_DK // 2), lambda i, j: (i, 0)),
                  pl.BlockSpec((tm, RET_DK // 2), lambda i, j: (i, 0))] + c_in,
        out_specs=[pl.BlockSpec((tm, tn), lambda i, j: (i, j))] + c_out,
        out_shape=[jax.ShapeDtypeStruct((m, RET_IN), BF16)] + c_shape,
        scratch_shapes=[pltpu.VMEM((tm, D_MODEL), BF16)],
        compiler_params=_params(("arbitrary", "arbitrary")),
        name="ret_proj",
    )(x, norm_g, w, cos, sin, *c_args)


def _ret_log_gamma(h):
    return math.log1p(-(2.0 ** (-5.0 - h)))


OUT_PIECE = 256
RET_PIECE_HEADS = ((0, 4), (0, 1, 2, 4, 5, 6))
ML_PIECE_HEADS = ((0, 1, 2, 4, 5, 6), (0, 4))


def _ret_kernel(q_ref, k_ref, v_ref, g_ref, x_ref, w_ref, gain_ref, o_ref,
                st_ref, dec_ref, xi_ref, zeta_ref, y_ref, *, nc):
    t = pl.program_id(0)
    slot = t % 2

    @pl.when(t == 0)
    def _():
        y_ref[...] = jnp.zeros_like(y_ref)

    @pl.when(t % nc == 0)
    def _():
        st_ref[...] = jnp.zeros_like(st_ref)
        n = lax.broadcasted_iota(jnp.int32, (RET_CHUNK, RET_CHUNK), 0)
        mcol = lax.broadcasted_iota(jnp.int32, (RET_CHUNK, RET_CHUNK), 1)
        rel = (n - mcol).astype(F32)
        row = lax.broadcasted_iota(jnp.int32, (RET_CHUNK, RET_DK), 0).astype(F32)
        for h in range(RET_HEADS):
            lg = _ret_log_gamma(h)
            dec_ref[h] = jnp.where(rel >= 0, jnp.exp(jnp.maximum(rel, 0.0) * lg), 0.0)
            xi_ref[h] = jnp.exp((row + 1.0) * lg).astype(BF16)
            zeta_ref[h] = jnp.exp((RET_CHUNK - 1.0 - row) * lg).astype(BF16)

    heads = range(RET_HEADS)
    q = [q_ref[:, h * RET_DK:(h + 1) * RET_DK] for h in heads]
    k = [k_ref[:, h * RET_DK:(h + 1) * RET_DK] for h in heads]
    v = [v_ref[:, h * RET_DV:(h + 1) * RET_DV] for h in heads]


    y_prev = y_ref[1 - slot]
    pieces = []

    def out_piece():
        p = len(pieces)
        pieces.append(jnp.dot(y_prev, w_ref[:, p * OUT_PIECE:(p + 1) * OUT_PIECE],
                              preferred_element_type=F32))

    raw = [lax.dot_general(q[h], k[h], NT_DIMS, preferred_element_type=F32) for h in heads]

    lhs, rhs, kz = [], [], []
    for h in heads:
        if h in RET_PIECE_HEADS[0]:
            out_piece()
        lhs.append(jnp.concatenate([(raw[h] * dec_ref[h]).astype(BF16), q[h] * xi_ref[h]],
                                   axis=1))
        rhs.append(jnp.concatenate([v[h], st_ref[h].astype(BF16)], axis=0))
        kz.append(k[h] * zeta_ref[h])

    y = []
    for h in heads:
        g_chunk = math.exp(RET_CHUNK * _ret_log_gamma(h))
        y.append(jnp.dot(lhs[h], rhs[h], preferred_element_type=F32))
        st_ref[h] = st_ref[h] * g_chunk + lax.dot_general(kz[h], v[h], TN_DIMS,
                                                          preferred_element_type=F32)

    for h in heads:
        if h in RET_PIECE_HEADS[1]:
            out_piece()
        cols = slice(h * RET_DV, (h + 1) * RET_DV)
        mu = jnp.mean(y[h], axis=-1, keepdims=True)
        yc = y[h] - mu
        var = jnp.mean(yc * yc, axis=-1, keepdims=True)
        yn = yc * lax.rsqrt(var + EPS)
        gh = g_ref[:, cols]
        y_ref[slot, :, cols] = (gh * jax.nn.sigmoid(gh)) * yn.astype(BF16)

    out = jnp.concatenate(pieces, axis=1)
    o_ref[...] = x_ref[...] + _rms(out, gain_ref[...])


def ret_mix_out(proj, x, w_out, norm_g, g_idx, batch, seq):
    assert sum(map(len, RET_PIECE_HEADS)) * OUT_PIECE == D_MODEL
    m = batch * seq
    n_chunks = m // RET_CHUNK
    cur = lambda t: jnp.minimum(t, n_chunks - 1)
    prev = lambda t: jnp.maximum(t - 1, 0)
    return pl.pallas_call(
        functools.partial(_ret_kernel, nc=seq // RET_CHUNK),
        grid=(n_chunks + 1,),
        in_specs=[pl.BlockSpec((RET_CHUNK, RET_QK), lambda t: (cur(t), 0)),
                  pl.BlockSpec((RET_CHUNK, RET_QK), lambda t: (cur(t), 1)),
                  pl.BlockSpec((RET_CHUNK, RET_V), lambda t: (cur(t), 1)),
                  pl.BlockSpec((RET_CHUNK, RET_V), lambda t: (cur(t), 2)),
                  pl.BlockSpec((RET_CHUNK, D_MODEL), lambda t: (prev(t), 0)),
                  pl.BlockSpec((RET_V, D_MODEL), lambda t: (0, 0),
                               pipeline_mode=pl.Buffered(1)),
                  pl.BlockSpec((None, 1, D_MODEL), lambda t: (g_idx, 0, 0))],
        out_specs=pl.BlockSpec((RET_CHUNK, D_MODEL), lambda t: (prev(t), 0)),
        out_shape=jax.ShapeDtypeStruct((m, D_MODEL), F32),
        scratch_shapes=[pltpu.VMEM((RET_HEADS, RET_DK, RET_DV), F32),
                        pltpu.VMEM((RET_HEADS, RET_CHUNK, RET_CHUNK), F32),
                        pltpu.VMEM((RET_HEADS, RET_CHUNK, RET_DK), BF16),
                        pltpu.VMEM((RET_HEADS, RET_CHUNK, RET_DK), BF16),
                        pltpu.VMEM((2, RET_CHUNK, RET_V), BF16)],
        compiler_params=_params(("arbitrary",)),
        name="ret_mix_out",
    )(proj, proj, proj, proj, x, w_out, norm_g)


def _log_sigmoid(x):
    return jnp.minimum(x, 0.0) - jnp.log1p(jnp.exp(-jnp.abs(x)))


def _gate_logs(pre, is_forget):
    capped = GATE_SOFTCAP * jnp.tanh(pre / GATE_SOFTCAP)
    return jnp.where(is_forget, _log_sigmoid(capped), capped)


def _ml_proj_kernel(x_ref, g_ref, w_ref, wg_ref, b_ref, *rest, n_cast):
    cast_src, (o_ref, gcol_ref, grow_ref), cast_dst, (h_ref,) = _split_refs(rest, n_cast, 3)
    j = pl.program_id(1)

    def main_tile():
        _cast_blocks(cast_src, cast_dst)
        o_ref[...] = jnp.dot(h_ref[...], w_ref[...], preferred_element_type=F32).astype(BF16)

    @pl.when(j == 0)
    def _():
        _norm_rows(h_ref, x_ref, g_ref)
        pre = jnp.dot(h_ref[...], wg_ref[...], preferred_element_type=F32) + b_ref[...]
        lane = lax.broadcasted_iota(jnp.int32, pre.shape, 1)
        logs = _gate_logs(pre, lane >= ML_HEADS)
        gcol_ref[...] = logs
        grow_ref[...] = logs.T[:2 * ML_HEADS, :]
        main_tile()

    @pl.when(j > 0)
    def _():
        main_tile()


def ml_proj(x, norm_g, g_idx, w, layer, wg, b, cast_jobs, tm=1024, tn=2048):
    m = x.shape[0]
    ng = 2 * ML_HEADS
    nj = ML_MAIN // tn
    n_blocks = CAST_BLOCKS // 2
    assert n_blocks <= (m // tm) * nj
    c_in, c_out, c_shape, c_args = _cast_specs(cast_jobs, n_blocks, lambda i, j: i * nj + j)
    return pl.pallas_call(
        functools.partial(_ml_proj_kernel, n_cast=len(cast_jobs)),
        grid=(m // tm, nj),
        in_specs=[pl.BlockSpec((tm, D_MODEL), lambda i, j: (i, 0)),
                  pl.BlockSpec((None, 1, D_MODEL), lambda i, j: (g_idx, 0, 0)),
                  pl.BlockSpec((None, D_MODEL, tn), lambda i, j: (layer, 0, j)),
                  pl.BlockSpec((D_MODEL, GATE_LANES), lambda i, j: (0, 0)),
                  pl.BlockSpec((1, GATE_LANES), lambda i, j: (0, 0))] + c_in,
        out_specs=[pl.BlockSpec((tm, tn), lambda i, j: (i, j)),
                   pl.BlockSpec((tm, GATE_LANES), lambda i, j: (i, 0)),
                   pl.BlockSpec((ng, tm), lambda i, j: (0, i))] + c_out,
        out_shape=[jax.ShapeDtypeStruct((m, ML_MAIN), BF16),
                   jax.ShapeDtypeStruct((m, GATE_LANES), F32),
                   jax.ShapeDtypeStruct((ng, m), F32)] + c_shape,
        scratch_shapes=[pltpu.VMEM((tm, D_MODEL), BF16)],
        compiler_params=_params(("arbitrary", "arbitrary")),
        name="ml_proj",
    )(x, norm_g, w, wg, b, *c_args)


def _ml_kernel(q_ref, k_ref, v_ref, o_ref, gcol_ref, grow_ref, ng_ref, x_ref, w_ref, gain_ref,
               *rest, nc, n_cast):
    cast_src, (out_ref,), cast_dst, (c_ref, n_ref, m_ref, y_ref) = _split_refs(rest, n_cast, 1)
    t = pl.program_id(0)
    slot = t % 2

    @pl.when(t == 0)
    def _():
        y_ref[...] = jnp.zeros_like(y_ref)

    @pl.when(t % nc == 0)
    def _():
        c_ref[...] = jnp.zeros_like(c_ref)
        n_ref[...] = jnp.zeros_like(n_ref)
        m_ref[...] = jnp.zeros_like(m_ref)

    _cast_blocks(cast_src, cast_dst)

    s_idx = lax.broadcasted_iota(jnp.int32, (CHUNK, CHUNK), 0)
    n_idx = lax.broadcasted_iota(jnp.int32, (CHUNK, CHUNK), 1)
    causal = s_idx <= n_idx
    eye = s_idx == n_idx
    lower = (s_idx >= n_idx).astype(F32)
    upper = causal.astype(F32)

    gcol = gcol_ref[...]
    grow = grow_ref[...]
    hi = lax.Precision.HIGHEST
    cum_col = jnp.dot(lower, gcol, precision=hi, preferred_element_type=F32)
    cum_row = jnp.dot(grow, upper, precision=hi, preferred_element_type=F32)
    key_col = gcol - pltpu.roll(cum_col, GATE_LANES - ML_HEADS, axis=1)
    scale = ML_DQK ** -0.5
    log_scale = math.log(scale)

    heads = range(ML_HEADS)
    q = [q_ref[:, h * ML_DQK:(h + 1) * ML_DQK] for h in heads]
    k = [k_ref[:, h * ML_DQK:(h + 1) * ML_DQK] for h in heads]
    v = [v_ref[:, h * ML_DV:(h + 1) * ML_DV] for h in heads]


    raw, qn, qc = [], [], []
    for h in heads:
        raw.append(lax.dot_general(k[h], q[h], NT_DIMS, preferred_element_type=F32))
        qn.append(lax.dot_general(n_ref[h].astype(BF16), q[h], NT_DIMS,
                                  preferred_element_type=F32)[0:1, :])
        qc.append(jnp.dot(q[h], c_ref[h].astype(BF16),
                          preferred_element_type=F32).astype(BF16))

    y_prev = y_ref[1 - slot]
    pieces = []

    def out_piece():
        p = len(pieces)
        pieces.append(jnp.dot(y_prev, w_ref[:, p * OUT_PIECE:(p + 1) * OUT_PIECE],
                              preferred_element_type=F32))

    lhs_t, kw_t, w_rows, carry, m_new = [], [], [], [], []
    for h in heads:
        if h in ML_PIECE_HEADS[0]:
            out_piece()
        brow = cum_row[ML_HEADS + h:ML_HEADS + h + 1, :]
        irow = grow[h:h + 1, :]
        kcol = key_col[:, h:h + 1]
        m_st = m_ref[h][0:1, 0:1]

        d_log = jnp.where(causal, brow + kcol, -jnp.inf)
        inter_log = brow + m_st
        m_out = jnp.maximum(inter_log, jnp.max(d_log, axis=0, keepdims=True))
        d_w = jnp.exp(d_log - (m_out - log_scale))
        inter_w = jnp.exp(inter_log - m_out)
        scores = raw[h] * d_w
        den = jnp.sum(scores, axis=0, keepdims=True) + inter_w * qn[h]
        r = 1.0 / jnp.maximum(jnp.abs(den), jnp.exp(-m_out))
        p_t = (scores * r).astype(BF16)
        diag = jnp.where(eye, inter_w * r, 0.0).astype(BF16)
        lhs_t.append(jnp.concatenate([p_t, diag], axis=0))

        b_tot = brow[:, CHUNK - 1:CHUNK]
        w_log = b_tot - brow + irow
        m_nw = jnp.maximum(b_tot + m_st, jnp.max(w_log, axis=1, keepdims=True))
        w_s = jnp.exp(w_log - m_nw) * scale
        carry.append(jnp.exp(b_tot + m_st - m_nw))
        m_new.append(m_nw)
        kw_t.append((k[h].astype(F32).T * w_s).astype(BF16))
        w_rows.append(jnp.broadcast_to(w_s, (8, CHUNK)).astype(BF16))

    h_out = []
    for h in heads:
        rhs = jnp.concatenate([v[h], qc[h]], axis=0)
        h_out.append(lax.dot_general(lhs_t[h], rhs, TN_DIMS, preferred_element_type=F32))
        c_ref[h] = carry[h] * c_ref[h] + jnp.dot(kw_t[h], v[h], preferred_element_type=F32)
        n_ref[h] = carry[h] * n_ref[h] + jnp.dot(w_rows[h], k[h], preferred_element_type=F32)
        m_ref[h] = jnp.broadcast_to(m_new[h], m_ref.shape[1:])

    for h in heads:
        if h in ML_PIECE_HEADS[1]:
            out_piece()
        cols = slice(h * ML_DV, (h + 1) * ML_DV)
        y = h_out[h]
        yn = y * lax.rsqrt(jnp.mean(y * y, axis=-1, keepdims=True) + EPS) * ng_ref[:, cols]
        y_ref[slot, :, cols] = yn.astype(BF16) * jax.nn.sigmoid(o_ref[:, cols])

    out = jnp.concatenate(pieces, axis=1)
    out_ref[...] = x_ref[...] + _rms(out, gain_ref[...])


def ml_mix_out(proj, gcol, grow, head_g, x, w_out, norm_g, g_idx, cast_jobs, batch, seq):
    assert sum(map(len, ML_PIECE_HEADS)) * OUT_PIECE == D_MODEL
    m = batch * seq
    n_chunks = m // CHUNK
    cur = lambda t: jnp.minimum(t, n_chunks - 1)
    prev = lambda t: jnp.maximum(t - 1, 0)
    assert CAST_BLOCKS <= n_chunks + 1
    c_in, c_out, c_shape, c_args = _cast_specs(cast_jobs, CAST_BLOCKS, lambda t: t)
    return pl.pallas_call(
        functools.partial(_ml_kernel, nc=seq // CHUNK, n_cast=len(cast_jobs)),
        grid=(n_chunks + 1,),
        in_specs=[pl.BlockSpec((CHUNK, ML_QK), lambda t: (cur(t), 0)),
                  pl.BlockSpec((CHUNK, ML_QK), lambda t: (cur(t), 1)),
                  pl.BlockSpec((CHUNK, ML_V), lambda t: (cur(t), 1)),
                  pl.BlockSpec((CHUNK, ML_V), lambda t: (cur(t), 2)),
                  pl.BlockSpec((CHUNK, GATE_LANES), lambda t: (cur(t), 0)),
                  pl.BlockSpec((2 * ML_HEADS, CHUNK), lambda t: (0, cur(t))),
                  pl.BlockSpec((1, ML_V), lambda t: (0, 0)),
                  pl.BlockSpec((CHUNK, D_MODEL), lambda t: (prev(t), 0)),
                  pl.BlockSpec((ML_V, D_MODEL), lambda t: (0, 0),
                               pipeline_mode=pl.Buffered(1)),
                  pl.BlockSpec((None, 1, D_MODEL), lambda t: (g_idx, 0, 0))] + c_in,
        out_specs=[pl.BlockSpec((CHUNK, D_MODEL), lambda t: (prev(t), 0))] + c_out,
        out_shape=[jax.ShapeDtypeStruct((m, D_MODEL), F32)] + c_shape,
        scratch_shapes=[pltpu.VMEM((ML_HEADS, ML_DQK, ML_DV), F32),
                        pltpu.VMEM((ML_HEADS, 8, ML_DQK), F32),
                        pltpu.VMEM((ML_HEADS, 8, 128), F32),
                        pltpu.VMEM((2, CHUNK, ML_V), BF16)],
        compiler_params=_params(("arbitrary",)),
        name="ml_mix_out",
    )(proj, proj, proj, proj, gcol, grow, head_g, x, w_out, norm_g, *c_args)


def _mlp_kernel(x_ref, g_in_ref, w1_ref, w2_ref, g_out_ref, o_ref, h_ref):
    f = pl.program_id(1)

    def chunk_out():
        hid = jnp.maximum(jnp.dot(h_ref[...], w1_ref[...], preferred_element_type=F32), 0.0)
        return jnp.dot((hid * hid).astype(BF16), w2_ref[...], preferred_element_type=F32)

    @pl.when(f == 0)
    def _():
        _norm_rows(h_ref, x_ref, g_in_ref)
        o_ref[...] = chunk_out()

    @pl.when(f > 0)
    def _():
        o_ref[...] += chunk_out()

    @pl.when(f == pl.num_programs(1) - 1)
    def _():
        _norm_rows(o_ref, o_ref, g_out_ref, res_ref=x_ref)


def mlp(x, norm_g, g_in_idx, g_out_idx, w1, w2, tm=1024, tf=1024):
    m = x.shape[0]
    return pl.pallas_call(
        _mlp_kernel,
        grid=(m // tm, D_FF // tf),
        in_specs=[pl.BlockSpec((tm, D_MODEL), lambda i, f: (i, 0)),
                  pl.BlockSpec((None, 1, D_MODEL), lambda i, f: (g_in_idx, 0, 0)),
                  pl.BlockSpec((D_MODEL, tf), lambda i, f: (0, f)),
                  pl.BlockSpec((tf, D_MODEL), lambda i, f: (f, 0)),
                  pl.BlockSpec((None, 1, D_MODEL), lambda i, f: (g_out_idx, 0, 0))],
        out_specs=pl.BlockSpec((tm, D_MODEL), lambda i, f: (i, 0)),
        out_shape=jax.ShapeDtypeStruct((m, D_MODEL), F32),
        scratch_shapes=[pltpu.VMEM((tm, D_MODEL), BF16)],
        compiler_params=_params(("parallel", "arbitrary")),
        name="mlp",
    )(x, norm_g, w1, w2, norm_g)


def kernel(x, positions, norm_g, ret_w_in, ret_w_out, mlstm_w_in, mlstm_b_gate,
           mlstm_norm_g, mlstm_w_out, mlp_w1, mlp_w2):
    batch, seq, d = x.shape
    m = batch * seq
    xf = x.reshape(m, d)

    inv_freq = jnp.power(ROPE_BASE, -jnp.linspace(0.0, 1.0, RET_DK // 2, dtype=F32))
    cos, sin = rope_tables(positions, inv_freq)

    gains = norm_g.astype(F32).reshape(DEPTH * 4, 1, d)
    ml_w_in_b = mlstm_w_in.astype(BF16)
    n_gate = 2 * ML_HEADS
    ret_w_in_b = ret_w_in[0].astype(BF16)

    for i in range(DEPTH):
        j = i // 2
        if i % 2 == 0:
            proj, w1_b, w2_b, w_out_b = ret_proj(
                xf, gains, 4 * i, ret_w_in_b, cos, sin,
                [(mlp_w1, i), (mlp_w2, i), (ret_w_out, j)])
            xf = ret_mix_out(proj, xf, w_out_b, gains, 4 * i + 1, batch, seq)
        else:
            w_gate = ml_w_in_b[j, :, ML_MAIN:]
            wg = jnp.pad(w_gate, ((0, 0), (0, GATE_LANES - n_gate)))
            bias = mlstm_b_gate[j].astype(F32)
            b = jnp.pad(bias, (0, GATE_LANES - n_gate)).reshape(1, GATE_LANES)
            proj, gcol, grow, w_out_b = ml_proj(xf, gains, 4 * i, ml_w_in_b, j, wg, b,
                                                [(mlstm_w_out, j)])
            head_g = mlstm_norm_g[j].reshape(1, ML_V).astype(F32)
            jobs = [(mlp_w1, i), (mlp_w2, i)]
            if i + 1 < DEPTH:
                jobs.append((ret_w_in, j + 1))
            xf, w1_b, w2_b, *nxt = ml_mix_out(proj, gcol, grow, head_g, xf, w_out_b, gains,
                                              4 * i + 1, jobs, batch, seq)
            if nxt:
                ret_w_in_b = nxt[0]
        xf = mlp(xf, gains, 4 * i + 2, 4 * i + 3, w1_b, w2_b)
    return xf.reshape(batch, seq, d)
```

```python
import functools
import math

import jax
import jax.numpy as jnp
from jax import lax
from jax.experimental import pallas as pl
from jax.experimental.pallas import tpu as pltpu

F32 = jnp.float32
BF16 = jnp.bfloat16

D_MODEL = 2048
DEPTH = 4
CHUNK = 256
RET_CHUNK = 256
EPS = 1e-6
D_FF = 4 * D_MODEL

RET_HEADS = 8
RET_DK = 256
RET_QK = RET_HEADS * RET_DK
RET_V = 2 * D_MODEL
RET_DV = RET_V // RET_HEADS
RET_IN = 2 * RET_QK + 2 * RET_V
ROPE_BASE = 10000.0

ML_HEADS = 8
ML_QK = D_MODEL // 2
ML_DQK = ML_QK // ML_HEADS
ML_V = D_MODEL
ML_DV = ML_V // ML_HEADS
ML_MAIN = 2 * ML_QK + 2 * ML_V
GATE_SOFTCAP = 15.0
GATE_LANES = 128

VMEM_LIMIT = 60 * 1024 * 1024

NT_DIMS = (((1,), (1,)), ((), ()))
TN_DIMS = (((0,), (0,)), ((), ()))


def _rms(x, g):
    ms = jnp.mean(x * x, axis=-1, keepdims=True)
    return x * lax.rsqrt(ms + EPS) * g


CAST_BLOCKS = 64
NORM_ROWS = 256
NORM_SUB = 64


def _norm_rows(dst_ref, src_ref, g_ref, res_ref=None):
    g = g_ref[...]

    def body(r, carry):
        rows = pl.ds(pl.multiple_of(r * NORM_ROWS, NORM_ROWS), NORM_ROWS)
        sq = src_ref[rows, :]
        inv = lax.rsqrt(jnp.mean(sq * sq, axis=-1, keepdims=True) + EPS)
        for t in range(NORM_ROWS // NORM_SUB):
            sub = pl.ds(pl.multiple_of(r * NORM_ROWS + t * NORM_SUB, NORM_SUB), NORM_SUB)
            v = src_ref[sub, :] * inv[t * NORM_SUB:(t + 1) * NORM_SUB] * g
            if res_ref is not None:
                v = res_ref[sub, :] + v
            dst_ref[sub, :] = v.astype(dst_ref.dtype)
        return carry

    lax.fori_loop(0, src_ref.shape[0] // NORM_ROWS, body, 0)


def _params(sem, **kw):
    return pltpu.CompilerParams(dimension_semantics=sem, vmem_limit_bytes=VMEM_LIMIT, **kw)


def _rope_kernel(pos_ref, freq_ref, cos_ref, sin_ref):
    ang = pos_ref[...].astype(F32) * freq_ref[...]
    cos_ref[...] = jnp.cos(ang)
    sin_ref[...] = jnp.sin(ang)


def rope_tables(positions, inv_freq, tm=2048):
    m = positions.size
    pos = positions.reshape(m, 1)
    half = inv_freq.shape[-1]
    return pl.pallas_call(
        _rope_kernel,
        grid=(m // tm,),
        in_specs=[pl.BlockSpec((tm, 1), lambda i: (i, 0)),
                  pl.BlockSpec((1, half), lambda i: (0, 0))],
        out_specs=[pl.BlockSpec((tm, half), lambda i: (i, 0)),
                   pl.BlockSpec((tm, half), lambda i: (i, 0))],
        out_shape=[jax.ShapeDtypeStruct((m, half), F32)] * 2,
        compiler_params=_params(("parallel",)),
        name="rope_tables",
    )(pos, inv_freq.reshape(1, half))


def _cast_blocks(src_refs, dst_refs):
    for src, dst in zip(src_refs, dst_refs):
        dst[...] = src[...].astype(BF16)


def _cast_specs(jobs, n_blocks, step_of):
    blk = lambda *ids: jnp.minimum(step_of(*ids), n_blocks - 1)
    in_specs, out_specs, out_shapes, operands = [], [], [], []
    for stack, layer in jobs:
        _, r, c = stack.shape
        rows = r // n_blocks
        in_specs.append(pl.BlockSpec((None, rows, c),
                                     lambda *ids, layer=layer: (layer, blk(*ids), 0)))
        out_specs.append(pl.BlockSpec((rows, c), lambda *ids: (blk(*ids), 0)))
        out_shapes.append(jax.ShapeDtypeStruct((r, c), BF16))
        operands.append(stack)
    return in_specs, out_specs, out_shapes, operands


def _split_refs(rest, n_cast, n_out):
    a, b = n_cast, n_cast + n_out
    return rest[:a], rest[a:b], rest[b:b + n_cast], rest[b + n_cast:]


def _ret_proj_kernel(x_ref, g_ref, w_ref, cos_ref, sin_ref, *rest, tn, n_cast):
    cast_src, (o_ref,), cast_dst, (h_ref,) = _split_refs(rest, n_cast, 1)
    j = pl.program_id(1)
    n_rot = 2 * RET_QK // tn

    @pl.when(j == 0)
    def _():
        _norm_rows(h_ref, x_ref, g_ref)

    @pl.when(j < n_rot)
    def _():
        _cast_blocks(cast_src, cast_dst)
        res = jnp.dot(h_ref[...], w_ref[...], preferred_element_type=F32)
        scale = jnp.where(j >= n_rot // 2, RET_DK ** -0.5, 1.0).astype(F32)
        c = cos_ref[...] * scale
        s = sin_ref[...] * scale
        half = RET_DK // 2
        for hh in range(tn // RET_DK):
            lo = hh * RET_DK
            t1 = res[:, lo:lo + half]
            t2 = res[:, lo + half:lo + RET_DK]
            o_ref[:, lo:lo + half] = (t1 * c - t2 * s).astype(BF16)
            o_ref[:, lo + half:lo + RET_DK] = (t2 * c + t1 * s).astype(BF16)

    @pl.when(j >= n_rot)
    def _():
        _cast_blocks(cast_src, cast_dst)
        o_ref[...] = jnp.dot(h_ref[...], w_ref[...],
                             preferred_element_type=F32).astype(BF16)


def ret_proj(x, norm_g, g_idx, w, cos, sin, cast_jobs, tm=1024, tn=2048):
    m = x.shape[0]
    nj = RET_IN // tn
    assert CAST_BLOCKS <= (m // tm) * nj
    c_in, c_out, c_shape, c_args = _cast_specs(cast_jobs, CAST_BLOCKS, lambda i, j: i * nj + j)
    return pl.pallas_call(
        functools.partial(_ret_proj_kernel, tn=tn, n_cast=len(cast_jobs)),
        grid=(m // tm, nj),
        in_specs=[pl.BlockSpec((tm, D_MODEL), lambda i, j: (i, 0)),
                  pl.BlockSpec((None, 1, D_MODEL), lambda i, j: (g_idx, 0, 0)),
                  pl.BlockSpec((D_MODEL, tn), lambda i, j: (0, j)),
                  pl.BlockSpec((tm, RET_DK // 2), lambda i, j: (i, 0)),
                  pl.BlockSpec((tm, RET_DK // 2), lambda i, j: (i, 0))] + c_in,
        out_specs=[pl.BlockSpec((tm, tn), lambda i, j: (i, j))] + c_out,
        out_shape=[jax.ShapeDtypeStruct((m, RET_IN), BF16)] + c_shape,
        scratch_shapes=[pltpu.VMEM((tm, D_MODEL), BF16)],
        compiler_params=_params(("arbitrary", "arbitrary")),
        name="ret_proj",
    )(x, norm_g, w, cos, sin, *c_args)


def _ret_log_gamma(h):
    return math.log1p(-(2.0 ** (-5.0 - h)))


OUT_PIECE = 256
RET_PIECE_HEADS = ((0, 4), (0, 1, 2, 4, 5, 6))
ML_PIECE_HEADS = ((0, 1, 2, 4, 5, 6), (0, 4))


def _ret_kernel(q_ref, k_ref, v_ref, g_ref, x_ref, w_ref, gain_ref, o_ref,
                st_ref, dec_ref, xi_ref, zeta_ref, y_ref, *, nc):
    t = pl.program_id(0)
    slot = t % 2

    @pl.when(t == 0)
    def _():
        y_ref[...] = jnp.zeros_like(y_ref)

    @pl.when(t % nc == 0)
    def _():
        st_ref[...] = jnp.zeros_like(st_ref)
        n = lax.broadcasted_iota(jnp.int32, (RET_CHUNK, RET_CHUNK), 0)
        mcol = lax.broadcasted_iota(jnp.int32, (RET_CHUNK, RET_CHUNK), 1)
        rel = (n - mcol).astype(F32)
        row = lax.broadcasted_iota(jnp.int32, (RET_CHUNK, RET_DK), 0).astype(F32)
        for h in range(RET_HEADS):
            lg = _ret_log_gamma(h)
            dec_ref[h] = jnp.where(rel >= 0, jnp.exp(jnp.maximum(rel, 0.0) * lg), 0.0)
            xi_ref[h] = jnp.exp((row + 1.0) * lg).astype(BF16)
            zeta_ref[h] = jnp.exp((RET_CHUNK - 1.0 - row) * lg).astype(BF16)

    heads = range(RET_HEADS)
    q = [q_ref[:, h * RET_DK:(h + 1) * RET_DK] for h in heads]
    k = [k_ref[:, h * RET_DK:(h + 1) * RET_DK] for h in heads]
    v = [v_ref[:, h * RET_DV:(h + 1) * RET_DV] for h in heads]


    y_prev = y_ref[1 - slot]
    pieces = []

    def out_piece():
        p = len(pieces)
        pieces.append(jnp.dot(y_prev, w_ref[:, p * OUT_PIECE:(p + 1) * OUT_PIECE],
                              preferred_element_type=F32))

    raw = [lax.dot_general(q[h], k[h], NT_DIMS, preferred_element_type=F32) for h in heads]

    lhs, rhs, kz = [], [], []
    for h in heads:
        if h in RET_PIECE_HEADS[0]:
            out_piece()
        lhs.append(jnp.concatenate([(raw[h] * dec_ref[h]).astype(BF16), q[h] * xi_ref[h]],
                                   axis=1))
        rhs.append(jnp.concatenate([v[h], st_ref[h].astype(BF16)], axis=0))
        kz.append(k[h] * zeta_ref[h])

    y = []
    for h in heads:
        g_chunk = math.exp(RET_CHUNK * _ret_log_gamma(h))
        y.append(jnp.dot(lhs[h], rhs[h], preferred_element_type=F32))
        st_ref[h] = st_ref[h] * g_chunk + lax.dot_general(kz[h], v[h], TN_DIMS,
                                                          preferred_element_type=F32)

    for h in heads:
        if h in RET_PIECE_HEADS[1]:
            out_piece()
        cols = slice(h * RET_DV, (h + 1) * RET_DV)
        mu = jnp.mean(y[h], axis=-1, keepdims=True)
        yc = y[h] - mu
        var = jnp.mean(yc * yc, axis=-1, keepdims=True)
        yn = yc * lax.rsqrt(var + EPS)
        gh = g_ref[:, cols]
        y_ref[slot, :, cols] = (gh * jax.nn.sigmoid(gh)) * yn.astype(BF16)

    out = jnp.concatenate(pieces, axis=1)
    o_ref[...] = x_ref[...] + _rms(out, gain_ref[...])


def ret_mix_out(proj, x, w_out, norm_g, g_idx, batch, seq):
    assert sum(map(len, RET_PIECE_HEADS)) * OUT_PIECE == D_MODEL
    m = batch * seq
    n_chunks = m // RET_CHUNK
    cur = lambda t: jnp.minimum(t, n_chunks - 1)
    prev = lambda t: jnp.maximum(t - 1, 0)
    return pl.pallas_call(
        functools.partial(_ret_kernel, nc=seq // RET_CHUNK),
        grid=(n_chunks + 1,),
        in_specs=[pl.BlockSpec((RET_CHUNK, RET_QK), lambda t: (cur(t), 0)),
                  pl.BlockSpec((RET_CHUNK, RET_QK), lambda t: (cur(t), 1)),
                  pl.BlockSpec((RET_CHUNK, RET_V), lambda t: (cur(t), 1)),
                  pl.BlockSpec((RET_CHUNK, RET_V), lambda t: (cur(t), 2)),
                  pl.BlockSpec((RET_CHUNK, D_MODEL), lambda t: (prev(t), 0)),
                  pl.BlockSpec((RET_V, D_MODEL), lambda t: (0, 0),
                               pipeline_mode=pl.Buffered(1)),
                  pl.BlockSpec((None, 1, D_MODEL), lambda t: (g_idx, 0, 0))],
        out_specs=pl.BlockSpec((RET_CHUNK, D_MODEL), lambda t: (prev(t), 0)),
        out_shape=jax.ShapeDtypeStruct((m, D_MODEL), F32),
        scratch_shapes=[pltpu.VMEM((RET_HEADS, RET_DK, RET_DV), F32),
                        pltpu.VMEM((RET_HEADS, RET_CHUNK, RET_CHUNK), F32),
                        pltpu.VMEM((RET_HEADS, RET_CHUNK, RET_DK), BF16),
                        pltpu.VMEM((RET_HEADS, RET_CHUNK, RET_DK), BF16),
                        pltpu.VMEM((2, RET_CHUNK, RET_V), BF16)],
        compiler_params=_params(("arbitrary",)),
        name="ret_mix_out",
    )(proj, proj, proj, proj, x, w_out, norm_g)


def _log_sigmoid(x):
    return jnp.minimum(x, 0.0) - jnp.log1p(jnp.exp(-jnp.abs(x)))


def _gate_logs(pre, is_forget):
    capped = GATE_SOFTCAP * jnp.tanh(pre / GATE_SOFTCAP)
    return jnp.where(is_forget, _log_sigmoid(capped), capped)


def _ml_proj_kernel(x_ref, g_ref, w_ref, wg_ref, b_ref, *rest, n_cast):
    cast_src, (o_ref, gcol_ref, grow_ref), cast_dst, (h_ref,) = _split_refs(rest, n_cast, 3)
    j = pl.program_id(1)

    def main_tile():
        _cast_blocks(cast_src, cast_dst)
        o_ref[...] = jnp.dot(h_ref[...], w_ref[...], preferred_element_type=F32).astype(BF16)

    @pl.when(j == 0)
    def _():
        _norm_rows(h_ref, x_ref, g_ref)
        pre = jnp.dot(h_ref[...], wg_ref[...], preferred_element_type=F32) + b_ref[...]
        lane = lax.broadcasted_iota(jnp.int32, pre.shape, 1)
        logs = _gate_logs(pre, lane >= ML_HEADS)
        gcol_ref[...] = logs
        grow_ref[...] = logs.T[:2 * ML_HEADS, :]
        main_tile()

    @pl.when(j > 0)
    def _():
        main_tile()


def ml_proj(x, norm_g, g_idx, w, layer, wg, b, cast_jobs, tm=1024, tn=2048):
    m = x.shape[0]
    ng = 2 * ML_HEADS
    nj = ML_MAIN // tn
    n_blocks = CAST_BLOCKS // 2
    assert n_blocks <= (m // tm) * nj
    c_in, c_out, c_shape, c_args = _cast_specs(cast_jobs, n_blocks, lambda i, j: i * nj + j)
    return pl.pallas_call(
        functools.partial(_ml_proj_kernel, n_cast=len(cast_jobs)),
        grid=(m // tm, nj),
        in_specs=[pl.BlockSpec((tm, D_MODEL), lambda i, j: (i, 0)),
                  pl.BlockSpec((None, 1, D_MODEL), lambda i, j: (g_idx, 0, 0)),
                  pl.BlockSpec((None, D_MODEL, tn), lambda i, j: (layer, 0, j)),
                  pl.BlockSpec((D_MODEL, GATE_LANES), lambda i, j: (0, 0)),
                  pl.BlockSpec((1, GATE_LANES), lambda i, j: (0, 0))] + c_in,
        out_specs=[pl.BlockSpec((tm, tn), lambda i, j: (i, j)),
                   pl.BlockSpec((tm, GATE_LANES), lambda i, j: (i, 0)),
                   pl.BlockSpec((ng, tm), lambda i, j: (0, i))] + c_out,
        out_shape=[jax.ShapeDtypeStruct((m, ML_MAIN), BF16),
                   jax.ShapeDtypeStruct((m, GATE_LANES), F32),
                   jax.ShapeDtypeStruct((ng, m), F32)] + c_shape,
        scratch_shapes=[pltpu.VMEM((tm, D_MODEL), BF16)],
        compiler_params=_params(("arbitrary", "arbitrary")),
        name="ml_proj",
    )(x, norm_g, w, wg, b, *c_args)


def _ml_kernel(q_ref, k_ref, v_ref, o_ref, gcol_ref, grow_ref, ng_ref, x_ref, w_ref, gain_ref,
               *rest, nc, n_cast):
    cast_src, (out_ref,), cast_dst, (c_ref, n_ref, m_ref, y_ref) = _split_refs(rest, n_cast, 1)
    t = pl.program_id(0)
    slot = t % 2

    @pl.when(t == 0)
    def _():
        y_ref[...] = jnp.zeros_like(y_ref)

    @pl.when(t % nc == 0)
    def _():
        c_ref[...] = jnp.zeros_like(c_ref)
        n_ref[...] = jnp.zeros_like(n_ref)
        m_ref[...] = jnp.zeros_like(m_ref)

    _cast_blocks(cast_src, cast_dst)

    s_idx = lax.broadcasted_iota(jnp.int32, (CHUNK, CHUNK), 0)
    n_idx = lax.broadcasted_iota(jnp.int32, (CHUNK, CHUNK), 1)
    causal = s_idx <= n_idx
    eye = s_idx == n_idx
    lower = (s_idx >= n_idx).astype(F32)
    upper = causal.astype(F32)

    gcol = gcol_ref[...]
    grow = grow_ref[...]
    hi = lax.Precision.HIGHEST
    cum_col = jnp.dot(lower, gcol, precision=hi, preferred_element_type=F32)
    cum_row = jnp.dot(grow, upper, precision=hi, preferred_element_type=F32)
    key_col = gcol - pltpu.roll(cum_col, GATE_LANES - ML_HEADS, axis=1)
    scale = ML_DQK ** -0.5
    log_scale = math.log(scale)

    heads = range(ML_HEADS)
    q = [q_ref[:, h * ML_DQK:(h + 1) * ML_DQK] for h in heads]
    k = [k_ref[:, h * ML_DQK:(h + 1) * ML_DQK] for h in heads]
    v = [v_ref[:, h * ML_DV:(h + 1) * ML_DV] for h in heads]


    raw, qn, qc = [], [], []
    for h in heads:
        raw.append(lax.dot_general(k[h], q[h], NT_DIMS, preferred_element_type=F32))
        qn.append(lax.dot_general(n_ref[h].astype(BF16), q[h], NT_DIMS,
                                  preferred_element_type=F32)[0:1, :])
        qc.append(jnp.dot(q[h], c_ref[h].astype(BF16),
                          preferred_element_type=F32).astype(BF16))

    y_prev = y_ref[1 - slot]
    pieces = []

    def out_piece():
        p = len(pieces)
        pieces.append(jnp.dot(y_prev, w_ref[:, p * OUT_PIECE:(p + 1) * OUT_PIECE],
                              preferred_element_type=F32))

    lhs_t, kw_t, w_rows, carry, m_new = [], [], [], [], []
    for h in heads:
        if h in ML_PIECE_HEADS[0]:
            out_piece()
        brow = cum_row[ML_HEADS + h:ML_HEADS + h + 1, :]
        irow = grow[h:h + 1, :]
        kcol = key_col[:, h:h + 1]
        m_st = m_ref[h][0:1, 0:1]

        d_log = jnp.where(causal, brow + kcol, -jnp.inf)
        inter_log = brow + m_st
        m_out = jnp.maximum(inter_log, jnp.max(d_log, axis=0, keepdims=True))
        d_w = jnp.exp(d_log - (m_out - log_scale))
        inter_w = jnp.exp(inter_log - m_out)
        scores = raw[h] * d_w
        den = jnp.sum(scores, axis=0, keepdims=True) + inter_w * qn[h]
        r = 1.0 / jnp.maximum(jnp.abs(den), jnp.exp(-m_out))
        p_t = (scores * r).astype(BF16)
        diag = jnp.where(eye, inter_w * r, 0.0).astype(BF16)
        lhs_t.append(jnp.concatenate([p_t, diag], axis=0))

        b_tot = brow[:, CHUNK - 1:CHUNK]
        w_log = b_tot - brow + irow
        m_nw = jnp.maximum(b_tot + m_st, jnp.max(w_log, axis=1, keepdims=True))
        w_s = jnp.exp(w_log - m_nw) * scale
        carry.append(jnp.exp(b_tot + m_st - m_nw))
        m_new.append(m_nw)
        kw_t.append((k[h].astype(F32).T * w_s).astype(BF16))
        w_rows.append(jnp.broadcast_to(w_s, (8, CHUNK)).astype(BF16))

    h_out = []
    for h in heads:
        rhs = jnp.concatenate([v[h], qc[h]], axis=0)
        h_out.append(lax.dot_general(lhs_t[h], rhs, TN_DIMS, preferred_element_type=F32))
        c_ref[h] = carry[h] * c_ref[h] + jnp.dot(kw_t[h], v[h], preferred_element_type=F32)
        n_ref[h] = carry[h] * n_ref[h] + jnp.dot(w_rows[h], k[h], preferred_element_type=F32)
        m_ref[h] = jnp.broadcast_to(m_new[h], m_ref.shape[1:])

    for h in heads:
        if h in ML_PIECE_HEADS[1]:
            out_piece()
        cols = slice(h * ML_DV, (h + 1) * ML_DV)
        y = h_out[h]
        yn = y * lax.rsqrt(jnp.mean(y * y, axis=-1, keepdims=True) + EPS) * ng_ref[:, cols]
        y_ref[slot, :, cols] = yn.astype(BF16) * jax.nn.sigmoid(o_ref[:, cols])

    out = jnp.concatenate(pieces, axis=1)
    out_ref[...] = x_ref[...] + _rms(out, gain_ref[...])


def ml_mix_out(proj, gcol, grow, head_g, x, w_out, norm_g, g_idx, cast_jobs, batch, seq):
    assert sum(map(len, ML_PIECE_HEADS)) * OUT_PIECE == D_MODEL
    m = batch * seq
    n_chunks = m // CHUNK
    cur = lambda t: jnp.minimum(t, n_chunks - 1)
    prev = lambda t: jnp.maximum(t - 1, 0)
    assert CAST_BLOCKS <= n_chunks + 1
    c_in, c_out, c_shape, c_args = _cast_specs(cast_jobs, CAST_BLOCKS, lambda t: t)
    return pl.pallas_call(
        functools.partial(_ml_kernel, nc=seq // CHUNK, n_cast=len(cast_jobs)),
        grid=(n_chunks + 1,),
        in_specs=[pl.BlockSpec((CHUNK, ML_QK), lambda t: (cur(t), 0)),
                  pl.BlockSpec((CHUNK, ML_QK), lambda t: (cur(t), 1)),
                  pl.BlockSpec((CHUNK, ML_V), lambda t: (cur(t), 1)),
                  pl.BlockSpec((CHUNK, ML_V), lambda t: (cur(t), 2)),
                  pl.BlockSpec((CHUNK, GATE_LANES), lambda t: (cur(t), 0)),
                  pl.BlockSpec((2 * ML_HEADS, CHUNK), lambda t: (0, cur(t))),
                  pl.BlockSpec((1, ML_V), lambda t: (0, 0)),
                  pl.BlockSpec((CHUNK, D_MODEL), lambda t: (prev(t), 0)),
                  pl.BlockSpec((ML_V, D_MODEL), lambda t: (0, 0),
                               pipeline_mode=pl.Buffered(1)),
                  pl.BlockSpec((None, 1, D_MODEL), lambda t: (g_idx, 0, 0))] + c_in,
        out_specs=[pl.BlockSpec((CHUNK, D_MODEL), lambda t: (prev(t), 0))] + c_out,
        out_shape=[jax.ShapeDtypeStruct((m, D_MODEL), F32)] + c_shape,
        scratch_shapes=[pltpu.VMEM((ML_HEADS, ML_DQK, ML_DV), F32),
                        pltpu.VMEM((ML_HEADS, 8, ML_DQK), F32),
                        pltpu.VMEM((ML_HEADS, 8, 128), F32),
                        pltpu.VMEM((2, CHUNK, ML_V), BF16)],
        compiler_params=_params(("arbitrary",)),
        name="ml_mix_out",
    )(proj, proj, proj, proj, gcol, grow, head_g, x, w_out, norm_g, *c_args)


def _mlp_kernel(xn_ref, xr_ref, g_in_ref, w1_ref, w2_ref, g_out_ref, o_ref,
                h_even, h_odd, acc_even, acc_odd, *, nt):
    i = pl.program_id(0)
    f = pl.program_id(1)
    n_rows = xn_ref.shape[0]
    rows = pl.ds(pl.multiple_of(f * n_rows, n_rows), n_rows)
    h_bufs = (h_even, h_odd)
    acc_bufs = (acc_even, acc_odd)

    def norm_rows(h_dst):
        h_dst[rows, :] = _rms(xn_ref[...], g_in_ref[...]).astype(BF16)

    def finish_rows(acc_src):
        o_ref[...] = xr_ref[...] + _rms(acc_src[rows, :], g_out_ref[...])

    def chunk(h_src, acc_dst, first):
        hid = jnp.maximum(jnp.dot(h_src[...], w1_ref[...], preferred_element_type=F32), 0.0)
        out = jnp.dot((hid * hid).astype(BF16), w2_ref[...], preferred_element_type=F32)
        if first:
            acc_dst[...] = out
        else:
            acc_dst[...] += out

    @pl.when(i == 0)
    def _():
        norm_rows(h_even)
        acc_odd[rows, :] = jnp.zeros((n_rows, acc_odd.shape[1]), F32)
        o_ref[...] = xr_ref[...]

    in_flight = jnp.logical_and(i >= 1, i <= nt)
    for parity in (0, 1):
        mine = jnp.logical_and(in_flight, (i - 1) % 2 == parity)

        @pl.when(jnp.logical_and(mine, f == 0))
        def _(parity=parity):
            finish_rows(acc_bufs[1 - parity])
            chunk(h_bufs[parity], acc_bufs[parity], True)
            norm_rows(h_bufs[1 - parity])

        @pl.when(jnp.logical_and(mine, f > 0))
        def _(parity=parity):
            finish_rows(acc_bufs[1 - parity])
            chunk(h_bufs[parity], acc_bufs[parity], False)
            norm_rows(h_bufs[1 - parity])

    @pl.when(i == nt + 1)
    def _():
        finish_rows(acc_bufs[(nt - 1) % 2])


def mlp(x, norm_g, g_in_idx, g_out_idx, w1, w2, tm=1024, tf=1024):
    m = x.shape[0]
    nt, nf = m // tm, D_FF // tf
    rows = tm // nf
    clamp = lambda t: jnp.clip(t, 0, nt - 1)
    chunk_of = lambda i, f: jnp.where(jnp.logical_and(i >= 1, i <= nt), f, 0)
    return pl.pallas_call(
        functools.partial(_mlp_kernel, nt=nt),
        grid=(nt + 2, nf),
        in_specs=[pl.BlockSpec((rows, D_MODEL), lambda i, f: (clamp(i) * nf + f, 0)),
                  pl.BlockSpec((rows, D_MODEL), lambda i, f: (clamp(i - 2) * nf + f, 0)),
                  pl.BlockSpec((None, 1, D_MODEL), lambda i, f: (g_in_idx, 0, 0)),
                  pl.BlockSpec((D_MODEL, tf), lambda i, f: (0, chunk_of(i, f))),
                  pl.BlockSpec((tf, D_MODEL), lambda i, f: (chunk_of(i, f), 0)),
                  pl.BlockSpec((None, 1, D_MODEL), lambda i, f: (g_out_idx, 0, 0))],
        out_specs=pl.BlockSpec((rows, D_MODEL), lambda i, f: (clamp(i - 2) * nf + f, 0)),
        out_shape=jax.ShapeDtypeStruct((m, D_MODEL), F32),
        scratch_shapes=[pltpu.VMEM((tm, D_MODEL), BF16), pltpu.VMEM((tm, D_MODEL), BF16),
                        pltpu.VMEM((tm, D_MODEL), F32), pltpu.VMEM((tm, D_MODEL), F32)],
        compiler_params=_params(("arbitrary", "arbitrary")),
        name="mlp",
    )(x, x, norm_g, w1, w2, norm_g)


def kernel(x, positions, norm_g, ret_w_in, ret_w_out, mlstm_w_in, mlstm_b_gate,
           mlstm_norm_g, mlstm_w_out, mlp_w1, mlp_w2):
    batch, seq, d = x.shape
    m = batch * seq
    xf = x.reshape(m, d)

    inv_freq = jnp.power(ROPE_BASE, -jnp.linspace(0.0, 1.0, RET_DK // 2, dtype=F32))
    cos, sin = rope_tables(positions, inv_freq)

    gains = norm_g.astype(F32).reshape(DEPTH * 4, 1, d)
    ml_w_in_b = mlstm_w_in.astype(BF16)
    n_gate = 2 * ML_HEADS
    ret_w_in_b = ret_w_in[0].astype(BF16)

    for i in range(DEPTH):
        j = i // 2
        if i % 2 == 0:
            proj, w1_b, w2_b, w_out_b = ret_proj(
                xf, gains, 4 * i, ret_w_in_b, cos, sin,
                [(mlp_w1, i), (mlp_w2, i), (ret_w_out, j)])
            xf = ret_mix_out(proj, xf, w_out_b, gains, 4 * i + 1, batch, seq)
        else:
            w_gate = ml_w_in_b[j, :, ML_MAIN:]
            wg = jnp.pad(w_gate, ((0, 0), (0, GATE_LANES - n_gate)))
            bias = mlstm_b_gate[j].astype(F32)
            b = jnp.pad(bias, (0, GATE_LANES - n_gate)).reshape(1, GATE_LANES)
            proj, gcol, grow, w_out_b = ml_proj(xf, gains, 4 * i, ml_w_in_b, j, wg, b,
                                                [(mlstm_w_out, j)])
            head_g = mlstm_norm_g[j].reshape(1, ML_V).astype(F32)
            jobs = [(mlp_w1, i), (mlp_w2, i)]
            if i + 1 < DEPTH:
                jobs.append((ret_w_in, j + 1))
            xf, w1_b, w2_b, *nxt = ml_mix_out(proj, gcol, grow, head_g, xf, w_out_b, gains,
                                              4 * i + 1, jobs, batch, seq)
            if nxt:
                ret_w_in_b = nxt[0]
        xf = mlp(xf, gains, 4 * i + 2, 4 * i + 3, w1_b, w2_b)
    return xf.reshape(batch, seq, d)
```

```python
import functools
import math

import jax
import jax.numpy as jnp
from jax import lax
from jax.experimental import pallas as pl
from jax.experimental.pallas import tpu as pltpu

F32 = jnp.float32
BF16 = jnp.bfloat16

D_MODEL = 2048
DEPTH = 4
CHUNK = 256
RET_CHUNK = 256
EPS = 1e-6
D_FF = 4 * D_MODEL

RET_HEADS = 8
RET_DK = 256
RET_QK = RET_HEADS * RET_DK
RET_V = 2 * D_MODEL
RET_DV = RET_V // RET_HEADS
RET_IN = 2 * RET_QK + 2 * RET_V
ROPE_BASE = 10000.0

ML_HEADS = 8
ML_QK = D_MODEL // 2
ML_DQK = ML_QK // ML_HEADS
ML_V = D_MODEL
ML_DV = ML_V // ML_HEADS
ML_MAIN = 2 * ML_QK + 2 * ML_V
GATE_SOFTCAP = 15.0
GATE_LANES = 128

VMEM_LIMIT = 60 * 1024 * 1024

NT_DIMS = (((1,), (1,)), ((), ()))
TN_DIMS = (((0,), (0,)), ((), ()))


def _rms(x, g):
    ms = jnp.mean(x * x, axis=-1, keepdims=True)
    return x * lax.rsqrt(ms + EPS) * g


CAST_BLOCKS = 64
NORM_ROWS = 256
NORM_SUB = 64


def _norm_rows(dst_ref, src_ref, g_ref, res_ref=None):
    g = g_ref[...]

    def body(r, carry):
        rows = pl.ds(pl.multiple_of(r * NORM_ROWS, NORM_ROWS), NORM_ROWS)
        sq = src_ref[rows, :]
        inv = lax.rsqrt(jnp.mean(sq * sq, axis=-1, keepdims=True) + EPS)
        for t in range(NORM_ROWS // NORM_SUB):
            sub = pl.ds(pl.multiple_of(r * NORM_ROWS + t * NORM_SUB, NORM_SUB), NORM_SUB)
            v = src_ref[sub, :] * inv[t * NORM_SUB:(t + 1) * NORM_SUB] * g
            if res_ref is not None:
                v = res_ref[sub, :] + v
            dst_ref[sub, :] = v.astype(dst_ref.dtype)
        return carry

    lax.fori_loop(0, src_ref.shape[0] // NORM_ROWS, body, 0)


def _params(sem, **kw):
    return pltpu.CompilerParams(dimension_semantics=sem, vmem_limit_bytes=VMEM_LIMIT, **kw)


def _rope_kernel(pos_ref, freq_ref, cos_ref, sin_ref):
    ang = pos_ref[...].astype(F32) * freq_ref[...]
    cos_ref[...] = jnp.cos(ang)
    sin_ref[...] = jnp.sin(ang)


def rope_tables(positions, inv_freq, tm=2048):
    m = positions.size
    pos = positions.reshape(m, 1)
    half = inv_freq.shape[-1]
    return pl.pallas_call(
        _rope_kernel,
        grid=(m // tm,),
        in_specs=[pl.BlockSpec((tm, 1), lambda i: (i, 0)),
                  pl.BlockSpec((1, half), lambda i: (0, 0))],
        out_specs=[pl.BlockSpec((tm, half), lambda i: (i, 0)),
                   pl.BlockSpec((tm, half), lambda i: (i, 0))],
        out_shape=[jax.ShapeDtypeStruct((m, half), F32)] * 2,
        compiler_params=_params(("parallel",)),
        name="rope_tables",
    )(pos, inv_freq.reshape(1, half))


def _cast_blocks(src_refs, dst_refs):
    for src, dst in zip(src_refs, dst_refs):
        dst[...] = src[...].astype(BF16)


def _cast_specs(jobs, n_blocks, step_of):
    blk = lambda *ids: jnp.minimum(step_of(*ids), n_blocks - 1)
    in_specs, out_specs, out_shapes, operands = [], [], [], []
    for stack, layer in jobs:
        _, r, c = stack.shape
        rows = r // n_blocks
        in_specs.append(pl.BlockSpec((None, rows, c),
                                     lambda *ids, layer=layer: (layer, blk(*ids), 0)))
        out_specs.append(pl.BlockSpec((rows, c), lambda *ids: (blk(*ids), 0)))
        out_shapes.append(jax.ShapeDtypeStruct((r, c), BF16))
        operands.append(stack)
    return in_specs, out_specs, out_shapes, operands


def _split_refs(rest, n_cast, n_out):
    a, b = n_cast, n_cast + n_out
    return rest[:a], rest[a:b], rest[b:b + n_cast], rest[b + n_cast:]


def _ret_proj_kernel(x_ref, g_ref, w_ref, cos_ref, sin_ref, *rest, tn, n_cast):
    cast_src, (o_ref,), cast_dst, (h_ref,) = _split_refs(rest, n_cast, 1)
    j = pl.program_id(1)
    n_rot = 2 * RET_QK // tn

    @pl.when(j == 0)
    def _():
        _norm_rows(h_ref, x_ref, g_ref)

    @pl.when(j < n_rot)
    def _():
        _cast_blocks(cast_src, cast_dst)
        res = jnp.dot(h_ref[...], w_ref[...], preferred_element_type=F32)
        scale = jnp.where(j >= n_rot // 2, RET_DK ** -0.5, 1.0).astype(F32)
        c = cos_ref[...] * scale
        s = sin_ref[...] * scale
        half = RET_DK // 2
        for hh in range(tn // RET_DK):
            lo = hh * RET_DK
            t1 = res[:, lo:lo + half]
            t2 = res[:, lo + half:lo + RET_DK]
            o_ref[:, lo:lo + half] = (t1 * c - t2 * s).astype(BF16)
            o_ref[:, lo + half:lo + RET_DK] = (t2 * c + t1 * s).astype(BF16)

    @pl.when(j >= n_rot)
    def _():
        _cast_blocks(cast_src, cast_dst)
        o_ref[...] = jnp.dot(h_ref[...], w_ref[...],
                             preferred_element_type=F32).astype(BF16)


def ret_proj(x, norm_g, g_idx, w, cos, sin, cast_jobs, tm=1024, tn=2048):
    m = x.shape[0]
    nj = RET_IN // tn
    assert CAST_BLOCKS <= (m // tm) * nj
    c_in, c_out, c_shape, c_args = _cast_specs(cast_jobs, CAST_BLOCKS, lambda i, j: i * nj + j)
    return pl.pallas_call(
        functools.partial(_ret_proj_kernel, tn=tn, n_cast=len(cast_jobs)),
        grid=(m // tm, nj),
        in_specs=[pl.BlockSpec((tm, D_MODEL), lambda i, j: (i, 0)),
                  pl.BlockSpec((None, 1, D_MODEL), lambda i, j: (g_idx, 0, 0)),
                  pl.BlockSpec((D_MODEL, tn), lambda i, j: (0, j)),
                  pl.BlockSpec((tm, RET_DK // 2), lambda i, j: (i, 0)),
                  pl.BlockSpec((tm, RET_DK // 2), lambda i, j: (i, 0))] + c_in,
        out_specs=[pl.BlockSpec((tm, tn), lambda i, j: (i, j))] + c_out,
        out_shape=[jax.ShapeDtypeStruct((m, RET_IN), BF16)] + c_shape,
        scratch_shapes=[pltpu.VMEM((tm, D_MODEL), BF16)],
        compiler_params=_params(("arbitrary", "arbitrary")),
        name="ret_proj",
    )(x, norm_g, w, cos, sin, *c_args)


def _ret_log_gamma(h):
    return math.log1p(-(2.0 ** (-5.0 - h)))


OUT_PIECE = 256
RET_PIECE_HEADS = ((0, 4), (0, 1, 2, 4, 5, 6))
ML_PIECE_HEADS = ((0, 1, 2, 4, 5, 6), (0, 4))


def _ret_kernel(p_ref, x_ref, w_ref, gain_ref, o_ref,
                st_ref, dec_ref, xi_ref, zeta_ref, y_ref, *, nc):
    t = pl.program_id(0)
    slot = t % 2

    @pl.when(t == 0)
    def _():
        y_ref[...] = jnp.zeros_like(y_ref)

    @pl.when(t % nc == 0)
    def _():
        st_ref[...] = jnp.zeros_like(st_ref)
        n = lax.broadcasted_iota(jnp.int32, (RET_CHUNK, RET_CHUNK), 0)
        mcol = lax.broadcasted_iota(jnp.int32, (RET_CHUNK, RET_CHUNK), 1)
        rel = (n - mcol).astype(F32)
        row = lax.broadcasted_iota(jnp.int32, (RET_CHUNK, RET_DK), 0).astype(F32)
        for h in range(RET_HEADS):
            lg = _ret_log_gamma(h)
            dec_ref[h] = jnp.where(rel >= 0, jnp.exp(jnp.maximum(rel, 0.0) * lg), 0.0)
            xi_ref[h] = jnp.exp((row + 1.0) * lg).astype(BF16)
            zeta_ref[h] = jnp.exp((RET_CHUNK - 1.0 - row) * lg).astype(BF16)

    heads = range(RET_HEADS)
    k0, v0, g0 = RET_QK, 2 * RET_QK, 2 * RET_QK + RET_V
    q = [p_ref[:, h * RET_DK:(h + 1) * RET_DK] for h in heads]
    k = [p_ref[:, k0 + h * RET_DK:k0 + (h + 1) * RET_DK] for h in heads]
    v = [p_ref[:, v0 + h * RET_DV:v0 + (h + 1) * RET_DV] for h in heads]


    y_prev = y_ref[1 - slot]
    pieces = []

    def out_piece():
        p = len(pieces)
        pieces.append(jnp.dot(y_prev, w_ref[:, p * OUT_PIECE:(p + 1) * OUT_PIECE],
                              preferred_element_type=F32))

    raw = [lax.dot_general(q[h], k[h], NT_DIMS, preferred_element_type=F32) for h in heads]

    lhs, rhs, kz = [], [], []
    for h in heads:
        if h in RET_PIECE_HEADS[0]:
            out_piece()
        lhs.append(jnp.concatenate([(raw[h] * dec_ref[h]).astype(BF16), q[h] * xi_ref[h]],
                                   axis=1))
        rhs.append(jnp.concatenate([v[h], st_ref[h].astype(BF16)], axis=0))
        kz.append(k[h] * zeta_ref[h])

    y = []
    for h in heads:
        g_chunk = math.exp(RET_CHUNK * _ret_log_gamma(h))
        y.append(jnp.dot(lhs[h], rhs[h], preferred_element_type=F32))
        st_ref[h] = st_ref[h] * g_chunk + lax.dot_general(kz[h], v[h], TN_DIMS,
                                                          preferred_element_type=F32)

    for h in heads:
        if h in RET_PIECE_HEADS[1]:
            out_piece()
        cols = slice(h * RET_DV, (h + 1) * RET_DV)
        mu = jnp.mean(y[h], axis=-1, keepdims=True)
        yc = y[h] - mu
        var = jnp.mean(yc * yc, axis=-1, keepdims=True)
        yn = yc * lax.rsqrt(var + EPS)
        gh = p_ref[:, g0 + h * RET_DV:g0 + (h + 1) * RET_DV]
        y_ref[slot, :, cols] = (gh * jax.nn.sigmoid(gh)) * yn.astype(BF16)

    out = jnp.concatenate(pieces, axis=1)
    o_ref[...] = x_ref[...] + _rms(out, gain_ref[...])


def ret_mix_out(proj, x, w_out, norm_g, g_idx, batch, seq):
    assert sum(map(len, RET_PIECE_HEADS)) * OUT_PIECE == D_MODEL
    m = batch * seq
    n_chunks = m // RET_CHUNK
    cur = lambda t: jnp.minimum(t, n_chunks - 1)
    prev = lambda t: jnp.maximum(t - 1, 0)
    return pl.pallas_call(
        functools.partial(_ret_kernel, nc=seq // RET_CHUNK),
        grid=(n_chunks + 1,),
        in_specs=[pl.BlockSpec((RET_CHUNK, RET_IN), lambda t: (cur(t), 0)),
                  pl.BlockSpec((RET_CHUNK, D_MODEL), lambda t: (prev(t), 0)),
                  pl.BlockSpec((RET_V, D_MODEL), lambda t: (0, 0),
                               pipeline_mode=pl.Buffered(1)),
                  pl.BlockSpec((None, 1, D_MODEL), lambda t: (g_idx, 0, 0))],
        out_specs=pl.BlockSpec((RET_CHUNK, D_MODEL), lambda t: (prev(t), 0)),
        out_shape=jax.ShapeDtypeStruct((m, D_MODEL), F32),
        scratch_shapes=[pltpu.VMEM((RET_HEADS, RET_DK, RET_DV), F32),
                        pltpu.VMEM((RET_HEADS, RET_CHUNK, RET_CHUNK), F32),
                        pltpu.VMEM((RET_HEADS, RET_CHUNK, RET_DK), BF16),
                        pltpu.VMEM((RET_HEADS, RET_CHUNK, RET_DK), BF16),
                        pltpu.VMEM((2, RET_CHUNK, RET_V), BF16)],
        compiler_params=_params(("arbitrary",)),
        name="ret_mix_out",
    )(proj, x, w_out, norm_g)


def _log_sigmoid(x):
    return jnp.minimum(x, 0.0) - jnp.log1p(jnp.exp(-jnp.abs(x)))


def _gate_logs(pre, is_forget):
    capped = GATE_SOFTCAP * jnp.tanh(pre / GATE_SOFTCAP)
    return jnp.where(is_forget, _log_sigmoid(capped), capped)


def _ml_proj_kernel(x_ref, g_ref, w_ref, wg_ref, b_ref, *rest, n_cast):
    cast_src, (o_ref, gcol_ref, grow_ref), cast_dst, (h_ref,) = _split_refs(rest, n_cast, 3)
    j = pl.program_id(1)

    def main_tile():
        _cast_blocks(cast_src, cast_dst)
        o_ref[...] = jnp.dot(h_ref[...], w_ref[...], preferred_element_type=F32).astype(BF16)

    @pl.when(j == 0)
    def _():
        _norm_rows(h_ref, x_ref, g_ref)
        pre = jnp.dot(h_ref[...], wg_ref[...], preferred_element_type=F32) + b_ref[...]
        lane = lax.broadcasted_iota(jnp.int32, pre.shape, 1)
        logs = _gate_logs(pre, lane >= ML_HEADS)
        gcol_ref[...] = logs
        grow_ref[...] = logs.T[:2 * ML_HEADS, :]
        main_tile()

    @pl.when(j > 0)
    def _():
        main_tile()


def ml_proj(x, norm_g, g_idx, w, layer, wg, b, cast_jobs, tm=1024, tn=2048):
    m = x.shape[0]
    ng = 2 * ML_HEADS
    nj = ML_MAIN // tn
    n_blocks = CAST_BLOCKS // 2
    assert n_blocks <= (m // tm) * nj
    c_in, c_out, c_shape, c_args = _cast_specs(cast_jobs, n_blocks, lambda i, j: i * nj + j)
    return pl.pallas_call(
        functools.partial(_ml_proj_kernel, n_cast=len(cast_jobs)),
        grid=(m // tm, nj),
        in_specs=[pl.BlockSpec((tm, D_MODEL), lambda i, j: (i, 0)),
                  pl.BlockSpec((None, 1, D_MODEL), lambda i, j: (g_idx, 0, 0)),
                  pl.BlockSpec((None, D_MODEL, tn), lambda i, j: (layer, 0, j)),
                  pl.BlockSpec((D_MODEL, GATE_LANES), lambda i, j: (0, 0)),
                  pl.BlockSpec((1, GATE_LANES), lambda i, j: (0, 0))] + c_in,
        out_specs=[pl.BlockSpec((tm, tn), lambda i, j: (i, j)),
                   pl.BlockSpec((tm, GATE_LANES), lambda i, j: (i, 0)),
                   pl.BlockSpec((ng, tm), lambda i, j: (0, i))] + c_out,
        out_shape=[jax.ShapeDtypeStruct((m, ML_MAIN), BF16),
                   jax.ShapeDtypeStruct((m, GATE_LANES), F32),
                   jax.ShapeDtypeStruct((ng, m), F32)] + c_shape,
        scratch_shapes=[pltpu.VMEM((tm, D_MODEL), BF16)],
        compiler_params=_params(("arbitrary", "arbitrary")),
        name="ml_proj",
    )(x, norm_g, w, wg, b, *c_args)


def _ml_kernel(p_ref, gcol_ref, grow_ref, ng_ref, x_ref, w_ref, gain_ref, *rest, nc, n_cast):
    cast_src, (out_ref,), cast_dst, (c_ref, n_ref, m_ref, y_ref) = _split_refs(rest, n_cast, 1)
    t = pl.program_id(0)
    slot = t % 2

    @pl.when(t == 0)
    def _():
        y_ref[...] = jnp.zeros_like(y_ref)

    @pl.when(t % nc == 0)
    def _():
        c_ref[...] = jnp.zeros_like(c_ref)
        n_ref[...] = jnp.zeros_like(n_ref)
        m_ref[...] = jnp.zeros_like(m_ref)

    _cast_blocks(cast_src, cast_dst)

    s_idx = lax.broadcasted_iota(jnp.int32, (CHUNK, CHUNK), 0)
    n_idx = lax.broadcasted_iota(jnp.int32, (CHUNK, CHUNK), 1)
    causal = s_idx <= n_idx
    eye = s_idx == n_idx
    lower = (s_idx >= n_idx).astype(F32)
    upper = causal.astype(F32)

    gcol = gcol_ref[...]
    grow = grow_ref[...]
    hi = lax.Precision.HIGHEST
    cum_col = jnp.dot(lower, gcol, precision=hi, preferred_element_type=F32)
    cum_row = jnp.dot(grow, upper, precision=hi, preferred_element_type=F32)
    key_col = gcol - pltpu.roll(cum_col, GATE_LANES - ML_HEADS, axis=1)
    scale = ML_DQK ** -0.5
    log_scale = math.log(scale)

    heads = range(ML_HEADS)
    k0, v0, o0 = ML_QK, 2 * ML_QK, 2 * ML_QK + ML_V
    q = [p_ref[:, h * ML_DQK:(h + 1) * ML_DQK] for h in heads]
    k = [p_ref[:, k0 + h * ML_DQK:k0 + (h + 1) * ML_DQK] for h in heads]
    v = [p_ref[:, v0 + h * ML_DV:v0 + (h + 1) * ML_DV] for h in heads]


    raw, qn, qc = [], [], []
    for h in heads:
        raw.append(lax.dot_general(k[h], q[h], NT_DIMS, preferred_element_type=F32))
        qn.append(lax.dot_general(n_ref[h].astype(BF16), q[h], NT_DIMS,
                                  preferred_element_type=F32)[0:1, :])
        qc.append(jnp.dot(q[h], c_ref[h].astype(BF16),
                          preferred_element_type=F32).astype(BF16))

    y_prev = y_ref[1 - slot]
    pieces = []

    def out_piece():
        p = len(pieces)
        pieces.append(jnp.dot(y_prev, w_ref[:, p * OUT_PIECE:(p + 1) * OUT_PIECE],
                              preferred_element_type=F32))

    lhs_t, kw_t, w_rows, carry, m_new = [], [], [], [], []
    for h in heads:
        if h in ML_PIECE_HEADS[0]:
            out_piece()
        brow = cum_row[ML_HEADS + h:ML_HEADS + h + 1, :]
        irow = grow[h:h + 1, :]
        kcol = key_col[:, h:h + 1]
        m_st = m_ref[h][0:1, 0:1]

        d_log = jnp.where(causal, brow + kcol, -jnp.inf)
        inter_log = brow + m_st
        m_out = jnp.maximum(inter_log, jnp.max(d_log, axis=0, keepdims=True))
        d_w = jnp.exp(d_log - (m_out - log_scale))
        inter_w = jnp.exp(inter_log - m_out)
        scores = raw[h] * d_w
        den = jnp.sum(scores, axis=0, keepdims=True) + inter_w * qn[h]
        r = 1.0 / jnp.maximum(jnp.abs(den), jnp.exp(-m_out))
        p_t = (scores * r).astype(BF16)
        diag = jnp.where(eye, inter_w * r, 0.0).astype(BF16)
        lhs_t.append(jnp.concatenate([p_t, diag], axis=0))

        b_tot = brow[:, CHUNK - 1:CHUNK]
        w_log = b_tot - brow + irow
        m_nw = jnp.maximum(b_tot + m_st, jnp.max(w_log, axis=1, keepdims=True))
        w_s = jnp.exp(w_log - m_nw) * scale
        carry.append(jnp.exp(b_tot + m_st - m_nw))
        m_new.append(m_nw)
        kw_t.append((k[h].astype(F32).T * w_s).astype(BF16))
        w_rows.append(jnp.broadcast_to(w_s, (8, CHUNK)).astype(BF16))

    h_out = []
    for h in heads:
        rhs = jnp.concatenate([v[h], qc[h]], axis=0)
        h_out.append(lax.dot_general(lhs_t[h], rhs, TN_DIMS, preferred_element_type=F32))
        c_ref[h] = carry[h] * c_ref[h] + jnp.dot(kw_t[h], v[h], preferred_element_type=F32)
        n_ref[h] = carry[h] * n_ref[h] + jnp.dot(w_rows[h], k[h], preferred_element_type=F32)
        m_ref[h] = jnp.broadcast_to(m_new[h], m_ref.shape[1:])

    for h in heads:
        if h in ML_PIECE_HEADS[1]:
            out_piece()
        cols = slice(h * ML_DV, (h + 1) * ML_DV)
        y = h_out[h]
        yn = y * lax.rsqrt(jnp.mean(y * y, axis=-1, keepdims=True) + EPS) * ng_ref[:, cols]
        gate = jax.nn.sigmoid(p_ref[:, o0 + h * ML_DV:o0 + (h + 1) * ML_DV])
        y_ref[slot, :, cols] = yn.astype(BF16) * gate

    out = jnp.concatenate(pieces, axis=1)
    out_ref[...] = x_ref[...] + _rms(out, gain_ref[...])


def ml_mix_out(proj, gcol, grow, head_g, x, w_out, norm_g, g_idx, cast_jobs, batch, seq):
    assert sum(map(len, ML_PIECE_HEADS)) * OUT_PIECE == D_MODEL
    m = batch * seq
    n_chunks = m // CHUNK
    cur = lambda t: jnp.minimum(t, n_chunks - 1)
    prev = lambda t: jnp.maximum(t - 1, 0)
    assert CAST_BLOCKS <= n_chunks + 1
    c_in, c_out, c_shape, c_args = _cast_specs(cast_jobs, CAST_BLOCKS, lambda t: t)
    return pl.pallas_call(
        functools.partial(_ml_kernel, nc=seq // CHUNK, n_cast=len(cast_jobs)),
        grid=(n_chunks + 1,),
        in_specs=[pl.BlockSpec((CHUNK, ML_MAIN), lambda t: (cur(t), 0)),
                  pl.BlockSpec((CHUNK, GATE_LANES), lambda t: (cur(t), 0)),
                  pl.BlockSpec((2 * ML_HEADS, CHUNK), lambda t: (0, cur(t))),
                  pl.BlockSpec((1, ML_V), lambda t: (0, 0)),
                  pl.BlockSpec((CHUNK, D_MODEL), lambda t: (prev(t), 0)),
                  pl.BlockSpec((ML_V, D_MODEL), lambda t: (0, 0),
                               pipeline_mode=pl.Buffered(1)),
                  pl.BlockSpec((None, 1, D_MODEL), lambda t: (g_idx, 0, 0))] + c_in,
        out_specs=[pl.BlockSpec((CHUNK, D_MODEL), lambda t: (prev(t), 0))] + c_out,
        out_shape=[jax.ShapeDtypeStruct((m, D_MODEL), F32)] + c_shape,
        scratch_shapes=[pltpu.VMEM((ML_HEADS, ML_DQK, ML_DV), F32),
                        pltpu.VMEM((ML_HEADS, 8, ML_DQK), F32),
                        pltpu.VMEM((ML_HEADS, 8, 128), F32),
                        pltpu.VMEM((2, CHUNK, ML_V), BF16)],
        compiler_params=_params(("arbitrary",)),
        name="ml_mix_out",
    )(proj, gcol, grow, head_g, x, w_out, norm_g, *c_args)


def _mlp_kernel(xn_ref, xr_ref, g_in_ref, w1_ref, w2_ref, g_out_ref, o_ref,
                h_even, h_odd, acc_even, acc_odd, *, nt):
    i = pl.program_id(0)
    f = pl.program_id(1)
    n_rows = xn_ref.shape[0]
    rows = pl.ds(pl.multiple_of(f * n_rows, n_rows), n_rows)
    h_bufs = (h_even, h_odd)
    acc_bufs = (acc_even, acc_odd)

    def norm_rows(h_dst):
        h_dst[rows, :] = _rms(xn_ref[...], g_in_ref[...]).astype(BF16)

    def finish_rows(acc_src):
        o_ref[...] = xr_ref[...] + _rms(acc_src[rows, :], g_out_ref[...])

    def chunk(h_src, acc_dst, first):
        hid = jnp.maximum(jnp.dot(h_src[...], w1_ref[...], preferred_element_type=F32), 0.0)
        out = jnp.dot((hid * hid).astype(BF16), w2_ref[...], preferred_element_type=F32)
        if first:
            acc_dst[...] = out
        else:
            acc_dst[...] += out

    @pl.when(i == 0)
    def _():
        norm_rows(h_even)
        acc_odd[rows, :] = jnp.zeros((n_rows, acc_odd.shape[1]), F32)
        o_ref[...] = xr_ref[...]

    in_flight = jnp.logical_and(i >= 1, i <= nt)
    for parity in (0, 1):
        mine = jnp.logical_and(in_flight, (i - 1) % 2 == parity)

        @pl.when(jnp.logical_and(mine, f == 0))
        def _(parity=parity):
            finish_rows(acc_bufs[1 - parity])
            chunk(h_bufs[parity], acc_bufs[parity], True)
            norm_rows(h_bufs[1 - parity])

        @pl.when(jnp.logical_and(mine, f > 0))
        def _(parity=parity):
            finish_rows(acc_bufs[1 - parity])
            chunk(h_bufs[parity], acc_bufs[parity], False)
            norm_rows(h_bufs[1 - parity])

    @pl.when(i == nt + 1)
    def _():
        finish_rows(acc_bufs[(nt - 1) % 2])


def mlp(x, norm_g, g_in_idx, g_out_idx, w1, w2, tm=1024, tf=1024):
    m = x.shape[0]
    nt, nf = m // tm, D_FF // tf
    rows = tm // nf
    clamp = lambda t: jnp.clip(t, 0, nt - 1)
    chunk_of = lambda i, f: jnp.where(jnp.logical_and(i >= 1, i <= nt), f, 0)
    return pl.pallas_call(
        functools.partial(_mlp_kernel, nt=nt),
        grid=(nt + 2, nf),
        in_specs=[pl.BlockSpec((rows, D_MODEL), lambda i, f: (clamp(i) * nf + f, 0)),
                  pl.BlockSpec((rows, D_MODEL), lambda i, f: (clamp(i - 2) * nf + f, 0)),
                  pl.BlockSpec((None, 1, D_MODEL), lambda i, f: (g_in_idx, 0, 0)),
                  pl.BlockSpec((D_MODEL, tf), lambda i, f: (0, chunk_of(i, f))),
                  pl.BlockSpec((tf, D_MODEL), lambda i, f: (chunk_of(i, f), 0)),
                  pl.BlockSpec((None, 1, D_MODEL), lambda i, f: (g_out_idx, 0, 0))],
        out_specs=pl.BlockSpec((rows, D_MODEL), lambda i, f: (clamp(i - 2) * nf + f, 0)),
        out_shape=jax.ShapeDtypeStruct((m, D_MODEL), F32),
        scratch_shapes=[pltpu.VMEM((tm, D_MODEL), BF16), pltpu.VMEM((tm, D_MODEL), BF16),
                        pltpu.VMEM((tm, D_MODEL), F32), pltpu.VMEM((tm, D_MODEL), F32)],
        compiler_params=_params(("arbitrary", "arbitrary")),
        name="mlp",
    )(x, x, norm_g, w1, w2, norm_g)


def kernel(x, positions, norm_g, ret_w_in, ret_w_out, mlstm_w_in, mlstm_b_gate,
           mlstm_norm_g, mlstm_w_out, mlp_w1, mlp_w2):
    batch, seq, d = x.shape
    m = batch * seq
    xf = x.reshape(m, d)

    inv_freq = jnp.power(ROPE_BASE, -jnp.linspace(0.0, 1.0, RET_DK // 2, dtype=F32))
    cos, sin = rope_tables(positions, inv_freq)

    gains = norm_g.astype(F32).reshape(DEPTH * 4, 1, d)
    ml_w_in_b = mlstm_w_in.astype(BF16)
    n_gate = 2 * ML_HEADS
    ret_w_in_b = ret_w_in[0].astype(BF16)

    for i in range(DEPTH):
        j = i // 2
        if i % 2 == 0:
            proj, w1_b, w2_b, w_out_b = ret_proj(
                xf, gains, 4 * i, ret_w_in_b, cos, sin,
                [(mlp_w1, i), (mlp_w2, i), (ret_w_out, j)])
            xf = ret_mix_out(proj, xf, w_out_b, gains, 4 * i + 1, batch, seq)
        else:
            w_gate = ml_w_in_b[j, :, ML_MAIN:]
            wg = jnp.pad(w_gate, ((0, 0), (0, GATE_LANES - n_gate)))
            bias = mlstm_b_gate[j].astype(F32)
            b = jnp.pad(bias, (0, GATE_LANES - n_gate)).reshape(1, GATE_LANES)
            proj, gcol, grow, w_out_b = ml_proj(xf, gains, 4 * i, ml_w_in_b, j, wg, b,
                                                [(mlstm_w_out, j)])
            head_g = mlstm_norm_g[j].reshape(1, ML_V).astype(F32)
            jobs = [(mlp_w1, i), (mlp_w2, i)]
            if i + 1 < DEPTH:
                jobs.append((ret_w_in, j + 1))
            xf, w1_b, w2_b, *nxt = ml_mix_out(proj, gcol, grow, head_g, xf, w_out_b, gains,
                                              4 * i + 1, jobs, batch, seq)
            if nxt:
                ret_w_in_b = nxt[0]
        xf = mlp(xf, gains, 4 * i + 2, 4 * i + 3, w1_b, w2_b)
    return xf.reshape(batch, seq, d)
```

```python
import functools
import math

import jax
import jax.numpy as jnp
from jax import lax
from jax.experimental import pallas as pl
from jax.experimental.pallas import tpu as pltpu

F32 = jnp.float32
BF16 = jnp.bfloat16

D_MODEL = 2048
DEPTH = 4
CHUNK = 256
RET_CHUNK = 256
EPS = 1e-6
D_FF = 4 * D_MODEL

RET_HEADS = 8
RET_DK = 256
RET_QK = RET_HEADS * RET_DK
RET_V = 2 * D_MODEL
RET_DV = RET_V // RET_HEADS
RET_IN = 2 * RET_QK + 2 * RET_V
ROPE_BASE = 10000.0

ML_HEADS = 8
ML_QK = D_MODEL // 2
ML_DQK = ML_QK // ML_HEADS
ML_V = D_MODEL
ML_DV = ML_V // ML_HEADS
ML_MAIN = 2 * ML_QK + 2 * ML_V
GATE_SOFTCAP = 15.0
GATE_LANES = 128

VMEM_LIMIT = 60 * 1024 * 1024

NT_DIMS = (((1,), (1,)), ((), ()))
TN_DIMS = (((0,), (0,)), ((), ()))


def _rms(x, g):
    ms = jnp.mean(x * x, axis=-1, keepdims=True)
    return x * lax.rsqrt(ms + EPS) * g


CAST_BLOCKS = 64
NORM_ROWS = 256
NORM_SUB = 64


def _norm_rows(dst_ref, src_ref, g_ref):
    g = g_ref[...]

    def body(r, carry):
        rows = pl.ds(pl.multiple_of(r * NORM_ROWS, NORM_ROWS), NORM_ROWS)
        sq = src_ref[rows, :]
        inv = lax.rsqrt(jnp.mean(sq * sq, axis=-1, keepdims=True) + EPS)
        for t in range(NORM_ROWS // NORM_SUB):
            sub = pl.ds(pl.multiple_of(r * NORM_ROWS + t * NORM_SUB, NORM_SUB), NORM_SUB)
            v = src_ref[sub, :] * inv[t * NORM_SUB:(t + 1) * NORM_SUB] * g
            dst_ref[sub, :] = v.astype(dst_ref.dtype)
        return carry

    lax.fori_loop(0, src_ref.shape[0] // NORM_ROWS, body, 0)


def _params(sem, **kw):
    return pltpu.CompilerParams(dimension_semantics=sem, vmem_limit_bytes=VMEM_LIMIT, **kw)


def _rope_kernel(pos_ref, freq_ref, cos_ref, sin_ref):
    ang = pos_ref[...].astype(F32) * freq_ref[...]
    cos_ref[...] = jnp.cos(ang)
    sin_ref[...] = jnp.sin(ang)


def rope_tables(positions, inv_freq, tm=2048):
    m = positions.size
    pos = positions.reshape(m, 1)
    half = inv_freq.shape[-1]
    return pl.pallas_call(
        _rope_kernel,
        grid=(m // tm,),
        in_specs=[pl.BlockSpec((tm, 1), lambda i: (i, 0)),
                  pl.BlockSpec((1, half), lambda i: (0, 0))],
        out_specs=[pl.BlockSpec((tm, half), lambda i: (i, 0)),
                   pl.BlockSpec((tm, half), lambda i: (i, 0))],
        out_shape=[jax.ShapeDtypeStruct((m, half), F32)] * 2,
        compiler_params=_params(("parallel",)),
        name="rope_tables",
    )(pos, inv_freq.reshape(1, half))


def _cast_blocks(src_refs, dst_refs):
    for src, dst in zip(src_refs, dst_refs):
        dst[...] = src[...].astype(BF16)


def _cast_specs(jobs, n_blocks, step_of):
    blk = lambda *ids: jnp.minimum(step_of(*ids), n_blocks - 1)
    in_specs, out_specs, out_shapes, operands = [], [], [], []
    for stack, layer in jobs:
        _, r, c = stack.shape
        rows = r // n_blocks
        in_specs.append(pl.BlockSpec((None, rows, c),
                                     lambda *ids, layer=layer: (layer, blk(*ids), 0)))
        out_specs.append(pl.BlockSpec((rows, c), lambda *ids: (blk(*ids), 0)))
        out_shapes.append(jax.ShapeDtypeStruct((r, c), BF16))
        operands.append(stack)
    return in_specs, out_specs, out_shapes, operands


def _split_refs(rest, n_cast, n_out):
    a, b = n_cast, n_cast + n_out
    return rest[:a], rest[a:b], rest[b:b + n_cast], rest[b + n_cast:]


def _ret_proj_kernel(x_ref, g_ref, w_ref, cos_ref, sin_ref, *rest, tn, n_cast):
    cast_src, (o_ref,), cast_dst, (h_ref,) = _split_refs(rest, n_cast, 1)
    j = pl.program_id(1)
    n_rot = 2 * RET_QK // tn

    @pl.when(j == 0)
    def _():
        _norm_rows(h_ref, x_ref, g_ref)

    @pl.when(j < n_rot)
    def _():
        _cast_blocks(cast_src, cast_dst)
        res = jnp.dot(h_ref[...], w_ref[...], preferred_element_type=F32)
        scale = jnp.where(j >= n_rot // 2, RET_DK ** -0.5, 1.0).astype(F32)
        c = cos_ref[...] * scale
        s = sin_ref[...] * scale
        half = RET_DK // 2
        for hh in range(tn // RET_DK):
            lo = hh * RET_DK
            t1 = res[:, lo:lo + half]
            t2 = res[:, lo + half:lo + RET_DK]
            o_ref[:, lo:lo + half] = (t1 * c - t2 * s).astype(BF16)
            o_ref[:, lo + half:lo + RET_DK] = (t2 * c + t1 * s).astype(BF16)

    @pl.when(j >= n_rot)
    def _():
        _cast_blocks(cast_src, cast_dst)
        o_ref[...] = jnp.dot(h_ref[...], w_ref[...],
                             preferred_element_type=F32).astype(BF16)


def ret_proj(x, norm_g, g_idx, w, cos, sin, cast_jobs, tm=1024, tn=2048):
    m = x.shape[0]
    nj = RET_IN // tn
    assert CAST_BLOCKS <= (m // tm) * nj
    c_in, c_out, c_shape, c_args = _cast_specs(cast_jobs, CAST_BLOCKS, lambda i, j: i * nj + j)
    return pl.pallas_call(
        functools.partial(_ret_proj_kernel, tn=tn, n_cast=len(cast_jobs)),
        grid=(m // tm, nj),
        in_specs=[pl.BlockSpec((tm, D_MODEL), lambda i, j: (i, 0)),
                  pl.BlockSpec((None, 1, D_MODEL), lambda i, j: (g_idx, 0, 0)),
                  pl.BlockSpec((D_MODEL, tn), lambda i, j: (0, j)),
                  pl.BlockSpec((tm, RET_DK // 2), lambda i, j: (i, 0)),
                  pl.BlockSpec((tm, RET_DK // 2), lambda i, j: (i, 0))] + c_in,
        out_specs=[pl.BlockSpec((tm, tn), lambda i, j: (i, j))] + c_out,
        out_shape=[jax.ShapeDtypeStruct((m, RET_IN), BF16)] + c_shape,
        scratch_shapes=[pltpu.VMEM((tm, D_MODEL), BF16)],
        compiler_params=_params(("arbitrary", "arbitrary")),
        name="ret_proj",
    )(x, norm_g, w, cos, sin, *c_args)


def _ret_log_gamma(h):
    return math.log1p(-(2.0 ** (-5.0 - h)))


OUT_PIECE = 256
RET_PIECE_HEADS = ((0, 2, 4, 6), (0, 2, 4, 6))
ML_PIECE_HEADS = ((0, 2, 4, 6), (0, 2, 4, 6))


def _ret_kernel(p_ref, x_ref, w_ref, gain_ref, o_ref,
                st_ref, dec_ref, xi_ref, zeta_ref, y_ref, *, nc):
    t = pl.program_id(0)
    slot = t % 2

    @pl.when(t == 0)
    def _():
        y_ref[...] = jnp.zeros_like(y_ref)

    @pl.when(t % nc == 0)
    def _():
        st_ref[...] = jnp.zeros_like(st_ref)
        n = lax.broadcasted_iota(jnp.int32, (RET_CHUNK, RET_CHUNK), 0)
        mcol = lax.broadcasted_iota(jnp.int32, (RET_CHUNK, RET_CHUNK), 1)
        rel = (n - mcol).astype(F32)
        row = lax.broadcasted_iota(jnp.int32, (RET_CHUNK, RET_DK), 0).astype(F32)
        for h in range(RET_HEADS):
            lg = _ret_log_gamma(h)
            dec_ref[h] = jnp.where(rel >= 0, jnp.exp(jnp.maximum(rel, 0.0) * lg), 0.0)
            xi_ref[h] = jnp.exp((row + 1.0) * lg).astype(BF16)
            zeta_ref[h] = jnp.exp((RET_CHUNK - 1.0 - row) * lg).astype(BF16)

    heads = range(RET_HEADS)
    k0, v0, g0 = RET_QK, 2 * RET_QK, 2 * RET_QK + RET_V
    q = [p_ref[:, h * RET_DK:(h + 1) * RET_DK] for h in heads]
    k = [p_ref[:, k0 + h * RET_DK:k0 + (h + 1) * RET_DK] for h in heads]
    v = [p_ref[:, v0 + h * RET_DV:v0 + (h + 1) * RET_DV] for h in heads]


    y_prev = y_ref[1 - slot]
    pieces = []

    def out_piece():
        p = len(pieces)
        pieces.append(jnp.dot(y_prev, w_ref[:, p * OUT_PIECE:(p + 1) * OUT_PIECE],
                              preferred_element_type=F32))

    raw = [lax.dot_general(q[h], k[h], NT_DIMS, preferred_element_type=F32) for h in heads]

    lhs, rhs, kz = [], [], []
    for h in heads:
        if h in RET_PIECE_HEADS[0]:
            out_piece()
        lhs.append(jnp.concatenate([(raw[h] * dec_ref[h]).astype(BF16), q[h] * xi_ref[h]],
                                   axis=1))
        rhs.append(jnp.concatenate([v[h], st_ref[h].astype(BF16)], axis=0))
        kz.append(k[h] * zeta_ref[h])

    y = []
    for h in heads:
        g_chunk = math.exp(RET_CHUNK * _ret_log_gamma(h))
        y.append(jnp.dot(lhs[h], rhs[h], preferred_element_type=F32))
        st_ref[h] = st_ref[h] * g_chunk + lax.dot_general(kz[h], v[h], TN_DIMS,
                                                          preferred_element_type=F32)

    for h in heads:
        if h in RET_PIECE_HEADS[1]:
            out_piece()
        cols = slice(h * RET_DV, (h + 1) * RET_DV)
        mu = jnp.mean(y[h], axis=-1, keepdims=True)
        yc = y[h] - mu
        var = jnp.mean(yc * yc, axis=-1, keepdims=True)
        yn = yc * lax.rsqrt(var + EPS)
        gh = p_ref[:, g0 + h * RET_DV:g0 + (h + 1) * RET_DV]
        y_ref[slot, :, cols] = (gh * jax.nn.sigmoid(gh)) * yn.astype(BF16)

    out = jnp.concatenate(pieces, axis=1)
    o_ref[...] = x_ref[...] + _rms(out, gain_ref[...])


def ret_mix_out(proj, x, w_out, norm_g, g_idx, batch, seq):
    assert sum(map(len, RET_PIECE_HEADS)) * OUT_PIECE == D_MODEL
    m = batch * seq
    n_chunks = m // RET_CHUNK
    cur = lambda t: jnp.minimum(t, n_chunks - 1)
    prev = lambda t: jnp.maximum(t - 1, 0)
    return pl.pallas_call(
        functools.partial(_ret_kernel, nc=seq // RET_CHUNK),
        grid=(n_chunks + 1,),
        in_specs=[pl.BlockSpec((RET_CHUNK, RET_IN), lambda t: (cur(t), 0)),
                  pl.BlockSpec((RET_CHUNK, D_MODEL), lambda t: (prev(t), 0)),
                  pl.BlockSpec((RET_V, D_MODEL), lambda t: (0, 0),
                               pipeline_mode=pl.Buffered(1)),
                  pl.BlockSpec((None, 1, D_MODEL), lambda t: (g_idx, 0, 0))],
        out_specs=pl.BlockSpec((RET_CHUNK, D_MODEL), lambda t: (prev(t), 0)),
        out_shape=jax.ShapeDtypeStruct((m, D_MODEL), F32),
        scratch_shapes=[pltpu.VMEM((RET_HEADS, RET_DK, RET_DV), F32),
                        pltpu.VMEM((RET_HEADS, RET_CHUNK, RET_CHUNK), F32),
                        pltpu.VMEM((RET_HEADS, RET_CHUNK, RET_DK), BF16),
                        pltpu.VMEM((RET_HEADS, RET_CHUNK, RET_DK), BF16),
                        pltpu.VMEM((2, RET_CHUNK, RET_V), BF16)],
        compiler_params=_params(("arbitrary",)),
        name="ret_mix_out",
    )(proj, x, w_out, norm_g)


def _log_sigmoid(x):
    return jnp.minimum(x, 0.0) - jnp.log1p(jnp.exp(-jnp.abs(x)))


def _gate_logs(pre, is_forget):
    capped = GATE_SOFTCAP * jnp.tanh(pre / GATE_SOFTCAP)
    return jnp.where(is_forget, _log_sigmoid(capped), capped)


def _ml_proj_kernel(x_ref, g_ref, w_ref, wg_ref, b_ref, *rest, n_cast):
    cast_src, (o_ref, gcol_ref, grow_ref), cast_dst, (h_ref,) = _split_refs(rest, n_cast, 3)
    j = pl.program_id(1)

    def main_tile():
        _cast_blocks(cast_src, cast_dst)
        o_ref[...] = jnp.dot(h_ref[...], w_ref[...], preferred_element_type=F32).astype(BF16)

    @pl.when(j == 0)
    def _():
        _norm_rows(h_ref, x_ref, g_ref)
        pre = jnp.dot(h_ref[...], wg_ref[...], preferred_element_type=F32) + b_ref[...]
        lane = lax.broadcasted_iota(jnp.int32, pre.shape, 1)
        logs = _gate_logs(pre, lane >= ML_HEADS)
        gcol_ref[...] = logs
        grow_ref[...] = logs.T[:2 * ML_HEADS, :]
        main_tile()

    @pl.when(j > 0)
    def _():
        main_tile()


def ml_proj(x, norm_g, g_idx, w, layer, wg, b, cast_jobs, tm=1024, tn=2048):
    m = x.shape[0]
    ng = 2 * ML_HEADS
    nj = ML_MAIN // tn
    n_blocks = CAST_BLOCKS // 2
    assert n_blocks <= (m // tm) * nj
    c_in, c_out, c_shape, c_args = _cast_specs(cast_jobs, n_blocks, lambda i, j: i * nj + j)
    return pl.pallas_call(
        functools.partial(_ml_proj_kernel, n_cast=len(cast_jobs)),
        grid=(m // tm, nj),
        in_specs=[pl.BlockSpec((tm, D_MODEL), lambda i, j: (i, 0)),
                  pl.BlockSpec((None, 1, D_MODEL), lambda i, j: (g_idx, 0, 0)),
                  pl.BlockSpec((None, D_MODEL, tn), lambda i, j: (layer, 0, j)),
                  pl.BlockSpec((D_MODEL, GATE_LANES), lambda i, j: (0, 0)),
                  pl.BlockSpec((1, GATE_LANES), lambda i, j: (0, 0))] + c_in,
        out_specs=[pl.BlockSpec((tm, tn), lambda i, j: (i, j)),
                   pl.BlockSpec((tm, GATE_LANES), lambda i, j: (i, 0)),
                   pl.BlockSpec((ng, tm), lambda i, j: (0, i))] + c_out,
        out_shape=[jax.ShapeDtypeStruct((m, ML_MAIN), BF16),
                   jax.ShapeDtypeStruct((m, GATE_LANES), F32),
                   jax.ShapeDtypeStruct((ng, m), F32)] + c_shape,
        scratch_shapes=[pltpu.VMEM((tm, D_MODEL), BF16)],
        compiler_params=_params(("arbitrary", "arbitrary")),
        name="ml_proj",
    )(x, norm_g, w, wg, b, *c_args)


def _ml_kernel(p_ref, gcol_ref, grow_ref, ng_ref, x_ref, w_ref, gain_ref, *rest, nc, n_cast):
    cast_src, (out_ref,), cast_dst, (c_ref, n_ref, m_ref, y_ref) = _split_refs(rest, n_cast, 1)
    t = pl.program_id(0)
    slot = t % 2

    @pl.when(t == 0)
    def _():
        y_ref[...] = jnp.zeros_like(y_ref)

    @pl.when(t % nc == 0)
    def _():
        c_ref[...] = jnp.zeros_like(c_ref)
        n_ref[...] = jnp.zeros_like(n_ref)
        m_ref[...] = jnp.zeros_like(m_ref)

    _cast_blocks(cast_src, cast_dst)

    s_idx = lax.broadcasted_iota(jnp.int32, (CHUNK, CHUNK), 0)
    n_idx = lax.broadcasted_iota(jnp.int32, (CHUNK, CHUNK), 1)
    causal = s_idx <= n_idx
    eye = s_idx == n_idx
    lower = (s_idx >= n_idx).astype(F32)
    upper = causal.astype(F32)

    gcol = gcol_ref[...]
    grow = grow_ref[...]
    hi = lax.Precision.HIGHEST
    cum_col = jnp.dot(lower, gcol, precision=hi, preferred_element_type=F32)
    cum_row = jnp.dot(grow, upper, precision=hi, preferred_element_type=F32)
    key_col = gcol - pltpu.roll(cum_col, GATE_LANES - ML_HEADS, axis=1)
    scale = ML_DQK ** -0.5
    log_scale = math.log(scale)

    heads = range(ML_HEADS)
    k0, v0, o0 = ML_QK, 2 * ML_QK, 2 * ML_QK + ML_V
    q = [p_ref[:, h * ML_DQK:(h + 1) * ML_DQK] for h in heads]
    k = [p_ref[:, k0 + h * ML_DQK:k0 + (h + 1) * ML_DQK] for h in heads]
    v = [p_ref[:, v0 + h * ML_DV:v0 + (h + 1) * ML_DV] for h in heads]


    raw, qn, qc = [], [], []
    for h in heads:
        raw.append(lax.dot_general(k[h], q[h], NT_DIMS, preferred_element_type=F32))
        qn.append(lax.dot_general(n_ref[h].astype(BF16), q[h], NT_DIMS,
                                  preferred_element_type=F32)[0:1, :])
        qc.append(jnp.dot(q[h], c_ref[h].astype(BF16),
                          preferred_element_type=F32).astype(BF16))

    y_prev = y_ref[1 - slot]
    pieces = []

    def out_piece():
        p = len(pieces)
        pieces.append(jnp.dot(y_prev, w_ref[:, p * OUT_PIECE:(p + 1) * OUT_PIECE],
                              preferred_element_type=F32))

    lhs_t, kw_t, w_rows, carry, m_new = [], [], [], [], []
    for h in heads:
        if h in ML_PIECE_HEADS[0]:
            out_piece()
        brow = cum_row[ML_HEADS + h:ML_HEADS + h + 1, :]
        irow = grow[h:h + 1, :]
        kcol = key_col[:, h:h + 1]
        m_st = m_ref[h][0:1, 0:1]

        d_log = jnp.where(causal, brow + kcol, -jnp.inf)
        inter_log = brow + m_st
        m_out = jnp.maximum(inter_log, jnp.max(d_log, axis=0, keepdims=True))
        d_w = jnp.exp(d_log - (m_out - log_scale))
        inter_w = jnp.exp(inter_log - m_out)
        scores = raw[h] * d_w
        den = jnp.sum(scores, axis=0, keepdims=True) + inter_w * qn[h]
        r = 1.0 / jnp.maximum(jnp.abs(den), jnp.exp(-m_out))
        p_t = (scores * r).astype(BF16)
        diag = jnp.where(eye, inter_w * r, 0.0).astype(BF16)
        lhs_t.append(jnp.concatenate([p_t, diag], axis=0))

        b_tot = brow[:, CHUNK - 1:CHUNK]
        w_log = b_tot - brow + irow
        m_nw = jnp.maximum(b_tot + m_st, jnp.max(w_log, axis=1, keepdims=True))
        w_s = jnp.exp(w_log - m_nw) * scale
        carry.append(jnp.exp(b_tot + m_st - m_nw))
        m_new.append(m_nw)
        kw_t.append((k[h].astype(F32).T * w_s).astype(BF16))
        w_rows.append(jnp.broadcast_to(w_s, (8, CHUNK)).astype(BF16))

    h_out = []
    for h in heads:
        rhs = jnp.concatenate([v[h], qc[h]], axis=0)
        h_out.append(lax.dot_general(lhs_t[h], rhs, TN_DIMS, preferred_element_type=F32))
        c_ref[h] = carry[h] * c_ref[h] + jnp.dot(kw_t[h], v[h], preferred_element_type=F32)
        n_ref[h] = carry[h] * n_ref[h] + jnp.dot(w_rows[h], k[h], preferred_element_type=F32)
        m_ref[h] = jnp.broadcast_to(m_new[h], m_ref.shape[1:])

    for h in heads:
        if h in ML_PIECE_HEADS[1]:
            out_piece()
        cols = slice(h * ML_DV, (h + 1) * ML_DV)
        y = h_out[h]
        yn = y * lax.rsqrt(jnp.mean(y * y, axis=-1, keepdims=True) + EPS) * ng_ref[:, cols]
        gate = jax.nn.sigmoid(p_ref[:, o0 + h * ML_DV:o0 + (h + 1) * ML_DV])
        y_ref[slot, :, cols] = yn.astype(BF16) * gate

    out = jnp.concatenate(pieces, axis=1)
    out_ref[...] = x_ref[...] + _rms(out, gain_ref[...])


def ml_mix_out(proj, gcol, grow, head_g, x, w_out, norm_g, g_idx, cast_jobs, batch, seq):
    assert sum(map(len, ML_PIECE_HEADS)) * OUT_PIECE == D_MODEL
    m = batch * seq
    n_chunks = m // CHUNK
    cur = lambda t: jnp.minimum(t, n_chunks - 1)
    prev = lambda t: jnp.maximum(t - 1, 0)
    assert CAST_BLOCKS <= n_chunks + 1
    c_in, c_out, c_shape, c_args = _cast_specs(cast_jobs, CAST_BLOCKS, lambda t: t)
    return pl.pallas_call(
        functools.partial(_ml_kernel, nc=seq // CHUNK, n_cast=len(cast_jobs)),
        grid=(n_chunks + 1,),
        in_specs=[pl.BlockSpec((CHUNK, ML_MAIN), lambda t: (cur(t), 0)),
                  pl.BlockSpec((CHUNK, GATE_LANES), lambda t: (cur(t), 0)),
                  pl.BlockSpec((2 * ML_HEADS, CHUNK), lambda t: (0, cur(t))),
                  pl.BlockSpec((1, ML_V), lambda t: (0, 0)),
                  pl.BlockSpec((CHUNK, D_MODEL), lambda t: (prev(t), 0)),
                  pl.BlockSpec((ML_V, D_MODEL), lambda t: (0, 0),
                               pipeline_mode=pl.Buffered(1)),
                  pl.BlockSpec((None, 1, D_MODEL), lambda t: (g_idx, 0, 0))] + c_in,
        out_specs=[pl.BlockSpec((CHUNK, D_MODEL), lambda t: (prev(t), 0))] + c_out,
        out_shape=[jax.ShapeDtypeStruct((m, D_MODEL), F32)] + c_shape,
        scratch_shapes=[pltpu.VMEM((ML_HEADS, ML_DQK, ML_DV), F32),
                        pltpu.VMEM((ML_HEADS, 8, ML_DQK), F32),
                        pltpu.VMEM((ML_HEADS, 8, 128), F32),
                        pltpu.VMEM((2, CHUNK, ML_V), BF16)],
        compiler_params=_params(("arbitrary",)),
        name="ml_mix_out",
    )(proj, gcol, grow, head_g, x, w_out, norm_g, *c_args)


def _mlp_kernel(xn_ref, xr_ref, g_in_ref, w1_ref, w2_ref, g_out_ref, o_ref,
                h_even, h_odd, acc_even, acc_odd, *, nt):
    i = pl.program_id(0)
    f = pl.program_id(1)
    n_rows = xn_ref.shape[0]
    rows = pl.ds(pl.multiple_of(f * n_rows, n_rows), n_rows)
    h_bufs = (h_even, h_odd)
    acc_bufs = (acc_even, acc_odd)

    def norm_rows(h_dst):
        h_dst[rows, :] = _rms(xn_ref[...], g_in_ref[...]).astype(BF16)

    def finish_rows(acc_src):
        o_ref[...] = xr_ref[...] + _rms(acc_src[rows, :], g_out_ref[...])

    def chunk(h_src, acc_dst, first):
        hid = jnp.maximum(jnp.dot(h_src[...], w1_ref[...], preferred_element_type=F32), 0.0)
        out = jnp.dot((hid * hid).astype(BF16), w2_ref[...], preferred_element_type=F32)
        if first:
            acc_dst[...] = out
        else:
            acc_dst[...] += out

    @pl.when(i == 0)
    def _():
        norm_rows(h_even)
        acc_odd[rows, :] = jnp.zeros((n_rows, acc_odd.shape[1]), F32)
        o_ref[...] = xr_ref[...]

    in_flight = jnp.logical_and(i >= 1, i <= nt)
    for parity in (0, 1):
        mine = jnp.logical_and(in_flight, (i - 1) % 2 == parity)

        @pl.when(jnp.logical_and(mine, f == 0))
        def _(parity=parity):
            finish_rows(acc_bufs[1 - parity])
            chunk(h_bufs[parity], acc_bufs[parity], True)
            norm_rows(h_bufs[1 - parity])

        @pl.when(jnp.logical_and(mine, f > 0))
        def _(parity=parity):
            finish_rows(acc_bufs[1 - parity])
            chunk(h_bufs[parity], acc_bufs[parity], False)
            norm_rows(h_bufs[1 - parity])

    @pl.when(i == nt + 1)
    def _():
        finish_rows(acc_bufs[(nt - 1) % 2])


def mlp(x, norm_g, g_in_idx, g_out_idx, w1, w2, tm=1024, tf=1024):
    m = x.shape[0]
    nt, nf = m // tm, D_FF // tf
    rows = tm // nf
    clamp = lambda t: jnp.clip(t, 0, nt - 1)
    chunk_of = lambda i, f: jnp.where(jnp.logical_and(i >= 1, i <= nt), f, 0)
    return pl.pallas_call(
        functools.partial(_mlp_kernel, nt=nt),
        grid=(nt + 2, nf),
        in_specs=[pl.BlockSpec((rows, D_MODEL), lambda i, f: (clamp(i) * nf + f, 0)),
                  pl.BlockSpec((rows, D_MODEL), lambda i, f: (clamp(i - 2) * nf + f, 0)),
                  pl.BlockSpec((None, 1, D_MODEL), lambda i, f: (g_in_idx, 0, 0)),
                  pl.BlockSpec((D_MODEL, tf), lambda i, f: (0, chunk_of(i, f))),
                  pl.BlockSpec((tf, D_MODEL), lambda i, f: (chunk_of(i, f), 0)),
                  pl.BlockSpec((None, 1, D_MODEL), lambda i, f: (g_out_idx, 0, 0))],
        out_specs=pl.BlockSpec((rows, D_MODEL), lambda i, f: (clamp(i - 2) * nf + f, 0)),
        out_shape=jax.ShapeDtypeStruct((m, D_MODEL), F32),
        scratch_shapes=[pltpu.VMEM((tm, D_MODEL), BF16), pltpu.VMEM((tm, D_MODEL), BF16),
                        pltpu.VMEM((tm, D_MODEL), F32), pltpu.VMEM((tm, D_MODEL), F32)],
        compiler_params=_params(("arbitrary", "arbitrary")),
        name="mlp",
    )(x, x, norm_g, w1, w2, norm_g)


def kernel(x, positions, norm_g, ret_w_in, ret_w_out, mlstm_w_in, mlstm_b_gate,
           mlstm_norm_g, mlstm_w_out, mlp_w1, mlp_w2):
    batch, seq, d = x.shape
    m = batch * seq
    xf = x.reshape(m, d)

    inv_freq = jnp.power(ROPE_BASE, -jnp.linspace(0.0, 1.0, RET_DK // 2, dtype=F32))
    cos, sin = rope_tables(positions, inv_freq)

    gains = norm_g.astype(F32).reshape(DEPTH * 4, 1, d)
    ml_w_in_b = mlstm_w_in.astype(BF16)
    n_gate = 2 * ML_HEADS
    ret_w_in_b = ret_w_in[0].astype(BF16)

    for i in range(DEPTH):
        j = i // 2
        if i % 2 == 0:
            proj, w1_b, w2_b, w_out_b = ret_proj(
                xf, gains, 4 * i, ret_w_in_b, cos, sin,
                [(mlp_w1, i), (mlp_w2, i), (ret_w_out, j)])
            xf = ret_mix_out(proj, xf, w_out_b, gains, 4 * i + 1, batch, seq)
        else:
            w_gate = ml_w_in_b[j, :, ML_MAIN:]
            wg = jnp.pad(w_gate, ((0, 0), (0, GATE_LANES - n_gate)))
            bias = mlstm_b_gate[j].astype(F32)
            b = jnp.pad(bias, (0, GATE_LANES - n_gate)).reshape(1, GATE_LANES)
            proj, gcol, grow, w_out_b = ml_proj(xf, gains, 4 * i, ml_w_in_b, j, wg, b,
                                                [(mlstm_w_out, j)])
            head_g = mlstm_norm_g[j].reshape(1, ML_V).astype(F32)
            jobs = [(mlp_w1, i), (mlp_w2, i)]
            if i + 1 < DEPTH:
                jobs.append((ret_w_in, j + 1))
            xf, w1_b, w2_b, *nxt = ml_mix_out(proj, gcol, grow, head_g, xf, w_out_b, gains,
                                              4 * i + 1, jobs, batch, seq)
            if nxt:
                ret_w_in_b = nxt[0]
        xf = mlp(xf, gains, 4 * i + 2, 4 * i + 3, w1_b, w2_b)
    return xf.reshape(batch, seq, d)
```

```python
import functools
import math

import jax
import jax.numpy as jnp
from jax import lax
from jax.experimental import pallas as pl
from jax.experimental.pallas import tpu as pltpu

F32 = jnp.float32
BF16 = jnp.bfloat16

D_MODEL = 2048
DEPTH = 4
CHUNK = 256
RET_CHUNK = 256
EPS = 1e-6
D_FF = 4 * D_MODEL

RET_HEADS = 8
RET_DK = 256
RET_QK = RET_HEADS * RET_DK
RET_V = 2 * D_MODEL
RET_DV = RET_V // RET_HEADS
RET_IN = 2 * RET_QK + 2 * RET_V
ROPE_BASE = 10000.0

ML_HEADS = 8
ML_QK = D_MODEL // 2
ML_DQK = ML_QK // ML_HEADS
ML_V = D_MODEL
ML_DV = ML_V // ML_HEADS
ML_MAIN = 2 * ML_QK + 2 * ML_V
GATE_SOFTCAP = 15.0
GATE_LANES = 128

VMEM_LIMIT = 60 * 1024 * 1024

NT_DIMS = (((1,), (1,)), ((), ()))
TN_DIMS = (((0,), (0,)), ((), ()))


def _rms(x, g):
    ms = jnp.mean(x * x, axis=-1, keepdims=True)
    return x * lax.rsqrt(ms + EPS) * g


CAST_BLOCKS = 64
NORM_ROWS = 256
NORM_SUB = 64


def _norm_rows(dst_ref, src_ref, g_ref):
    g = g_ref[...]

    def body(r, carry):
        rows = pl.ds(pl.multiple_of(r * NORM_ROWS, NORM_ROWS), NORM_ROWS)
        sq = src_ref[rows, :]
        inv = lax.rsqrt(jnp.mean(sq * sq, axis=-1, keepdims=True) + EPS)
        for t in range(NORM_ROWS // NORM_SUB):
            sub = pl.ds(pl.multiple_of(r * NORM_ROWS + t * NORM_SUB, NORM_SUB), NORM_SUB)
            v = src_ref[sub, :] * inv[t * NORM_SUB:(t + 1) * NORM_SUB] * g
            dst_ref[sub, :] = v.astype(dst_ref.dtype)
        return carry

    lax.fori_loop(0, src_ref.shape[0] // NORM_ROWS, body, 0)


def _params(sem, **kw):
    return pltpu.CompilerParams(dimension_semantics=sem, vmem_limit_bytes=VMEM_LIMIT, **kw)


def _rope_kernel(pos_ref, freq_ref, *rest, n_cast):
    cast_src, (cos_ref, sin_ref), cast_dst, _ = _split_refs(rest, n_cast, 2)
    _cast_blocks(cast_src, cast_dst)
    ang = pos_ref[...].astype(F32) * freq_ref[...]
    cos_ref[...] = jnp.cos(ang)
    sin_ref[...] = jnp.sin(ang)


def rope_tables(positions, inv_freq, cast_jobs, tm=1024):
    m = positions.size
    pos = positions.reshape(m, 1)
    half = inv_freq.shape[-1]
    n_steps = m // tm
    c_in, c_out, c_shape, c_args = _cast_specs(cast_jobs, n_steps, lambda i: i)
    return pl.pallas_call(
        functools.partial(_rope_kernel, n_cast=len(cast_jobs)),
        grid=(n_steps,),
        in_specs=[pl.BlockSpec((tm, 1), lambda i: (i, 0)),
                  pl.BlockSpec((1, half), lambda i: (0, 0))] + c_in,
        out_specs=[pl.BlockSpec((tm, half), lambda i: (i, 0)),
                   pl.BlockSpec((tm, half), lambda i: (i, 0))] + c_out,
        out_shape=[jax.ShapeDtypeStruct((m, half), F32)] * 2 + c_shape,
        compiler_params=_params(("arbitrary",)),
        name="rope_tables",
    )(pos, inv_freq.reshape(1, half), *c_args)


def _cast_blocks(src_refs, dst_refs):
    for src, dst in zip(src_refs, dst_refs):
        dst[...] = src[...].astype(BF16)


def _cast_specs(jobs, n_blocks, step_of):
    blk = lambda *ids: jnp.minimum(step_of(*ids), n_blocks - 1)
    in_specs, out_specs, out_shapes, operands = [], [], [], []
    for stack, layer in jobs:
        _, r, c = stack.shape
        rows = r // n_blocks
        in_specs.append(pl.BlockSpec((None, rows, c),
                                     lambda *ids, layer=layer: (layer, blk(*ids), 0)))
        out_specs.append(pl.BlockSpec((rows, c), lambda *ids: (blk(*ids), 0)))
        out_shapes.append(jax.ShapeDtypeStruct((r, c), BF16))
        operands.append(stack)
    return in_specs, out_specs, out_shapes, operands


def _split_refs(rest, n_cast, n_out):
    a, b = n_cast, n_cast + n_out
    return rest[:a], rest[a:b], rest[b:b + n_cast], rest[b + n_cast:]


def _ret_proj_kernel(x_ref, g_ref, w_ref, cos_ref, sin_ref, *rest, tn, n_cast):
    cast_src, (o_ref,), cast_dst, (h_ref,) = _split_refs(rest, n_cast, 1)
    j = pl.program_id(1)
    n_rot = 2 * RET_QK // tn

    @pl.when(j == 0)
    def _():
        _norm_rows(h_ref, x_ref, g_ref)

    @pl.when(j < n_rot)
    def _():
        _cast_blocks(cast_src, cast_dst)
        res = jnp.dot(h_ref[...], w_ref[...], preferred_element_type=F32)
        scale = jnp.where(j >= n_rot // 2, RET_DK ** -0.5, 1.0).astype(F32)
        c = cos_ref[...] * scale
        s = sin_ref[...] * scale
        half = RET_DK // 2
        for hh in range(tn // RET_DK):
            lo = hh * RET_DK
            t1 = res[:, lo:lo + half]
            t2 = res[:, lo + half:lo + RET_DK]
            o_ref[:, lo:lo + half] = (t1 * c - t2 * s).astype(BF16)
            o_ref[:, lo + half:lo + RET_DK] = (t2 * c + t1 * s).astype(BF16)

    @pl.when(j >= n_rot)
    def _():
        _cast_blocks(cast_src, cast_dst)
        o_ref[...] = jnp.dot(h_ref[...], w_ref[...],
                             preferred_element_type=F32).astype(BF16)


def ret_proj(x, norm_g, g_idx, w, cos, sin, cast_jobs, tm=1024, tn=2048):
    m = x.shape[0]
    nj = RET_IN // tn
    assert CAST_BLOCKS <= (m // tm) * nj
    c_in, c_out, c_shape, c_args = _cast_specs(cast_jobs, CAST_BLOCKS, lambda i, j: i * nj + j)
    return pl.pallas_call(
        functools.partial(_ret_proj_kernel, tn=tn, n_cast=len(cast_jobs)),
        grid=(m // tm, nj),
        in_specs=[pl.BlockSpec((tm, D_MODEL), lambda i, j: (i, 0)),
                  pl.BlockSpec((None, 1, D_MODEL), lambda i, j: (g_idx, 0, 0)),
                  pl.BlockSpec((D_MODEL, tn), lambda i, j: (0, j)),
                  pl.BlockSpec((tm, RET_DK // 2), lambda i, j: (i, 0)),
                  pl.BlockSpec((tm, RET_DK // 2), lambda i, j: (i, 0))] + c_in,
        out_specs=[pl.BlockSpec((tm, tn), lambda i, j: (i, j))] + c_out,
        out_shape=[jax.ShapeDtypeStruct((m, RET_IN), BF16)] + c_shape,
        scratch_shapes=[pltpu.VMEM((tm, D_MODEL), BF16)],
        compiler_params=_params(("arbitrary", "arbitrary")),
        name="ret_proj",
    )(x, norm_g, w, cos, sin, *c_args)


def _ret_log_gamma(h):
    return math.log1p(-(2.0 ** (-5.0 - h)))


OUT_PIECE = 256
RET_PIECE_HEADS = ((0, 2, 4, 6), (0, 2, 4, 6))
ML_PIECE_HEADS = ((0, 2, 4, 6), (0, 2, 4, 6))


def _ret_kernel(p_ref, x_ref, w_ref, gain_ref, o_ref,
                st_ref, dec_ref, xi_ref, zeta_ref, y_ref, *, nc):
    t = pl.program_id(0)
    slot = t % 2

    @pl.when(t == 0)
    def _():
        y_ref[...] = jnp.zeros_like(y_ref)

    @pl.when(t % nc == 0)
    def _():
        st_ref[...] = jnp.zeros_like(st_ref)
        n = lax.broadcasted_iota(jnp.int32, (RET_CHUNK, RET_CHUNK), 0)
        mcol = lax.broadcasted_iota(jnp.int32, (RET_CHUNK, RET_CHUNK), 1)
        rel = (n - mcol).astype(F32)
        row = lax.broadcasted_iota(jnp.int32, (RET_CHUNK, RET_DK), 0).astype(F32)
        for h in range(RET_HEADS):
            lg = _ret_log_gamma(h)
            dec_ref[h] = jnp.where(rel >= 0, jnp.exp(jnp.maximum(rel, 0.0) * lg), 0.0)
            xi_ref[h] = jnp.exp((row + 1.0) * lg).astype(BF16)
            zeta_ref[h] = jnp.exp((RET_CHUNK - 1.0 - row) * lg).astype(BF16)

    heads = range(RET_HEADS)
    k0, v0, g0 = RET_QK, 2 * RET_QK, 2 * RET_QK + RET_V
    q = [p_ref[:, h * RET_DK:(h + 1) * RET_DK] for h in heads]
    k = [p_ref[:, k0 + h * RET_DK:k0 + (h + 1) * RET_DK] for h in heads]
    v = [p_ref[:, v0 + h * RET_DV:v0 + (h + 1) * RET_DV] for h in heads]


    y_prev = y_ref[1 - slot]
    pieces = []

    def out_piece():
        p = len(pieces)
        pieces.append(jnp.dot(y_prev, w_ref[:, p * OUT_PIECE:(p + 1) * OUT_PIECE],
                              preferred_element_type=F32))

    raw = [lax.dot_general(q[h], k[h], NT_DIMS, preferred_element_type=F32) for h in heads]

    lhs, rhs, kz = [], [], []
    for h in heads:
        if h in RET_PIECE_HEADS[0]:
            out_piece()
        lhs.append(jnp.concatenate([(raw[h] * dec_ref[h]).astype(BF16), q[h] * xi_ref[h]],
                                   axis=1))
        rhs.append(jnp.concatenate([v[h], st_ref[h].astype(BF16)], axis=0))
        kz.append(k[h] * zeta_ref[h])

    y = []
    for h in heads:
        g_chunk = math.exp(RET_CHUNK * _ret_log_gamma(h))
        y.append(jnp.dot(lhs[h], rhs[h], preferred_element_type=F32))
        st_ref[h] = st_ref[h] * g_chunk + lax.dot_general(kz[h], v[h], TN_DIMS,
                                                          preferred_element_type=F32)

    for h in heads:
        if h in RET_PIECE_HEADS[1]:
            out_piece()
        cols = slice(h * RET_DV, (h + 1) * RET_DV)
        mu = jnp.mean(y[h], axis=-1, keepdims=True)
        yc = y[h] - mu
        var = jnp.mean(yc * yc, axis=-1, keepdims=True)
        yn = yc * lax.rsqrt(var + EPS)
        gh = p_ref[:, g0 + h * RET_DV:g0 + (h + 1) * RET_DV]
        y_ref[slot, :, cols] = (gh * jax.nn.sigmoid(gh)) * yn.astype(BF16)

    out = jnp.concatenate(pieces, axis=1)
    o_ref[...] = x_ref[...] + _rms(out, gain_ref[...])


def ret_mix_out(proj, x, w_out, norm_g, g_idx, batch, seq):
    assert sum(map(len, RET_PIECE_HEADS)) * OUT_PIECE == D_MODEL
    m = batch * seq
    n_chunks = m // RET_CHUNK
    cur = lambda t: jnp.minimum(t, n_chunks - 1)
    prev = lambda t: jnp.maximum(t - 1, 0)
    return pl.pallas_call(
        functools.partial(_ret_kernel, nc=seq // RET_CHUNK),
        grid=(n_chunks + 1,),
        in_specs=[pl.BlockSpec((RET_CHUNK, RET_IN), lambda t: (cur(t), 0)),
                  pl.BlockSpec((RET_CHUNK, D_MODEL), lambda t: (prev(t), 0)),
                  pl.BlockSpec((RET_V, D_MODEL), lambda t: (0, 0),
                               pipeline_mode=pl.Buffered(1)),
                  pl.BlockSpec((None, 1, D_MODEL), lambda t: (g_idx, 0, 0))],
        out_specs=pl.BlockSpec((RET_CHUNK, D_MODEL), lambda t: (prev(t), 0)),
        out_shape=jax.ShapeDtypeStruct((m, D_MODEL), F32),
        scratch_shapes=[pltpu.VMEM((RET_HEADS, RET_DK, RET_DV), F32),
                        pltpu.VMEM((RET_HEADS, RET_CHUNK, RET_CHUNK), F32),
                        pltpu.VMEM((RET_HEADS, RET_CHUNK, RET_DK), BF16),
                        pltpu.VMEM((RET_HEADS, RET_CHUNK, RET_DK), BF16),
                        pltpu.VMEM((2, RET_CHUNK, RET_V), BF16)],
        compiler_params=_params(("arbitrary",)),
        name="ret_mix_out",
    )(proj, x, w_out, norm_g)


def _log_sigmoid(x):
    return jnp.minimum(x, 0.0) - jnp.log1p(jnp.exp(-jnp.abs(x)))


def _gate_logs(pre, is_forget):
    capped = GATE_SOFTCAP * jnp.tanh(pre / GATE_SOFTCAP)
    return jnp.where(is_forget, _log_sigmoid(capped), capped)


def _ml_proj_kernel(x_ref, g_ref, w_ref, wg_ref, b_ref, *rest, n_cast):
    cast_src, (o_ref, gcol_ref, grow_ref), cast_dst, (h_ref,) = _split_refs(rest, n_cast, 3)
    j = pl.program_id(1)

    def main_tile():
        _cast_blocks(cast_src, cast_dst)
        o_ref[...] = jnp.dot(h_ref[...], w_ref[...], preferred_element_type=F32).astype(BF16)

    @pl.when(j == 0)
    def _():
        _norm_rows(h_ref, x_ref, g_ref)
        pre = jnp.dot(h_ref[...], wg_ref[...], preferred_element_type=F32) + b_ref[...]
        lane = lax.broadcasted_iota(jnp.int32, pre.shape, 1)
        logs = _gate_logs(pre, lane >= ML_HEADS)
        gcol_ref[...] = logs
        grow_ref[...] = logs.T[:2 * ML_HEADS, :]
        main_tile()

    @pl.when(j > 0)
    def _():
        main_tile()


def ml_proj(x, norm_g, g_idx, w, layer, wg, b, cast_jobs, tm=1024, tn=2048):
    m = x.shape[0]
    ng = 2 * ML_HEADS
    nj = ML_MAIN // tn
    n_blocks = CAST_BLOCKS // 2
    assert n_blocks <= (m // tm) * nj
    c_in, c_out, c_shape, c_args = _cast_specs(cast_jobs, n_blocks, lambda i, j: i * nj + j)
    return pl.pallas_call(
        functools.partial(_ml_proj_kernel, n_cast=len(cast_jobs)),
        grid=(m // tm, nj),
        in_specs=[pl.BlockSpec((tm, D_MODEL), lambda i, j: (i, 0)),
                  pl.BlockSpec((None, 1, D_MODEL), lambda i, j: (g_idx, 0, 0)),
                  pl.BlockSpec((None, D_MODEL, tn), lambda i, j: (layer, 0, j)),
                  pl.BlockSpec((D_MODEL, GATE_LANES), lambda i, j: (0, 0)),
                  pl.BlockSpec((1, GATE_LANES), lambda i, j: (0, 0))] + c_in,
        out_specs=[pl.BlockSpec((tm, tn), lambda i, j: (i, j)),
                   pl.BlockSpec((tm, GATE_LANES), lambda i, j: (i, 0)),
                   pl.BlockSpec((ng, tm), lambda i, j: (0, i))] + c_out,
        out_shape=[jax.ShapeDtypeStruct((m, ML_MAIN), BF16),
                   jax.ShapeDtypeStruct((m, GATE_LANES), F32),
                   jax.ShapeDtypeStruct((ng, m), F32)] + c_shape,
        scratch_shapes=[pltpu.VMEM((tm, D_MODEL), BF16)],
        compiler_params=_params(("arbitrary", "arbitrary")),
        name="ml_proj",
    )(x, norm_g, w, wg, b, *c_args)


def _ml_kernel(p_ref, gcol_ref, grow_ref, ng_ref, x_ref, w_ref, gain_ref, *rest, nc, n_cast):
    cast_src, (out_ref,), cast_dst, (c_ref, n_ref, m_ref, y_ref) = _split_refs(rest, n_cast, 1)
    t = pl.program_id(0)
    slot = t % 2

    @pl.when(t == 0)
    def _():
        y_ref[...] = jnp.zeros_like(y_ref)

    @pl.when(t % nc == 0)
    def _():
        c_ref[...] = jnp.zeros_like(c_ref)
        n_ref[...] = jnp.zeros_like(n_ref)
        m_ref[...] = jnp.zeros_like(m_ref)

    _cast_blocks(cast_src, cast_dst)

    s_idx = lax.broadcasted_iota(jnp.int32, (CHUNK, CHUNK), 0)
    n_idx = lax.broadcasted_iota(jnp.int32, (CHUNK, CHUNK), 1)
    causal = s_idx <= n_idx
    eye = s_idx == n_idx
    lower = (s_idx >= n_idx).astype(F32)
    upper = causal.astype(F32)

    gcol = gcol_ref[...]
    grow = grow_ref[...]
    hi = lax.Precision.HIGHEST
    cum_col = jnp.dot(lower, gcol, precision=hi, preferred_element_type=F32)
    cum_row = jnp.dot(grow, upper, precision=hi, preferred_element_type=F32)
    key_col = gcol - pltpu.roll(cum_col, GATE_LANES - ML_HEADS, axis=1)
    scale = ML_DQK ** -0.5
    log_scale = math.log(scale)

    heads = range(ML_HEADS)
    k0, v0, o0 = ML_QK, 2 * ML_QK, 2 * ML_QK + ML_V
    q = [p_ref[:, h * ML_DQK:(h + 1) * ML_DQK] for h in heads]
    k = [p_ref[:, k0 + h * ML_DQK:k0 + (h + 1) * ML_DQK] for h in heads]
    v = [p_ref[:, v0 + h * ML_DV:v0 + (h + 1) * ML_DV] for h in heads]


    raw, qn, qc = [], [], []
    for h in heads:
        raw.append(lax.dot_general(k[h], q[h], NT_DIMS, preferred_element_type=F32))
        qn.append(lax.dot_general(n_ref[h].astype(BF16), q[h], NT_DIMS,
                                  preferred_element_type=F32)[0:1, :])
        qc.append(jnp.dot(q[h], c_ref[h].astype(BF16),
                          preferred_element_type=F32).astype(BF16))

    y_prev = y_ref[1 - slot]
    pieces = []

    def out_piece():
        p = len(pieces)
        pieces.append(jnp.dot(y_prev, w_ref[:, p * OUT_PIECE:(p + 1) * OUT_PIECE],
                              preferred_element_type=F32))

    lhs_t, kw_t, w_rows, carry, m_new = [], [], [], [], []
    for h in heads:
        if h in ML_PIECE_HEADS[0]:
            out_piece()
        brow = cum_row[ML_HEADS + h:ML_HEADS + h + 1, :]
        irow = grow[h:h + 1, :]
        kcol = key_col[:, h:h + 1]
        m_st = m_ref[h][0:1, 0:1]

        d_log = jnp.where(causal, brow + kcol, -jnp.inf)
        inter_log = brow + m_st
        m_out = jnp.maximum(inter_log, jnp.max(d_log, axis=0, keepdims=True))
        d_w = jnp.exp(d_log - (m_out - log_scale))
        inter_w = jnp.exp(inter_log - m_out)
        scores = raw[h] * d_w
        den = jnp.sum(scores, axis=0, keepdims=True) + inter_w * qn[h]
        r = 1.0 / jnp.maximum(jnp.abs(den), jnp.exp(-m_out))
        p_t = (scores * r).astype(BF16)
        diag = jnp.where(eye, inter_w * r, 0.0).astype(BF16)
        lhs_t.append(jnp.concatenate([p_t, diag], axis=0))

        b_tot = brow[:, CHUNK - 1:CHUNK]
        w_log = b_tot - brow + irow
        m_nw = jnp.maximum(b_tot + m_st, jnp.max(w_log, axis=1, keepdims=True))
        w_s = jnp.exp(w_log - m_nw) * scale
        carry.append(jnp.exp(b_tot + m_st - m_nw))
        m_new.append(m_nw)
        kw_t.append((k[h].astype(F32).T * w_s).astype(BF16))
        w_rows.append(jnp.broadcast_to(w_s, (8, CHUNK)).astype(BF16))

    h_out = []
    for h in heads:
        rhs = jnp.concatenate([v[h], qc[h]], axis=0)
        h_out.append(lax.dot_general(lhs_t[h], rhs, TN_DIMS, preferred_element_type=F32))
        c_ref[h] = carry[h] * c_ref[h] + jnp.dot(kw_t[h], v[h], preferred_element_type=F32)
        n_ref[h] = carry[h] * n_ref[h] + jnp.dot(w_rows[h], k[h], preferred_element_type=F32)
        m_ref[h] = jnp.broadcast_to(m_new[h], m_ref.shape[1:])

    for h in heads:
        if h in ML_PIECE_HEADS[1]:
            out_piece()
        cols = slice(h * ML_DV, (h + 1) * ML_DV)
        y = h_out[h]
        yn = y * lax.rsqrt(jnp.mean(y * y, axis=-1, keepdims=True) + EPS) * ng_ref[:, cols]
        gate = jax.nn.sigmoid(p_ref[:, o0 + h * ML_DV:o0 + (h + 1) * ML_DV])
        y_ref[slot, :, cols] = yn.astype(BF16) * gate

    out = jnp.concatenate(pieces, axis=1)
    out_ref[...] = x_ref[...] + _rms(out, gain_ref[...])


def ml_mix_out(proj, gcol, grow, head_g, x, w_out, norm_g, g_idx, cast_jobs, batch, seq):
    assert sum(map(len, ML_PIECE_HEADS)) * OUT_PIECE == D_MODEL
    m = batch * seq
    n_chunks = m // CHUNK
    cur = lambda t: jnp.minimum(t, n_chunks - 1)
    prev = lambda t: jnp.maximum(t - 1, 0)
    assert CAST_BLOCKS <= n_chunks + 1
    c_in, c_out, c_shape, c_args = _cast_specs(cast_jobs, CAST_BLOCKS, lambda t: t)
    return pl.pallas_call(
        functools.partial(_ml_kernel, nc=seq // CHUNK, n_cast=len(cast_jobs)),
        grid=(n_chunks + 1,),
        in_specs=[pl.BlockSpec((CHUNK, ML_MAIN), lambda t: (cur(t), 0)),
                  pl.BlockSpec((CHUNK, GATE_LANES), lambda t: (cur(t), 0)),
                  pl.BlockSpec((2 * ML_HEADS, CHUNK), lambda t: (0, cur(t))),
                  pl.BlockSpec((1, ML_V), lambda t: (0, 0)),
                  pl.BlockSpec((CHUNK, D_MODEL), lambda t: (prev(t), 0)),
                  pl.BlockSpec((ML_V, D_MODEL), lambda t: (0, 0),
                               pipeline_mode=pl.Buffered(1)),
                  pl.BlockSpec((None, 1, D_MODEL), lambda t: (g_idx, 0, 0))] + c_in,
        out_specs=[pl.BlockSpec((CHUNK, D_MODEL), lambda t: (prev(t), 0))] + c_out,
        out_shape=[jax.ShapeDtypeStruct((m, D_MODEL), F32)] + c_shape,
        scratch_shapes=[pltpu.VMEM((ML_HEADS, ML_DQK, ML_DV), F32),
                        pltpu.VMEM((ML_HEADS, 8, ML_DQK), F32),
                        pltpu.VMEM((ML_HEADS, 8, 128), F32),
                        pltpu.VMEM((2, CHUNK, ML_V), BF16)],
        compiler_params=_params(("arbitrary",)),
        name="ml_mix_out",
    )(proj, gcol, grow, head_g, x, w_out, norm_g, *c_args)


def _mlp_kernel(xn_ref, xr_ref, g_in_ref, w1_ref, w2_ref, g_out_ref, o_ref,
                h_even, h_odd, acc_even, acc_odd, *, nt):
    i = pl.program_id(0)
    f = pl.program_id(1)
    n_rows = xn_ref.shape[0]
    rows = pl.ds(pl.multiple_of(f * n_rows, n_rows), n_rows)
    h_bufs = (h_even, h_odd)
    acc_bufs = (acc_even, acc_odd)

    def norm_rows(h_dst):
        h_dst[rows, :] = _rms(xn_ref[...], g_in_ref[...]).astype(BF16)

    def finish_rows(acc_src):
        o_ref[...] = xr_ref[...] + _rms(acc_src[rows, :], g_out_ref[...])

    def chunk(h_src, acc_dst, first):
        hid = jnp.maximum(jnp.dot(h_src[...], w1_ref[...], preferred_element_type=F32), 0.0)
        out = jnp.dot((hid * hid).astype(BF16), w2_ref[...], preferred_element_type=F32)
        if first:
            acc_dst[...] = out
        else:
            acc_dst[...] += out

    @pl.when(i == 0)
    def _():
        norm_rows(h_even)
        acc_odd[rows, :] = jnp.zeros((n_rows, acc_odd.shape[1]), F32)
        o_ref[...] = xr_ref[...]

    in_flight = jnp.logical_and(i >= 1, i <= nt)
    for parity in (0, 1):
        mine = jnp.logical_and(in_flight, (i - 1) % 2 == parity)

        @pl.when(jnp.logical_and(mine, f == 0))
        def _(parity=parity):
            finish_rows(acc_bufs[1 - parity])
            chunk(h_bufs[parity], acc_bufs[parity], True)
            norm_rows(h_bufs[1 - parity])

        @pl.when(jnp.logical_and(mine, f > 0))
        def _(parity=parity):
            finish_rows(acc_bufs[1 - parity])
            chunk(h_bufs[parity], acc_bufs[parity], False)
            norm_rows(h_bufs[1 - parity])

    @pl.when(i == nt + 1)
    def _():
        finish_rows(acc_bufs[(nt - 1) % 2])


def mlp(x, norm_g, g_in_idx, g_out_idx, w1, w2, tm=1024, tf=1024):
    m = x.shape[0]
    nt, nf = m // tm, D_FF // tf
    rows = tm // nf
    clamp = lambda t: jnp.clip(t, 0, nt - 1)
    chunk_of = lambda i, f: jnp.where(jnp.logical_and(i >= 1, i <= nt), f, 0)
    return pl.pallas_call(
        functools.partial(_mlp_kernel, nt=nt),
        grid=(nt + 2, nf),
        in_specs=[pl.BlockSpec((rows, D_MODEL), lambda i, f: (clamp(i) * nf + f, 0)),
                  pl.BlockSpec((rows, D_MODEL), lambda i, f: (clamp(i - 2) * nf + f, 0)),
                  pl.BlockSpec((None, 1, D_MODEL), lambda i, f: (g_in_idx, 0, 0)),
                  pl.BlockSpec((D_MODEL, tf), lambda i, f: (0, chunk_of(i, f))),
                  pl.BlockSpec((tf, D_MODEL), lambda i, f: (chunk_of(i, f), 0)),
                  pl.BlockSpec((None, 1, D_MODEL), lambda i, f: (g_out_idx, 0, 0))],
        out_specs=pl.BlockSpec((rows, D_MODEL), lambda i, f: (clamp(i - 2) * nf + f, 0)),
        out_shape=jax.ShapeDtypeStruct((m, D_MODEL), F32),
        scratch_shapes=[pltpu.VMEM((tm, D_MODEL), BF16), pltpu.VMEM((tm, D_MODEL), BF16),
                        pltpu.VMEM((tm, D_MODEL), F32), pltpu.VMEM((tm, D_MODEL), F32)],
        compiler_params=_params(("arbitrary", "arbitrary")),
        name="mlp",
    )(x, x, norm_g, w1, w2, norm_g)


def kernel(x, positions, norm_g, ret_w_in, ret_w_out, mlstm_w_in, mlstm_b_gate,
           mlstm_norm_g, mlstm_w_out, mlp_w1, mlp_w2):
    batch, seq, d = x.shape
    m = batch * seq
    xf = x.reshape(m, d)

    inv_freq = jnp.power(ROPE_BASE, -jnp.linspace(0.0, 1.0, RET_DK // 2, dtype=F32))
    cos, sin, ret_w_in_b = rope_tables(positions, inv_freq, [(ret_w_in, 0)])
    gains = norm_g.astype(F32).reshape(DEPTH * 4, 1, d)
    ml_w_in_b = mlstm_w_in.astype(BF16)
    n_gate = 2 * ML_HEADS

    for i in range(DEPTH):
        j = i // 2
        if i % 2 == 0:
            proj, w1_b, w2_b, w_out_b = ret_proj(
                xf, gains, 4 * i, ret_w_in_b, cos, sin,
                [(mlp_w1, i), (mlp_w2, i), (ret_w_out, j)])
            xf = ret_mix_out(proj, xf, w_out_b, gains, 4 * i + 1, batch, seq)
        else:
            w_gate = ml_w_in_b[j, :, ML_MAIN:]
            wg = jnp.pad(w_gate, ((0, 0), (0, GATE_LANES - n_gate)))
            bias = mlstm_b_gate[j].astype(F32)
            b = jnp.pad(bias, (0, GATE_LANES - n_gate)).reshape(1, GATE_LANES)
            proj, gcol, grow, w_out_b = ml_proj(xf, gains, 4 * i, ml_w_in_b, j, wg, b,
                                                [(mlstm_w_out, j)])
            head_g = mlstm_norm_g[j].reshape(1, ML_V).astype(F32)
            jobs = [(mlp_w1, i), (mlp_w2, i)]
            if i + 1 < DEPTH:
                jobs.append((ret_w_in, j + 1))
            xf, w1_b, w2_b, *nxt = ml_mix_out(proj, gcol, grow, head_g, xf, w_out_b, gains,
                                              4 * i + 1, jobs, batch, seq)
            if nxt:
                ret_w_in_b = nxt[0]
        xf = mlp(xf, gains, 4 * i + 2, 4 * i + 3, w1_b, w2_b)
    return xf.reshape(batch, seq, d)
```

```python
import functools
import math

import jax
import jax.numpy as jnp
from jax import lax
from jax.experimental import pallas as pl
from jax.experimental.pallas import tpu as pltpu

F32 = jnp.float32
BF16 = jnp.bfloat16

D_MODEL = 2048
DEPTH = 4
CHUNK = 256
RET_CHUNK = 256
EPS = 1e-6
D_FF = 4 * D_MODEL

RET_HEADS = 8
RET_DK = 256
RET_QK = RET_HEADS * RET_DK
RET_V = 2 * D_MODEL
RET_DV = RET_V // RET_HEADS
RET_IN = 2 * RET_QK + 2 * RET_V
ROPE_BASE = 10000.0

ML_HEADS = 8
ML_QK = D_MODEL // 2
ML_DQK = ML_QK // ML_HEADS
ML_V = D_MODEL
ML_DV = ML_V // ML_HEADS
ML_MAIN = 2 * ML_QK + 2 * ML_V
GATE_SOFTCAP = 15.0
GATE_LANES = 128

VMEM_LIMIT = 60 * 1024 * 1024

NT_DIMS = (((1,), (1,)), ((), ()))
TN_DIMS = (((0,), (0,)), ((), ()))


def _rms(x, g):
    ms = jnp.mean(x * x, axis=-1, keepdims=True)
    return x * lax.rsqrt(ms + EPS) * g


CAST_BLOCKS = 64
NORM_ROWS = 256
NORM_SUB = 64


def _norm_rows(dst_ref, src_ref, g_ref):
    g = g_ref[...]

    def body(r, carry):
        rows = pl.ds(pl.multiple_of(r * NORM_ROWS, NORM_ROWS), NORM_ROWS)
        sq = src_ref[rows, :]
        inv = lax.rsqrt(jnp.mean(sq * sq, axis=-1, keepdims=True) + EPS)
        for t in range(NORM_ROWS // NORM_SUB):
            sub = pl.ds(pl.multiple_of(r * NORM_ROWS + t * NORM_SUB, NORM_SUB), NORM_SUB)
            v = src_ref[sub, :] * inv[t * NORM_SUB:(t + 1) * NORM_SUB] * g
            dst_ref[sub, :] = v.astype(dst_ref.dtype)
        return carry

    lax.fori_loop(0, src_ref.shape[0] // NORM_ROWS, body, 0)


def _params(sem, **kw):
    return pltpu.CompilerParams(dimension_semantics=sem, vmem_limit_bytes=VMEM_LIMIT, **kw)


def _rope_kernel(pos_ref, freq_ref, *rest, n_cast):
    cast_src, (cos_ref, sin_ref), cast_dst, _ = _split_refs(rest, n_cast, 2)
    _cast_blocks(cast_src, cast_dst)
    ang = pos_ref[...].astype(F32) * freq_ref[...]
    cos_ref[...] = jnp.cos(ang)
    sin_ref[...] = jnp.sin(ang)


def rope_tables(positions, inv_freq, cast_jobs, tm=1024):
    m = positions.size
    pos = positions.reshape(m, 1)
    half = inv_freq.shape[-1]
    n_steps = m // tm
    c_in, c_out, c_shape, c_args = _cast_specs(cast_jobs, n_steps, lambda i: i)
    return pl.pallas_call(
        functools.partial(_rope_kernel, n_cast=len(cast_jobs)),
        grid=(n_steps,),
        in_specs=[pl.BlockSpec((tm, 1), lambda i: (i, 0)),
                  pl.BlockSpec((1, half), lambda i: (0, 0))] + c_in,
        out_specs=[pl.BlockSpec((tm, half), lambda i: (i, 0)),
                   pl.BlockSpec((tm, half), lambda i: (i, 0))] + c_out,
        out_shape=[jax.ShapeDtypeStruct((m, half), F32)] * 2 + c_shape,
        compiler_params=_params(("arbitrary",)),
        name="rope_tables",
    )(pos, inv_freq.reshape(1, half), *c_args)


def _cast_blocks(src_refs, dst_refs):
    for src, dst in zip(src_refs, dst_refs):
        dst[...] = src[...].astype(BF16)


def _cast_specs(jobs, n_blocks, step_of):
    blk = lambda *ids: jnp.minimum(step_of(*ids), n_blocks - 1)
    in_specs, out_specs, out_shapes, operands = [], [], [], []
    for stack, layer in jobs:
        _, r, c = stack.shape
        rows = r // n_blocks
        in_specs.append(pl.BlockSpec((None, rows, c),
                                     lambda *ids, layer=layer: (layer, blk(*ids), 0)))
        out_specs.append(pl.BlockSpec((rows, c), lambda *ids: (blk(*ids), 0)))
        out_shapes.append(jax.ShapeDtypeStruct((r, c), BF16))
        operands.append(stack)
    return in_specs, out_specs, out_shapes, operands


def _split_refs(rest, n_cast, n_out):
    a, b = n_cast, n_cast + n_out
    return rest[:a], rest[a:b], rest[b:b + n_cast], rest[b + n_cast:]


def _ret_proj_kernel(x_ref, g_ref, w_ref, cos_ref, sin_ref, *rest, tn, n_cast):
    cast_src, (o_ref,), cast_dst, (h_ref,) = _split_refs(rest, n_cast, 1)
    j = pl.program_id(1)
    n_rot = 2 * RET_QK // tn

    @pl.when(j == 0)
    def _():
        _norm_rows(h_ref, x_ref, g_ref)

    @pl.when(j < n_rot)
    def _():
        _cast_blocks(cast_src, cast_dst)
        res = jnp.dot(h_ref[...], w_ref[...], preferred_element_type=F32)
        scale = jnp.where(j >= n_rot // 2, RET_DK ** -0.5, 1.0).astype(F32)
        c = cos_ref[...] * scale
        s = sin_ref[...] * scale
        half = RET_DK // 2
        for hh in range(tn // RET_DK):
            lo = hh * RET_DK
            t1 = res[:, lo:lo + half]
            t2 = res[:, lo + half:lo + RET_DK]
            o_ref[:, lo:lo + half] = (t1 * c - t2 * s).astype(BF16)
            o_ref[:, lo + half:lo + RET_DK] = (t2 * c + t1 * s).astype(BF16)

    @pl.when(j >= n_rot)
    def _():
        _cast_blocks(cast_src, cast_dst)
        o_ref[...] = jnp.dot(h_ref[...], w_ref[...],
                             preferred_element_type=F32).astype(BF16)


def ret_proj(x, norm_g, g_idx, w, cos, sin, cast_jobs, tm=1024, tn=2048):
    m = x.shape[0]
    nj = RET_IN // tn
    assert CAST_BLOCKS <= (m // tm) * nj
    c_in, c_out, c_shape, c_args = _cast_specs(cast_jobs, CAST_BLOCKS, lambda i, j: i * nj + j)
    return pl.pallas_call(
        functools.partial(_ret_proj_kernel, tn=tn, n_cast=len(cast_jobs)),
        grid=(m // tm, nj),
        in_specs=[pl.BlockSpec((tm, D_MODEL), lambda i, j: (i, 0)),
                  pl.BlockSpec((None, 1, D_MODEL), lambda i, j: (g_idx, 0, 0)),
                  pl.BlockSpec((D_MODEL, tn), lambda i, j: (0, j)),
                  pl.BlockSpec((tm, RET_DK // 2), lambda i, j: (i, 0)),
                  pl.BlockSpec((tm, RET_DK // 2), lambda i, j: (i, 0))] + c_in,
        out_specs=[pl.BlockSpec((tm, tn), lambda i, j: (i, j))] + c_out,
        out_shape=[jax.ShapeDtypeStruct((m, RET_IN), BF16)] + c_shape,
        scratch_shapes=[pltpu.VMEM((tm, D_MODEL), BF16)],
        compiler_params=_params(("arbitrary", "arbitrary")),
        name="ret_proj",
    )(x, norm_g, w, cos, sin, *c_args)


def _ret_log_gamma(h):
    return math.log1p(-(2.0 ** (-5.0 - h)))


OUT_PIECE = 256
RET_PIECE_HEADS = ((0, 2, 4, 6), (0, 2, 4, 6))
ML_PIECE_HEADS = ((0, 2, 4, 6), (0, 2, 4, 6))


def _ret_kernel(p_ref, x_ref, w_ref, gain_ref, o_ref,
                st_ref, dec_ref, xi_ref, zeta_ref, y_ref, *, nc):
    t = pl.program_id(0)
    slot = t % 2

    @pl.when(t == 0)
    def _():
        y_ref[...] = jnp.zeros_like(y_ref)

    @pl.when(t % nc == 0)
    def _():
        st_ref[...] = jnp.zeros_like(st_ref)
        n = lax.broadcasted_iota(jnp.int32, (RET_CHUNK, RET_CHUNK), 0)
        mcol = lax.broadcasted_iota(jnp.int32, (RET_CHUNK, RET_CHUNK), 1)
        rel = (n - mcol).astype(F32)
        row = lax.broadcasted_iota(jnp.int32, (RET_CHUNK, RET_DK), 0).astype(F32)
        for h in range(RET_HEADS):
            lg = _ret_log_gamma(h)
            dec_ref[h] = jnp.where(rel >= 0, jnp.exp(jnp.maximum(rel, 0.0) * lg), 0.0)
            xi_ref[h] = jnp.exp((row + 1.0) * lg).astype(BF16)
            zeta_ref[h] = jnp.exp((RET_CHUNK - 1.0 - row) * lg).astype(BF16)

    heads = range(RET_HEADS)
    k0, v0, g0 = RET_QK, 2 * RET_QK, 2 * RET_QK + RET_V
    q = [p_ref[:, h * RET_DK:(h + 1) * RET_DK] for h in heads]
    k = [p_ref[:, k0 + h * RET_DK:k0 + (h + 1) * RET_DK] for h in heads]
    v = [p_ref[:, v0 + h * RET_DV:v0 + (h + 1) * RET_DV] for h in heads]


    y_prev = y_ref[1 - slot]
    pieces = []

    def out_piece():
        p = len(pieces)
        pieces.append(jnp.dot(y_prev, w_ref[:, p * OUT_PIECE:(p + 1) * OUT_PIECE],
                              preferred_element_type=F32))

    raw = [lax.dot_general(q[h], k[h], NT_DIMS, preferred_element_type=F32) for h in heads]

    lhs, rhs, kz = [], [], []
    for h in heads:
        if h in RET_PIECE_HEADS[0]:
            out_piece()
        lhs.append(jnp.concatenate([(raw[h] * dec_ref[h]).astype(BF16), q[h] * xi_ref[h]],
                                   axis=1))
        rhs.append(jnp.concatenate([v[h], st_ref[h].astype(BF16)], axis=0))
        kz.append(k[h] * zeta_ref[h])

    y = []
    for h in heads:
        g_chunk = math.exp(RET_CHUNK * _ret_log_gamma(h))
        y.append(jnp.dot(lhs[h], rhs[h], preferred_element_type=F32))
        st_ref[h] = st_ref[h] * g_chunk + lax.dot_general(kz[h], v[h], TN_DIMS,
                                                          preferred_element_type=F32)

    for h in heads:
        if h in RET_PIECE_HEADS[1]:
            out_piece()
        cols = slice(h * RET_DV, (h + 1) * RET_DV)
        mu = jnp.mean(y[h], axis=-1, keepdims=True)
        yc = y[h] - mu
        var = jnp.mean(yc * yc, axis=-1, keepdims=True)
        yn = yc * lax.rsqrt(var + EPS)
        gh = p_ref[:, g0 + h * RET_DV:g0 + (h + 1) * RET_DV]
        y_ref[slot, :, cols] = (gh * jax.nn.sigmoid(gh)) * yn.astype(BF16)

    out = jnp.concatenate(pieces, axis=1)
    o_ref[...] = x_ref[...] + _rms(out, gain_ref[...])


def ret_mix_out(proj, x, w_out, norm_g, g_idx, batch, seq):
    assert sum(map(len, RET_PIECE_HEADS)) * OUT_PIECE == D_MODEL
    m = batch * seq
    n_chunks = m // RET_CHUNK
    cur = lambda t: jnp.minimum(t, n_chunks - 1)
    prev = lambda t: jnp.maximum(t - 1, 0)
    return pl.pallas_call(
        functools.partial(_ret_kernel, nc=seq // RET_CHUNK),
        grid=(n_chunks + 1,),
        in_specs=[pl.BlockSpec((RET_CHUNK, RET_IN), lambda t: (cur(t), 0)),
                  pl.BlockSpec((RET_CHUNK, D_MODEL), lambda t: (prev(t), 0)),
                  pl.BlockSpec((RET_V, D_MODEL), lambda t: (0, 0),
                               pipeline_mode=pl.Buffered(1)),
                  pl.BlockSpec((None, 1, D_MODEL), lambda t: (g_idx, 0, 0))],
        out_specs=pl.BlockSpec((RET_CHUNK, D_MODEL), lambda t: (prev(t), 0)),
        out_shape=jax.ShapeDtypeStruct((m, D_MODEL), F32),
        scratch_shapes=[pltpu.VMEM((RET_HEADS, RET_DK, RET_DV), F32),
                        pltpu.VMEM((RET_HEADS, RET_CHUNK, RET_CHUNK), F32),
                        pltpu.VMEM((RET_HEADS, RET_CHUNK, RET_DK), BF16),
                        pltpu.VMEM((RET_HEADS, RET_CHUNK, RET_DK), BF16),
                        pltpu.VMEM((2, RET_CHUNK, RET_V), BF16)],
        compiler_params=_params(("arbitrary",)),
        name="ret_mix_out",
    )(proj, x, w_out, norm_g)


def _log_sigmoid(x):
    return jnp.minimum(x, 0.0) - jnp.log1p(jnp.exp(-jnp.abs(x)))


def _gate_logs(pre, is_forget):
    capped = GATE_SOFTCAP * jnp.tanh(pre / GATE_SOFTCAP)
    return jnp.where(is_forget, _log_sigmoid(capped), capped)


def _ml_proj_kernel(x_ref, g_ref, w_ref, wg_ref, b_ref, *rest, n_cast):
    cast_src, (o_ref, gcol_ref, grow_ref), cast_dst, (h_ref,) = _split_refs(rest, n_cast, 3)
    j = pl.program_id(1)

    def main_tile():
        _cast_blocks(cast_src, cast_dst)
        o_ref[...] = jnp.dot(h_ref[...], w_ref[...], preferred_element_type=F32).astype(BF16)

    @pl.when(j == 0)
    def _():
        _norm_rows(h_ref, x_ref, g_ref)
        pre = jnp.dot(h_ref[...], wg_ref[...], preferred_element_type=F32) + b_ref[...]
        lane = lax.broadcasted_iota(jnp.int32, pre.shape, 1)
        logs = _gate_logs(pre, lane >= ML_HEADS)
        gcol_ref[...] = logs
        grow_ref[...] = logs.T[:2 * ML_HEADS, :]
        main_tile()

    @pl.when(j > 0)
    def _():
        main_tile()


def ml_proj(x, norm_g, g_idx, w, layer, wg, b, cast_jobs, tm=1024, tn=2048):
    m = x.shape[0]
    ng = 2 * ML_HEADS
    nj = ML_MAIN // tn
    n_blocks = CAST_BLOCKS // 2
    assert n_blocks <= (m // tm) * nj
    c_in, c_out, c_shape, c_args = _cast_specs(cast_jobs, n_blocks, lambda i, j: i * nj + j)
    return pl.pallas_call(
        functools.partial(_ml_proj_kernel, n_cast=len(cast_jobs)),
        grid=(m // tm, nj),
        in_specs=[pl.BlockSpec((tm, D_MODEL), lambda i, j: (i, 0)),
                  pl.BlockSpec((None, 1, D_MODEL), lambda i, j: (g_idx, 0, 0)),
                  pl.BlockSpec((None, D_MODEL, tn), lambda i, j: (layer, 0, j)),
                  pl.BlockSpec((D_MODEL, GATE_LANES), lambda i, j: (0, 0)),
                  pl.BlockSpec((1, GATE_LANES), lambda i, j: (0, 0))] + c_in,
        out_specs=[pl.BlockSpec((tm, tn), lambda i, j: (i, j)),
                   pl.BlockSpec((tm, GATE_LANES), lambda i, j: (i, 0)),
                   pl.BlockSpec((ng, tm), lambda i, j: (0, i))] + c_out,
        out_shape=[jax.ShapeDtypeStruct((m, ML_MAIN), BF16),
                   jax.ShapeDtypeStruct((m, GATE_LANES), F32),
                   jax.ShapeDtypeStruct((ng, m), F32)] + c_shape,
        scratch_shapes=[pltpu.VMEM((tm, D_MODEL), BF16)],
        compiler_params=_params(("arbitrary", "arbitrary")),
        name="ml_proj",
    )(x, norm_g, w, wg, b, *c_args)


def _ml_kernel(p_ref, gcol_ref, grow_ref, ng_ref, x_ref, w_ref, gain_ref, *rest, nc, n_cast):
    cast_src, (out_ref,), cast_dst, (c_ref, n_ref, m_ref, y_ref) = _split_refs(rest, n_cast, 1)
    t = pl.program_id(0)
    slot = t % 2

    @pl.when(t == 0)
    def _():
        y_ref[...] = jnp.zeros_like(y_ref)

    @pl.when(t % nc == 0)
    def _():
        c_ref[...] = jnp.zeros_like(c_ref)
        n_ref[...] = jnp.zeros_like(n_ref)
        m_ref[...] = jnp.zeros_like(m_ref)

    _cast_blocks(cast_src, cast_dst)

    s_idx = lax.broadcasted_iota(jnp.int32, (CHUNK, CHUNK), 0)
    n_idx = lax.broadcasted_iota(jnp.int32, (CHUNK, CHUNK), 1)
    causal = s_idx <= n_idx
    eye = s_idx == n_idx
    lower = (s_idx >= n_idx).astype(F32)
    upper = causal.astype(F32)

    gcol = gcol_ref[...]
    grow = grow_ref[...]
    hi = lax.Precision.HIGHEST
    cum_col = jnp.dot(lower, gcol, precision=hi, preferred_element_type=F32)
    cum_row = jnp.dot(grow, upper, precision=hi, preferred_element_type=F32)
    key_col = gcol - pltpu.roll(cum_col, GATE_LANES - ML_HEADS, axis=1)
    scale = ML_DQK ** -0.5
    log_scale = math.log(scale)

    heads = range(ML_HEADS)
    k0, v0, o0 = ML_QK, 2 * ML_QK, 2 * ML_QK + ML_V
    q = [p_ref[:, h * ML_DQK:(h + 1) * ML_DQK] for h in heads]
    k = [p_ref[:, k0 + h * ML_DQK:k0 + (h + 1) * ML_DQK] for h in heads]
    v = [p_ref[:, v0 + h * ML_DV:v0 + (h + 1) * ML_DV] for h in heads]


    raw, qn, qc = [], [], []
    for h in heads:
        raw.append(lax.dot_general(k[h], q[h], NT_DIMS, preferred_element_type=F32))
        qn.append(lax.dot_general(n_ref[h].astype(BF16), q[h], NT_DIMS,
                                  preferred_element_type=F32)[0:1, :])
        qc.append(jnp.dot(q[h], c_ref[h].astype(BF16),
                          preferred_element_type=F32).astype(BF16))

    y_prev = y_ref[1 - slot]
    pieces = []

    def out_piece():
        p = len(pieces)
        pieces.append(jnp.dot(y_prev, w_ref[:, p * OUT_PIECE:(p + 1) * OUT_PIECE],
                              preferred_element_type=F32))

    lhs_t, kw_t, w_rows, carry, m_new = [], [], [], [], []
    for h in heads:
        if h in ML_PIECE_HEADS[0]:
            out_piece()
        brow = cum_row[ML_HEADS + h:ML_HEADS + h + 1, :]
        irow = grow[h:h + 1, :]
        kcol = key_col[:, h:h + 1]
        m_st = m_ref[h][0:1, 0:1]

        d_log = jnp.where(causal, brow + kcol, -jnp.inf)
        inter_log = brow + m_st
        m_out = jnp.maximum(inter_log, jnp.max(d_log, axis=0, keepdims=True))
        d_w = jnp.exp(d_log - (m_out - log_scale))
        inter_w = jnp.exp(inter_log - m_out)
        scores = raw[h] * d_w
        den = jnp.sum(scores, axis=0, keepdims=True) + inter_w * qn[h]
        r = 1.0 / jnp.maximum(jnp.abs(den), jnp.exp(-m_out))
        p_t = (scores * r).astype(BF16)
        diag = jnp.where(eye, inter_w * r, 0.0).astype(BF16)
        lhs_t.append(jnp.concatenate([p_t, diag], axis=0))

        b_tot = brow[:, CHUNK - 1:CHUNK]
        w_log = b_tot - brow + irow
        m_nw = jnp.maximum(b_tot + m_st, jnp.max(w_log, axis=1, keepdims=True))
        w_s = jnp.exp(w_log - m_nw) * scale
        carry.append(jnp.exp(b_tot + m_st - m_nw))
        m_new.append(m_nw)
        kw_t.append((k[h].astype(F32).T * w_s).astype(BF16))
        w_rows.append(jnp.broadcast_to(w_s, (8, CHUNK)).astype(BF16))

    h_out = []
    for h in heads:
        rhs = jnp.concatenate([v[h], qc[h]], axis=0)
        h_out.append(lax.dot_general(lhs_t[h], rhs, TN_DIMS, preferred_element_type=F32))
        c_ref[h] = carry[h] * c_ref[h] + jnp.dot(kw_t[h], v[h], preferred_element_type=F32)
        n_ref[h] = carry[h] * n_ref[h] + jnp.dot(w_rows[h], k[h], preferred_element_type=F32)
        m_ref[h] = jnp.broadcast_to(m_new[h], m_ref.shape[1:])

    for h in heads:
        if h in ML_PIECE_HEADS[1]:
            out_piece()
        cols = slice(h * ML_DV, (h + 1) * ML_DV)
        y = h_out[h]
        yn = y * lax.rsqrt(jnp.mean(y * y, axis=-1, keepdims=True) + EPS) * ng_ref[:, cols]
        gate = jax.nn.sigmoid(p_ref[:, o0 + h * ML_DV:o0 + (h + 1) * ML_DV])
        y_ref[slot, :, cols] = yn.astype(BF16) * gate

    out = jnp.concatenate(pieces, axis=1)
    out_ref[...] = x_ref[...] + _rms(out, gain_ref[...])


def ml_mix_out(proj, gcol, grow, head_g, x, w_out, norm_g, g_idx, cast_jobs, batch, seq):
    assert sum(map(len, ML_PIECE_HEADS)) * OUT_PIECE == D_MODEL
    m = batch * seq
    n_chunks = m // CHUNK
    cur = lambda t: jnp.minimum(t, n_chunks - 1)
    prev = lambda t: jnp.maximum(t - 1, 0)
    assert CAST_BLOCKS <= n_chunks + 1
    c_in, c_out, c_shape, c_args = _cast_specs(cast_jobs, CAST_BLOCKS, lambda t: t)
    return pl.pallas_call(
        functools.partial(_ml_kernel, nc=seq // CHUNK, n_cast=len(cast_jobs)),
        grid=(n_chunks + 1,),
        in_specs=[pl.BlockSpec((CHUNK, ML_MAIN), lambda t: (cur(t), 0)),
                  pl.BlockSpec((CHUNK, GATE_LANES), lambda t: (cur(t), 0)),
                  pl.BlockSpec((2 * ML_HEADS, CHUNK), lambda t: (0, cur(t))),
                  pl.BlockSpec((1, ML_V), lambda t: (0, 0)),
                  pl.BlockSpec((CHUNK, D_MODEL), lambda t: (prev(t), 0)),
                  pl.BlockSpec((ML_V, D_MODEL), lambda t: (0, 0),
                               pipeline_mode=pl.Buffered(1)),
                  pl.BlockSpec((None, 1, D_MODEL), lambda t: (g_idx, 0, 0))] + c_in,
        out_specs=[pl.BlockSpec((CHUNK, D_MODEL), lambda t: (prev(t), 0))] + c_out,
        out_shape=[jax.ShapeDtypeStruct((m, D_MODEL), F32)] + c_shape,
        scratch_shapes=[pltpu.VMEM((ML_HEADS, ML_DQK, ML_DV), F32),
                        pltpu.VMEM((ML_HEADS, 8, ML_DQK), F32),
                        pltpu.VMEM((ML_HEADS, 8, 128), F32),
                        pltpu.VMEM((2, CHUNK, ML_V), BF16)],
        compiler_params=_params(("arbitrary",)),
        name="ml_mix_out",
    )(proj, gcol, grow, head_g, x, w_out, norm_g, *c_args)


def _mlp_kernel(xn_ref, xr_ref, g_in_ref, w1_ref, w2_ref, g_out_ref, o_ref,
                h_even, h_odd, acc_even, acc_odd, *, nt):
    i = pl.program_id(0)
    f = pl.program_id(1)
    n_rows = xn_ref.shape[0]
    rows = pl.ds(pl.multiple_of(f * n_rows, n_rows), n_rows)
    h_bufs = (h_even, h_odd)
    acc_bufs = (acc_even, acc_odd)

    def norm_rows(h_dst):
        h_dst[rows, :] = _rms(xn_ref[...], g_in_ref[...]).astype(BF16)

    def finish_rows(acc_src):
        o_ref[...] = xr_ref[...] + _rms(acc_src[rows, :], g_out_ref[...])

    def chunk(h_src, acc_dst, first):
        hid = jnp.maximum(jnp.dot(h_src[...], w1_ref[...], preferred_element_type=F32), 0.0)
        out = jnp.dot((hid * hid).astype(BF16), w2_ref[...], preferred_element_type=F32)
        if first:
            acc_dst[...] = out
        else:
            acc_dst[...] += out

    @pl.when(i == 0)
    def _():
        norm_rows(h_even)
        acc_odd[rows, :] = jnp.zeros((n_rows, acc_odd.shape[1]), F32)
        o_ref[...] = xr_ref[...]

    in_flight = jnp.logical_and(i >= 1, i <= nt)
    for parity in (0, 1):
        mine = jnp.logical_and(in_flight, (i - 1) % 2 == parity)

        @pl.when(jnp.logical_and(mine, f == 0))
        def _(parity=parity):
            finish_rows(acc_bufs[1 - parity])
            chunk(h_bufs[parity], acc_bufs[parity], True)
            norm_rows(h_bufs[1 - parity])

        @pl.when(jnp.logical_and(mine, f > 0))
        def _(parity=parity):
            finish_rows(acc_bufs[1 - parity])
            chunk(h_bufs[parity], acc_bufs[parity], False)
            norm_rows(h_bufs[1 - parity])

    @pl.when(i == nt + 1)
    def _():
        finish_rows(acc_bufs[(nt - 1) % 2])


def mlp(x, norm_g, g_in_idx, g_out_idx, w1, w2, tm=1024, tf=1024):
    m = x.shape[0]
    nt, nf = m // tm, D_FF // tf
    rows = tm // nf
    clamp = lambda t: jnp.clip(t, 0, nt - 1)
    chunk_of = lambda i, f: jnp.where(jnp.logical_and(i >= 1, i <= nt), f, 0)
    done_rows = lambda i, f: jnp.where(i >= 2, clamp(i - 2) * nf + f, 0)
    return pl.pallas_call(
        functools.partial(_mlp_kernel, nt=nt),
        grid=(nt + 2, nf),
        in_specs=[pl.BlockSpec((rows, D_MODEL), lambda i, f: (clamp(i) * nf + f, 0)),
                  pl.BlockSpec((rows, D_MODEL), lambda i, f: (done_rows(i, f), 0)),
                  pl.BlockSpec((None, 1, D_MODEL), lambda i, f: (g_in_idx, 0, 0)),
                  pl.BlockSpec((D_MODEL, tf), lambda i, f: (0, chunk_of(i, f))),
                  pl.BlockSpec((tf, D_MODEL), lambda i, f: (chunk_of(i, f), 0)),
                  pl.BlockSpec((None, 1, D_MODEL), lambda i, f: (g_out_idx, 0, 0))],
        out_specs=pl.BlockSpec((rows, D_MODEL), lambda i, f: (done_rows(i, f), 0)),
        out_shape=jax.ShapeDtypeStruct((m, D_MODEL), F32),
        scratch_shapes=[pltpu.VMEM((tm, D_MODEL), BF16), pltpu.VMEM((tm, D_MODEL), BF16),
                        pltpu.VMEM((tm, D_MODEL), F32), pltpu.VMEM((tm, D_MODEL), F32)],
        compiler_params=_params(("arbitrary", "arbitrary")),
        name="mlp",
    )(x, x, norm_g, w1, w2, norm_g)


def kernel(x, positions, norm_g, ret_w_in, ret_w_out, mlstm_w_in, mlstm_b_gate,
           mlstm_norm_g, mlstm_w_out, mlp_w1, mlp_w2):
    batch, seq, d = x.shape
    m = batch * seq
    xf = x.reshape(m, d)

    inv_freq = jnp.power(ROPE_BASE, -jnp.linspace(0.0, 1.0, RET_DK // 2, dtype=F32))
    cos, sin, ret_w_in_b = rope_tables(positions, inv_freq, [(ret_w_in, 0)])
    gains = norm_g.astype(F32).reshape(DEPTH * 4, 1, d)
    ml_w_in_b = mlstm_w_in.astype(BF16)
    n_gate = 2 * ML_HEADS

    for i in range(DEPTH):
        j = i // 2
        if i % 2 == 0:
            proj, w1_b, w2_b, w_out_b = ret_proj(
                xf, gains, 4 * i, ret_w_in_b, cos, sin,
                [(mlp_w1, i), (mlp_w2, i), (ret_w_out, j)])
            xf = ret_mix_out(proj, xf, w_out_b, gains, 4 * i + 1, batch, seq)
        else:
            w_gate = ml_w_in_b[j, :, ML_MAIN:]
            wg = jnp.pad(w_gate, ((0, 0), (0, GATE_LANES - n_gate)))
            bias = mlstm_b_gate[j].astype(F32)
            b = jnp.pad(bias, (0, GATE_LANES - n_gate)).reshape(1, GATE_LANES)
            proj, gcol, grow, w_out_b = ml_proj(xf, gains, 4 * i, ml_w_in_b, j, wg, b,
                                                [(mlstm_w_out, j)])
            head_g = mlstm_norm_g[j].reshape(1, ML_V).astype(F32)
            jobs = [(mlp_w1, i), (mlp_w2, i)]
            if i + 1 < DEPTH:
                jobs.append((ret_w_in, j + 1))
            xf, w1_b, w2_b, *nxt = ml_mix_out(proj, gcol, grow, head_g, xf, w_out_b, gains,
                                              4 * i + 1, jobs, batch, seq)
            if nxt:
                ret_w_in_b = nxt[0]
        xf = mlp(xf, gains, 4 * i + 2, 4 * i + 3, w1_b, w2_b)
    return xf.reshape(batch, seq, d)
```

```python
import functools
import math

import jax
import jax.numpy as jnp
from jax import lax
from jax.experimental import pallas as pl
from jax.experimental.pallas import tpu as pltpu

F32 = jnp.float32
BF16 = jnp.bfloat16

D_MODEL = 2048
DEPTH = 4
CHUNK = 256
RET_CHUNK = 256
EPS = 1e-6
D_FF = 4 * D_MODEL

RET_HEADS = 8
RET_DK = 256
RET_QK = RET_HEADS * RET_DK
RET_V = 2 * D_MODEL
RET_DV = RET_V // RET_HEADS
RET_IN = 2 * RET_QK + 2 * RET_V
ROPE_BASE = 10000.0

ML_HEADS = 8
ML_QK = D_MODEL // 2
ML_DQK = ML_QK // ML_HEADS
ML_V = D_MODEL
ML_DV = ML_V // ML_HEADS
ML_MAIN = 2 * ML_QK + 2 * ML_V
GATE_SOFTCAP = 15.0
GATE_LANES = 128

VMEM_LIMIT = 60 * 1024 * 1024

NT_DIMS = (((1,), (1,)), ((), ()))
TN_DIMS = (((0,), (0,)), ((), ()))


def _rms(x, g):
    ms = jnp.mean(x * x, axis=-1, keepdims=True)
    return x * lax.rsqrt(ms + EPS) * g


CAST_BLOCKS = 64
NORM_ROWS = 256
NORM_SUB = 64


def _norm_rows(dst_ref, src_ref, g_ref):
    g = g_ref[...]

    def body(r, carry):
        rows = pl.ds(pl.multiple_of(r * NORM_ROWS, NORM_ROWS), NORM_ROWS)
        sq = src_ref[rows, :]
        inv = lax.rsqrt(jnp.mean(sq * sq, axis=-1, keepdims=True) + EPS)
        for t in range(NORM_ROWS // NORM_SUB):
            sub = pl.ds(pl.multiple_of(r * NORM_ROWS + t * NORM_SUB, NORM_SUB), NORM_SUB)
            v = src_ref[sub, :] * inv[t * NORM_SUB:(t + 1) * NORM_SUB] * g
            dst_ref[sub, :] = v.astype(dst_ref.dtype)
        return carry

    lax.fori_loop(0, src_ref.shape[0] // NORM_ROWS, body, 0)


def _params(sem, **kw):
    return pltpu.CompilerParams(dimension_semantics=sem, vmem_limit_bytes=VMEM_LIMIT, **kw)


def _rope_kernel(pos_ref, freq_ref, *rest, n_cast):
    cast_src, (cos_ref, sin_ref), cast_dst, _ = _split_refs(rest, n_cast, 2)
    _cast_blocks(cast_src, cast_dst)
    ang = pos_ref[...].astype(F32) * freq_ref[...]
    cos_ref[...] = jnp.cos(ang)
    sin_ref[...] = jnp.sin(ang)


def rope_tables(positions, inv_freq, cast_jobs, tm=1024):
    m = positions.size
    pos = positions.reshape(m, 1)
    half = inv_freq.shape[-1]
    n_steps = m // tm
    c_in, c_out, c_shape, c_args = _cast_specs(cast_jobs, n_steps, lambda i: i)
    return pl.pallas_call(
        functools.partial(_rope_kernel, n_cast=len(cast_jobs)),
        grid=(n_steps,),
        in_specs=[pl.BlockSpec((tm, 1), lambda i: (i, 0)),
                  pl.BlockSpec((1, half), lambda i: (0, 0))] + c_in,
        out_specs=[pl.BlockSpec((tm, half), lambda i: (i, 0)),
                   pl.BlockSpec((tm, half), lambda i: (i, 0))] + c_out,
        out_shape=[jax.ShapeDtypeStruct((m, half), F32)] * 2 + c_shape,
        compiler_params=_params(("arbitrary",)),
        name="rope_tables",
    )(pos, inv_freq.reshape(1, half), *c_args)


def _cast_blocks(src_refs, dst_refs):
    for src, dst in zip(src_refs, dst_refs):
        dst[...] = src[...].astype(BF16)


def _cast_specs(jobs, n_blocks, step_of):
    blk = lambda *ids: jnp.minimum(step_of(*ids), n_blocks - 1)
    in_specs, out_specs, out_shapes, operands = [], [], [], []
    for stack, layer in jobs:
        _, r, c = stack.shape
        rows = r // n_blocks
        in_specs.append(pl.BlockSpec((None, rows, c),
                                     lambda *ids, layer=layer: (layer, blk(*ids), 0)))
        out_specs.append(pl.BlockSpec((rows, c), lambda *ids: (blk(*ids), 0)))
        out_shapes.append(jax.ShapeDtypeStruct((r, c), BF16))
        operands.append(stack)
    return in_specs, out_specs, out_shapes, operands


def _split_refs(rest, n_cast, n_out):
    a, b = n_cast, n_cast + n_out
    return rest[:a], rest[a:b], rest[b:b + n_cast], rest[b + n_cast:]


def _ret_proj_kernel(x_ref, g_ref, w_ref, cos_ref, sin_ref, *rest, tn, n_cast):
    cast_src, (o_ref,), cast_dst, (h_ref,) = _split_refs(rest, n_cast, 1)
    j = pl.program_id(1)
    n_rot = 2 * RET_QK // tn

    @pl.when(j == 0)
    def _():
        _norm_rows(h_ref, x_ref, g_ref)

    @pl.when(j < n_rot)
    def _():
        _cast_blocks(cast_src, cast_dst)
        res = jnp.dot(h_ref[...], w_ref[...], preferred_element_type=F32)
        scale = jnp.where(j >= n_rot // 2, RET_DK ** -0.5, 1.0).astype(F32)
        c = cos_ref[...] * scale
        s = sin_ref[...] * scale
        half = RET_DK // 2
        for hh in range(tn // RET_DK):
            lo = hh * RET_DK
            t1 = res[:, lo:lo + half]
            t2 = res[:, lo + half:lo + RET_DK]
            o_ref[:, lo:lo + half] = (t1 * c - t2 * s).astype(BF16)
            o_ref[:, lo + half:lo + RET_DK] = (t2 * c + t1 * s).astype(BF16)

    @pl.when(j >= n_rot)
    def _():
        _cast_blocks(cast_src, cast_dst)
        o_ref[...] = jnp.dot(h_ref[...], w_ref[...],
                             preferred_element_type=F32).astype(BF16)


def ret_proj(x, norm_g, g_idx, w, cos, sin, cast_jobs, tm=1024, tn=2048):
    m = x.shape[0]
    nj = RET_IN // tn
    assert CAST_BLOCKS <= (m // tm) * nj
    c_in, c_out, c_shape, c_args = _cast_specs(cast_jobs, CAST_BLOCKS, lambda i, j: i * nj + j)
    return pl.pallas_call(
        functools.partial(_ret_proj_kernel, tn=tn, n_cast=len(cast_jobs)),
        grid=(m // tm, nj),
        in_specs=[pl.BlockSpec((tm, D_MODEL), lambda i, j: (i, 0)),
                  pl.BlockSpec((None, 1, D_MODEL), lambda i, j: (g_idx, 0, 0)),
                  pl.BlockSpec((D_MODEL, tn), lambda i, j: (0, j)),
                  pl.BlockSpec((tm, RET_DK // 2), lambda i, j: (i, 0)),
                  pl.BlockSpec((tm, RET_DK // 2), lambda i, j: (i, 0))] + c_in,
        out_specs=[pl.BlockSpec((tm, tn), lambda i, j: (i, j))] + c_out,
        out_shape=[jax.ShapeDtypeStruct((m, RET_IN), BF16)] + c_shape,
        scratch_shapes=[pltpu.VMEM((tm, D_MODEL), BF16)],
        compiler_params=_params(("arbitrary", "arbitrary")),
        name="ret_proj",
    )(x, norm_g, w, cos, sin, *c_args)


def _ret_log_gamma(h):
    return math.log1p(-(2.0 ** (-5.0 - h)))


OUT_PIECE = 256
RET_PIECE_HEADS = ((0, 2, 4, 6), (0, 2, 4, 6))
ML_PIECE_HEADS = ((0, 2, 4, 6), (0, 2, 4, 6))


def _skewed_steps(step, n_steps):
    t = pl.program_id(0)
    pl.when(t == 0)(functools.partial(step, do_mix=True, do_out=False))
    pl.when(jnp.logical_and(t > 0, t < n_steps - 1))(
        functools.partial(step, do_mix=True, do_out=True))
    pl.when(t == n_steps - 1)(functools.partial(step, do_mix=False, do_out=True))


def _ret_kernel(*refs, nc, n_steps):
    _skewed_steps(functools.partial(_ret_step, *refs, nc=nc), n_steps)


def _ret_step(p_ref, x_ref, w_ref, gain_ref, o_ref,
              st_ref, dec_ref, xi_ref, zeta_ref, y_ref, *, nc, do_mix, do_out):
    t = pl.program_id(0)
    slot = t % 2

    @pl.when(t == 0)
    def _():
        y_ref[...] = jnp.zeros_like(y_ref)

    @pl.when(t % nc == 0)
    def _():
        st_ref[...] = jnp.zeros_like(st_ref)
        n = lax.broadcasted_iota(jnp.int32, (RET_CHUNK, RET_CHUNK), 0)
        mcol = lax.broadcasted_iota(jnp.int32, (RET_CHUNK, RET_CHUNK), 1)
        rel = (n - mcol).astype(F32)
        row = lax.broadcasted_iota(jnp.int32, (RET_CHUNK, RET_DK), 0).astype(F32)
        for h in range(RET_HEADS):
            lg = _ret_log_gamma(h)
            dec_ref[h] = jnp.where(rel >= 0, jnp.exp(jnp.maximum(rel, 0.0) * lg), 0.0)
            xi_ref[h] = jnp.exp((row + 1.0) * lg).astype(BF16)
            zeta_ref[h] = jnp.exp((RET_CHUNK - 1.0 - row) * lg).astype(BF16)

    heads = range(RET_HEADS if do_mix else 0)
    k0, v0, g0 = RET_QK, 2 * RET_QK, 2 * RET_QK + RET_V
    q = [p_ref[:, h * RET_DK:(h + 1) * RET_DK] for h in heads]
    k = [p_ref[:, k0 + h * RET_DK:k0 + (h + 1) * RET_DK] for h in heads]
    v = [p_ref[:, v0 + h * RET_DV:v0 + (h + 1) * RET_DV] for h in heads]


    y_prev = y_ref[1 - slot]
    pieces = []

    def out_piece():
        if do_out:
            p = len(pieces)
            pieces.append(jnp.dot(y_prev, w_ref[:, p * OUT_PIECE:(p + 1) * OUT_PIECE],
                                  preferred_element_type=F32))

    raw = [lax.dot_general(q[h], k[h], NT_DIMS, preferred_element_type=F32) for h in heads]

    lhs, rhs, kz = [], [], []
    for h in heads:
        if h in RET_PIECE_HEADS[0]:
            out_piece()
        lhs.append(jnp.concatenate([(raw[h] * dec_ref[h]).astype(BF16), q[h] * xi_ref[h]],
                                   axis=1))
        rhs.append(jnp.concatenate([v[h], st_ref[h].astype(BF16)], axis=0))
        kz.append(k[h] * zeta_ref[h])

    y = []
    for h in heads:
        g_chunk = math.exp(RET_CHUNK * _ret_log_gamma(h))
        y.append(jnp.dot(lhs[h], rhs[h], preferred_element_type=F32))
        st_ref[h] = st_ref[h] * g_chunk + lax.dot_general(kz[h], v[h], TN_DIMS,
                                                          preferred_element_type=F32)

    for h in heads:
        if h in RET_PIECE_HEADS[1]:
            out_piece()
        cols = slice(h * RET_DV, (h + 1) * RET_DV)
        mu = jnp.mean(y[h], axis=-1, keepdims=True)
        yc = y[h] - mu
        var = jnp.mean(yc * yc, axis=-1, keepdims=True)
        yn = yc * lax.rsqrt(var + EPS)
        gh = p_ref[:, g0 + h * RET_DV:g0 + (h + 1) * RET_DV]
        y_ref[slot, :, cols] = (gh * jax.nn.sigmoid(gh)) * yn.astype(BF16)

    if do_out:
        while len(pieces) * OUT_PIECE < D_MODEL:
            out_piece()
        out = jnp.concatenate(pieces, axis=1)
        o_ref[...] = x_ref[...] + _rms(out, gain_ref[...])


def ret_mix_out(proj, x, w_out, norm_g, g_idx, batch, seq):
    assert sum(map(len, RET_PIECE_HEADS)) * OUT_PIECE == D_MODEL
    m = batch * seq
    n_chunks = m // RET_CHUNK
    cur = lambda t: jnp.minimum(t, n_chunks - 1)
    prev = lambda t: jnp.maximum(t - 1, 0)
    return pl.pallas_call(
        functools.partial(_ret_kernel, nc=seq // RET_CHUNK, n_steps=n_chunks + 1),
        grid=(n_chunks + 1,),
        in_specs=[pl.BlockSpec((RET_CHUNK, RET_IN), lambda t: (cur(t), 0)),
                  pl.BlockSpec((RET_CHUNK, D_MODEL), lambda t: (prev(t), 0)),
                  pl.BlockSpec((RET_V, D_MODEL), lambda t: (0, 0),
                               pipeline_mode=pl.Buffered(1)),
                  pl.BlockSpec((None, 1, D_MODEL), lambda t: (g_idx, 0, 0))],
        out_specs=pl.BlockSpec((RET_CHUNK, D_MODEL), lambda t: (prev(t), 0)),
        out_shape=jax.ShapeDtypeStruct((m, D_MODEL), F32),
        scratch_shapes=[pltpu.VMEM((RET_HEADS, RET_DK, RET_DV), F32),
                        pltpu.VMEM((RET_HEADS, RET_CHUNK, RET_CHUNK), F32),
                        pltpu.VMEM((RET_HEADS, RET_CHUNK, RET_DK), BF16),
                        pltpu.VMEM((RET_HEADS, RET_CHUNK, RET_DK), BF16),
                        pltpu.VMEM((2, RET_CHUNK, RET_V), BF16)],
        compiler_params=_params(("arbitrary",)),
        name="ret_mix_out",
    )(proj, x, w_out, norm_g)


def _log_sigmoid(x):
    return jnp.minimum(x, 0.0) - jnp.log1p(jnp.exp(-jnp.abs(x)))


def _gate_logs(pre, is_forget):
    capped = GATE_SOFTCAP * jnp.tanh(pre / GATE_SOFTCAP)
    return jnp.where(is_forget, _log_sigmoid(capped), capped)


def _ml_proj_kernel(x_ref, g_ref, w_ref, wg_ref, b_ref, *rest, n_cast):
    cast_src, (o_ref, gcol_ref, grow_ref), cast_dst, (h_ref,) = _split_refs(rest, n_cast, 3)
    j = pl.program_id(1)

    def main_tile():
        _cast_blocks(cast_src, cast_dst)
        o_ref[...] = jnp.dot(h_ref[...], w_ref[...], preferred_element_type=F32).astype(BF16)

    @pl.when(j == 0)
    def _():
        _norm_rows(h_ref, x_ref, g_ref)
        pre = jnp.dot(h_ref[...], wg_ref[...], preferred_element_type=F32) + b_ref[...]
        lane = lax.broadcasted_iota(jnp.int32, pre.shape, 1)
        logs = _gate_logs(pre, lane >= ML_HEADS)
        gcol_ref[...] = logs
        grow_ref[...] = logs.T[:2 * ML_HEADS, :]
        main_tile()

    @pl.when(j > 0)
    def _():
        main_tile()


def ml_proj(x, norm_g, g_idx, w, layer, wg, b, cast_jobs, tm=1024, tn=2048):
    m = x.shape[0]
    ng = 2 * ML_HEADS
    nj = ML_MAIN // tn
    n_blocks = CAST_BLOCKS // 2
    assert n_blocks <= (m // tm) * nj
    c_in, c_out, c_shape, c_args = _cast_specs(cast_jobs, n_blocks, lambda i, j: i * nj + j)
    return pl.pallas_call(
        functools.partial(_ml_proj_kernel, n_cast=len(cast_jobs)),
        grid=(m // tm, nj),
        in_specs=[pl.BlockSpec((tm, D_MODEL), lambda i, j: (i, 0)),
                  pl.BlockSpec((None, 1, D_MODEL), lambda i, j: (g_idx, 0, 0)),
                  pl.BlockSpec((None, D_MODEL, tn), lambda i, j: (layer, 0, j)),
                  pl.BlockSpec((D_MODEL, GATE_LANES), lambda i, j: (0, 0)),
                  pl.BlockSpec((1, GATE_LANES), lambda i, j: (0, 0))] + c_in,
        out_specs=[pl.BlockSpec((tm, tn), lambda i, j: (i, j)),
                   pl.BlockSpec((tm, GATE_LANES), lambda i, j: (i, 0)),
                   pl.BlockSpec((ng, tm), lambda i, j: (0, i))] + c_out,
        out_shape=[jax.ShapeDtypeStruct((m, ML_MAIN), BF16),
                   jax.ShapeDtypeStruct((m, GATE_LANES), F32),
                   jax.ShapeDtypeStruct((ng, m), F32)] + c_shape,
        scratch_shapes=[pltpu.VMEM((tm, D_MODEL), BF16)],
        compiler_params=_params(("arbitrary", "arbitrary")),
        name="ml_proj",
    )(x, norm_g, w, wg, b, *c_args)


def _ml_kernel(p_ref, gcol_ref, grow_ref, ng_ref, x_ref, w_ref, gain_ref, *rest, nc, n_cast):
    cast_src, (out_ref,), cast_dst, (c_ref, n_ref, m_ref, y_ref) = _split_refs(rest, n_cast, 1)
    t = pl.program_id(0)
    slot = t % 2

    @pl.when(t == 0)
    def _():
        y_ref[...] = jnp.zeros_like(y_ref)

    @pl.when(t % nc == 0)
    def _():
        c_ref[...] = jnp.zeros_like(c_ref)
        n_ref[...] = jnp.zeros_like(n_ref)
        m_ref[...] = jnp.zeros_like(m_ref)

    _cast_blocks(cast_src, cast_dst)

    s_idx = lax.broadcasted_iota(jnp.int32, (CHUNK, CHUNK), 0)
    n_idx = lax.broadcasted_iota(jnp.int32, (CHUNK, CHUNK), 1)
    causal = s_idx <= n_idx
    eye = s_idx == n_idx
    lower = (s_idx >= n_idx).astype(F32)
    upper = causal.astype(F32)

    gcol = gcol_ref[...]
    grow = grow_ref[...]
    hi = lax.Precision.HIGHEST
    cum_col = jnp.dot(lower, gcol, precision=hi, preferred_element_type=F32)
    cum_row = jnp.dot(grow, upper, precision=hi, preferred_element_type=F32)
    key_col = gcol - pltpu.roll(cum_col, GATE_LANES - ML_HEADS, axis=1)
    scale = ML_DQK ** -0.5
    log_scale = math.log(scale)

    heads = range(ML_HEADS)
    k0, v0, o0 = ML_QK, 2 * ML_QK, 2 * ML_QK + ML_V
    q = [p_ref[:, h * ML_DQK:(h + 1) * ML_DQK] for h in heads]
    k = [p_ref[:, k0 + h * ML_DQK:k0 + (h + 1) * ML_DQK] for h in heads]
    v = [p_ref[:, v0 + h * ML_DV:v0 + (h + 1) * ML_DV] for h in heads]


    raw, qn, qc = [], [], []
    for h in heads:
        raw.append(lax.dot_general(k[h], q[h], NT_DIMS, preferred_element_type=F32))
        qn.append(lax.dot_general(n_ref[h].astype(BF16), q[h], NT_DIMS,
                                  preferred_element_type=F32)[0:1, :])
        qc.append(jnp.dot(q[h], c_ref[h].astype(BF16),
                          preferred_element_type=F32).astype(BF16))

    y_prev = y_ref[1 - slot]
    pieces = []

    def out_piece():
        p = len(pieces)
        pieces.append(jnp.dot(y_prev, w_ref[:, p * OUT_PIECE:(p + 1) * OUT_PIECE],
                              preferred_element_type=F32))

    lhs_t, kw_t, w_rows, carry, m_new = [], [], [], [], []
    for h in heads:
        if h in ML_PIECE_HEADS[0]:
            out_piece()
        brow = cum_row[ML_HEADS + h:ML_HEADS + h + 1, :]
        irow = grow[h:h + 1, :]
        kcol = key_col[:, h:h + 1]
        m_st = m_ref[h][0:1, 0:1]

        d_log = jnp.where(causal, brow + kcol, -jnp.inf)
        inter_log = brow + m_st
        m_out = jnp.maximum(inter_log, jnp.max(d_log, axis=0, keepdims=True))
        d_w = jnp.exp(d_log - (m_out - log_scale))
        inter_w = jnp.exp(inter_log - m_out)
        scores = raw[h] * d_w
        den = jnp.sum(scores, axis=0, keepdims=True) + inter_w * qn[h]
        r = 1.0 / jnp.maximum(jnp.abs(den), jnp.exp(-m_out))
        p_t = (scores * r).astype(BF16)
        diag = jnp.where(eye, inter_w * r, 0.0).astype(BF16)
        lhs_t.append(jnp.concatenate([p_t, diag], axis=0))

        b_tot = brow[:, CHUNK - 1:CHUNK]
        w_log = b_tot - brow + irow
        m_nw = jnp.maximum(b_tot + m_st, jnp.max(w_log, axis=1, keepdims=True))
        w_s = jnp.exp(w_log - m_nw) * scale
        carry.append(jnp.exp(b_tot + m_st - m_nw))
        m_new.append(m_nw)
        kw_t.append((k[h].astype(F32).T * w_s).astype(BF16))
        w_rows.append(jnp.broadcast_to(w_s, (8, CHUNK)).astype(BF16))

    h_out = []
    for h in heads:
        rhs = jnp.concatenate([v[h], qc[h]], axis=0)
        h_out.append(lax.dot_general(lhs_t[h], rhs, TN_DIMS, preferred_element_type=F32))
        c_ref[h] = carry[h] * c_ref[h] + jnp.dot(kw_t[h], v[h], preferred_element_type=F32)
        n_ref[h] = carry[h] * n_ref[h] + jnp.dot(w_rows[h], k[h], preferred_element_type=F32)
        m_ref[h] = jnp.broadcast_to(m_new[h], m_ref.shape[1:])

    for h in heads:
        if h in ML_PIECE_HEADS[1]:
            out_piece()
        cols = slice(h * ML_DV, (h + 1) * ML_DV)
        y = h_out[h]
        yn = y * lax.rsqrt(jnp.mean(y * y, axis=-1, keepdims=True) + EPS) * ng_ref[:, cols]
        gate = jax.nn.sigmoid(p_ref[:, o0 + h * ML_DV:o0 + (h + 1) * ML_DV])
        y_ref[slot, :, cols] = yn.astype(BF16) * gate

    out = jnp.concatenate(pieces, axis=1)
    out_ref[...] = x_ref[...] + _rms(out, gain_ref[...])


def ml_mix_out(proj, gcol, grow, head_g, x, w_out, norm_g, g_idx, cast_jobs, batch, seq):
    assert sum(map(len, ML_PIECE_HEADS)) * OUT_PIECE == D_MODEL
    m = batch * seq
    n_chunks = m // CHUNK
    cur = lambda t: jnp.minimum(t, n_chunks - 1)
    prev = lambda t: jnp.maximum(t - 1, 0)
    assert CAST_BLOCKS <= n_chunks + 1
    c_in, c_out, c_shape, c_args = _cast_specs(cast_jobs, CAST_BLOCKS, lambda t: t)
    return pl.pallas_call(
        functools.partial(_ml_kernel, nc=seq // CHUNK, n_cast=len(cast_jobs)),
        grid=(n_chunks + 1,),
        in_specs=[pl.BlockSpec((CHUNK, ML_MAIN), lambda t: (cur(t), 0)),
                  pl.BlockSpec((CHUNK, GATE_LANES), lambda t: (cur(t), 0)),
                  pl.BlockSpec((2 * ML_HEADS, CHUNK), lambda t: (0, cur(t))),
                  pl.BlockSpec((1, ML_V), lambda t: (0, 0)),
                  pl.BlockSpec((CHUNK, D_MODEL), lambda t: (prev(t), 0)),
                  pl.BlockSpec((ML_V, D_MODEL), lambda t: (0, 0),
                               pipeline_mode=pl.Buffered(1)),
                  pl.BlockSpec((None, 1, D_MODEL), lambda t: (g_idx, 0, 0))] + c_in,
        out_specs=[pl.BlockSpec((CHUNK, D_MODEL), lambda t: (prev(t), 0))] + c_out,
        out_shape=[jax.ShapeDtypeStruct((m, D_MODEL), F32)] + c_shape,
        scratch_shapes=[pltpu.VMEM((ML_HEADS, ML_DQK, ML_DV), F32),
                        pltpu.VMEM((ML_HEADS, 8, ML_DQK), F32),
                        pltpu.VMEM((ML_HEADS, 8, 128), F32),
                        pltpu.VMEM((2, CHUNK, ML_V), BF16)],
        compiler_params=_params(("arbitrary",)),
        name="ml_mix_out",
    )(proj, gcol, grow, head_g, x, w_out, norm_g, *c_args)


def _mlp_kernel(xn_ref, xr_ref, g_in_ref, w1_ref, w2_ref, g_out_ref, o_ref,
                h_even, h_odd, acc_even, acc_odd, *, nt):
    i = pl.program_id(0)
    f = pl.program_id(1)
    n_rows = xn_ref.shape[0]
    rows = pl.ds(pl.multiple_of(f * n_rows, n_rows), n_rows)
    h_bufs = (h_even, h_odd)
    acc_bufs = (acc_even, acc_odd)

    def norm_rows(h_dst):
        h_dst[rows, :] = _rms(xn_ref[...], g_in_ref[...]).astype(BF16)

    def finish_rows(acc_src):
        o_ref[...] = xr_ref[...] + _rms(acc_src[rows, :], g_out_ref[...])

    def chunk(h_src, acc_dst, first):
        hid = jnp.maximum(jnp.dot(h_src[...], w1_ref[...], preferred_element_type=F32), 0.0)
        out = jnp.dot((hid * hid).astype(BF16), w2_ref[...], preferred_element_type=F32)
        if first:
            acc_dst[...] = out
        else:
            acc_dst[...] += out

    @pl.when(i == 0)
    def _():
        norm_rows(h_even)
        acc_odd[rows, :] = jnp.zeros((n_rows, acc_odd.shape[1]), F32)
        o_ref[...] = xr_ref[...]

    in_flight = jnp.logical_and(i >= 1, i <= nt)
    for parity in (0, 1):
        mine = jnp.logical_and(in_flight, (i - 1) % 2 == parity)

        @pl.when(jnp.logical_and(mine, f == 0))
        def _(parity=parity):
            finish_rows(acc_bufs[1 - parity])
            chunk(h_bufs[parity], acc_bufs[parity], True)
            norm_rows(h_bufs[1 - parity])

        @pl.when(jnp.logical_and(mine, f > 0))
        def _(parity=parity):
            finish_rows(acc_bufs[1 - parity])
            chunk(h_bufs[parity], acc_bufs[parity], False)
            norm_rows(h_bufs[1 - parity])

    @pl.when(i == nt + 1)
    def _():
        finish_rows(acc_bufs[(nt - 1) % 2])


def mlp(x, norm_g, g_in_idx, g_out_idx, w1, w2, tm=1024, tf=1024):
    m = x.shape[0]
    nt, nf = m // tm, D_FF // tf
    rows = tm // nf
    clamp = lambda t: jnp.clip(t, 0, nt - 1)
    chunk_of = lambda i, f: jnp.where(jnp.logical_and(i >= 1, i <= nt), f, 0)
    done_rows = lambda i, f: jnp.where(i >= 2, clamp(i - 2) * nf + f, 0)
    return pl.pallas_call(
        functools.partial(_mlp_kernel, nt=nt),
        grid=(nt + 2, nf),
        in_specs=[pl.BlockSpec((rows, D_MODEL), lambda i, f: (clamp(i) * nf + f, 0)),
                  pl.BlockSpec((rows, D_MODEL), lambda i, f: (done_rows(i, f), 0)),
                  pl.BlockSpec((None, 1, D_MODEL), lambda i, f: (g_in_idx, 0, 0)),
                  pl.BlockSpec((D_MODEL, tf), lambda i, f: (0, chunk_of(i, f))),
                  pl.BlockSpec((tf, D_MODEL), lambda i, f: (chunk_of(i, f), 0)),
                  pl.BlockSpec((None, 1, D_MODEL), lambda i, f: (g_out_idx, 0, 0))],
        out_specs=pl.BlockSpec((rows, D_MODEL), lambda i, f: (done_rows(i, f), 0)),
        out_shape=jax.ShapeDtypeStruct((m, D_MODEL), F32),
        scratch_shapes=[pltpu.VMEM((tm, D_MODEL), BF16), pltpu.VMEM((tm, D_MODEL), BF16),
                        pltpu.VMEM((tm, D_MODEL), F32), pltpu.VMEM((tm, D_MODEL), F32)],
        compiler_params=_params(("arbitrary", "arbitrary")),
        name="mlp",
    )(x, x, norm_g, w1, w2, norm_g)


def kernel(x, positions, norm_g, ret_w_in, ret_w_out, mlstm_w_in, mlstm_b_gate,
           mlstm_norm_g, mlstm_w_out, mlp_w1, mlp_w2):
    batch, seq, d = x.shape
    m = batch * seq
    xf = x.reshape(m, d)

    inv_freq = jnp.power(ROPE_BASE, -jnp.linspace(0.0, 1.0, RET_DK // 2, dtype=F32))
    cos, sin, ret_w_in_b = rope_tables(positions, inv_freq, [(ret_w_in, 0)])
    gains = norm_g.astype(F32).reshape(DEPTH * 4, 1, d)
    ml_w_in_b = mlstm_w_in.astype(BF16)
    n_gate = 2 * ML_HEADS

    for i in range(DEPTH):
        j = i // 2
        if i % 2 == 0:
            proj, w1_b, w2_b, w_out_b = ret_proj(
                xf, gains, 4 * i, ret_w_in_b, cos, sin,
                [(mlp_w1, i), (mlp_w2, i), (ret_w_out, j)])
            xf = ret_mix_out(proj, xf, w_out_b, gains, 4 * i + 1, batch, seq)
        else:
            w_gate = ml_w_in_b[j, :, ML_MAIN:]
            wg = jnp.pad(w_gate, ((0, 0), (0, GATE_LANES - n_gate)))
            bias = mlstm_b_gate[j].astype(F32)
            b = jnp.pad(bias, (0, GATE_LANES - n_gate)).reshape(1, GATE_LANES)
            proj, gcol, grow, w_out_b = ml_proj(xf, gains, 4 * i, ml_w_in_b, j, wg, b,
                                                [(mlstm_w_out, j)])
            head_g = mlstm_norm_g[j].reshape(1, ML_V).astype(F32)
            jobs = [(mlp_w1, i), (mlp_w2, i)]
            if i + 1 < DEPTH:
                jobs.append((ret_w_in, j + 1))
            xf, w1_b, w2_b, *nxt = ml_mix_out(proj, gcol, grow, head_g, xf, w_out_b, gains,
                                              4 * i + 1, jobs, batch, seq)
            if nxt:
                ret_w_in_b = nxt[0]
        xf = mlp(xf, gains, 4 * i + 2, 4 * i + 3, w1_b, w2_b)
    return xf.reshape(batch, seq, d)
```
